```python
import math
import jax, jax.numpy as jnp
from jax import lax
import numpy as np

D_MODEL = 1024
BATCH = 2
SEQ = 8192
DEPTH = 2
DEC_BATCH = 128
DEC_SEQ = 4
PAST_LEN = 2048
PAGE_SIZE = 128

H_A = 8
G_A = 2
DH_A = 64
R_A = H_A // G_A
Q_A = H_A * DH_A
KV_A = 3 * 2 * G_A * DH_A
CMP_BLOCK = 32
SEL_BLOCK = 64
TOP_N = 16
WINDOW = 512
Q_BLOCK = 128
FORCE_BONUS = 1.0e3
NEG_INF = -1.0e30
D_RNN = D_MODEL // 2
RG_BLOCKS = 8
RG_BW = D_RNN // RG_BLOCKS
RG_C = 8.0
CONV_W = 4
IN_A = Q_A + KV_A + 3 * H_A + 2 * D_RNN
OUT_A = Q_A + D_RNN
H_C = 8
DK_C = 128
DV_C = 128
D_C = H_C * DV_C
QKV_C = 2 * H_C * DK_C + H_C * DV_C
IN_C = QKV_C + D_C + 2 * H_C
GDN_CHUNK = 64
P_HEADS = 8
N_KEYS = 128
N_EXP = N_KEYS * N_KEYS
P_DKEY = 256
P_DHALF = P_DKEY // 2
P_TOPK = 16
P_BLOCK = 256
EPS = 1e-6
N_A_LAYERS = (DEPTH + 1) // 2
N_C_LAYERS = DEPTH // 2

kernel_name = "nsa_rglru_gdn_peer_hybrid_step"


def rmsnorm(x, g):
    xf = x.astype(jnp.float32)
    y = xf * lax.rsqrt(jnp.mean(xf * xf, -1, keepdims=True) + EPS) * g.astype(jnp.float32)
    return y.astype(x.dtype)


def l2norm(x):
    return x * lax.rsqrt(jnp.sum(x * x, -1, keepdims=True) + 1e-6)


def causal_dwconv(x, buf, w):
    T = x.shape[1]
    xc = jnp.concatenate([buf.astype(x.dtype), x], axis=1)
    out = sum(xc[:, j:j + T] * w[j] for j in range(CONV_W))
    return out, xc[:, -(CONV_W - 1):]


def split_even(z):
    B, T, _ = z.shape
    i0 = Q_A
    i1 = i0 + KV_A
    i2 = i1 + 3 * H_A
    i3 = i2 + D_RNN
    q = z[..., :i0].reshape(B, T, H_A, DH_A)
    kv = z[..., i0:i1].reshape(B, T, 3, 2, G_A, DH_A)
    gates = jax.nn.sigmoid(z[..., i1:i2].astype(jnp.float32)).reshape(B, T, H_A, 3)
    return q, kv, gates, z[..., i2:i3], z[..., i3:]


def nsa_blocks(kv_cmp, kv_sel, w_pool):
    B, L = kv_cmp.shape[:2]
    Lp = -(-L // SEL_BLOCK) * SEL_BLOCK
    padw = [(0, 0), (0, Lp - L), (0, 0), (0, 0), (0, 0)]
    cb = jnp.pad(kv_cmp, padw).reshape(B, Lp // CMP_BLOCK, CMP_BLOCK, 2, G_A, DH_A)
    kvc = jnp.einsum('bnckgd,kc->bnkgd', cb, w_pool.astype(cb.dtype))
    sb = jnp.pad(kv_sel, padw).reshape(B, Lp // SEL_BLOCK, SEL_BLOCK, 2, G_A, DH_A)
    return kvc[:, :, 0], kvc[:, :, 1], sb[:, :, :, 0], sb[:, :, :, 1]


def nsa_core(q, qpos, kc, vc, ks, vs, kw, vw, wpos, gates):
    f32 = jnp.float32
    B, Tq = q.shape[:2]
    dt = q.dtype
    scale = DH_A ** -0.5
    qg = q.reshape(B, Tq, G_A, R_A, DH_A)
    nc = kc.shape[1]
    c_end = (jnp.arange(nc) + 1) * CMP_BLOCK - 1
    cmask = (c_end[None, :] <= qpos[:, None])[None, :, None, None, :]
    s_c = jnp.einsum('btgrd,bngd->btgrn', qg, kc).astype(f32) * scale
    p_c = jax.nn.softmax(jnp.where(cmask, s_c, NEG_INF), -1) * cmask
    o_c = jnp.einsum('btgrn,bngd->btgrd', p_c.astype(dt), vc).astype(f32)
    nsel = ks.shape[1]
    imp = p_c.sum(3).reshape(B, Tq, G_A, nsel, SEL_BLOCK // CMP_BLOCK).sum(-1)
    blk = jnp.arange(nsel)
    valid = (blk[None, :] * SEL_BLOCK <= qpos[:, None])[None, :, None, :]
    forced = ((blk[None, :] == 0) | (blk[None, :] == (qpos // SEL_BLOCK)[:, None]))[None, :, None, :]
    score = jnp.where(valid, imp + jnp.where(forced, FORCE_BONUS, 0.0), -jnp.inf)
    _, idx = lax.top_k(score, min(TOP_N, nsel))
    bi = jnp.arange(B)[:, None, None, None]
    gi = jnp.arange(G_A)[None, None, :, None]
    kg = ks.transpose(0, 3, 1, 2, 4)[bi, gi, idx]
    vg = vs.transpose(0, 3, 1, 2, 4)[bi, gi, idx]
    kpos = idx[..., None] * SEL_BLOCK + jnp.arange(SEL_BLOCK)
    smask = (kpos <= qpos[None, :, None, None, None])[:, :, :, None]
    s_s = jnp.einsum('btgrd,btgnsd->btgrns', qg, kg).astype(f32) * scale
    s_s = jnp.where(smask, s_s, NEG_INF)
    p_s = jax.nn.softmax(s_s.reshape(B, Tq, G_A, R_A, -1), -1).reshape(s_s.shape)
    o_s = jnp.einsum('btgrns,btgnsd->btgrd', p_s.astype(dt), vg).astype(f32)
    d = qpos[:, None] - wpos[None, :]
    wmask = ((d >= 0) & (d <= WINDOW) & (wpos[None, :] >= 0))[None, :, None, None, :]
    s_w = jnp.einsum('btgrd,blgd->btgrl', qg, kw).astype(f32) * scale
    p_w = jax.nn.softmax(jnp.where(wmask, s_w, NEG_INF), -1)
    o_w = jnp.einsum('btgrl,blgd->btgrd', p_w.astype(dt), vw).astype(f32)
    gt = gates.astype(f32).reshape(B, Tq, G_A, R_A, 3)
    o = gt[..., 0:1] * o_c + gt[..., 1:2] * o_s + gt[..., 2:3] * o_w
    return o.reshape(B, Tq, Q_A).astype(dt)


def nsa_prompt(q, kv, gates, w_pool):
    B, T = q.shape[:2]
    kc, vc, ks, vs = nsa_blocks(kv[:, :, 0], kv[:, :, 1], w_pool)
    kvw = jnp.pad(kv[:, :, 2], [(0, 0), (WINDOW, 0), (0, 0), (0, 0), (0, 0)])
    nq = T // Q_BLOCK
    qb = q.reshape(B, nq, Q_BLOCK, H_A, DH_A).swapaxes(0, 1)
    gb = gates.reshape(B, nq, Q_BLOCK, H_A, 3).swapaxes(0, 1)

    def one(args):
        i, qi, gi = args
        start = i * Q_BLOCK
        qpos = start + jnp.arange(Q_BLOCK)
        slab = lax.dynamic_slice_in_dim(kvw, start, WINDOW + Q_BLOCK, axis=1)
        wpos = start - WINDOW + jnp.arange(WINDOW + Q_BLOCK)
        return nsa_core(qi, qpos, kc, vc, ks, vs, slab[:, :, 0], slab[:, :, 1], wpos, gi)

    out = lax.map(one, (jnp.arange(nq), qb, gb))
    return out.swapaxes(0, 1).reshape(B, T, Q_A)


def nsa_sample(q, kv, gates, w_pool, cmp_pages, sel_pages, win_buf, page_table):
    B, T = q.shape[:2]

    def past(pages):
        return pages[page_table].reshape(B, -1, 2, G_A, DH_A)

    kv_cmp = jnp.concatenate([past(cmp_pages), kv[:, :, 0]], 1)
    kv_sel = jnp.concatenate([past(sel_pages), kv[:, :, 1]], 1)
    kc, vc, ks, vs = nsa_blocks(kv_cmp, kv_sel, w_pool)
    wbuf = win_buf.shape[1]
    kvw = jnp.concatenate([win_buf, kv[:, :, 2]], 1)
    qpos = PAST_LEN + jnp.arange(T)
    wpos = PAST_LEN - wbuf + jnp.arange(wbuf + T)
    out = nsa_core(q, qpos, kc, vc, ks, vs, kvw[:, :, 0], kvw[:, :, 1], wpos, gates)
    return out, kvw[:, T:]


def rglru(xr, yg, conv_buf, h0, w_conv, b_conv, w_a, b_a, w_x, b_x, lam):
    f32 = jnp.float32
    B, T, _ = xr.shape
    xc, new_buf = causal_dwconv(xr, conv_buf, w_conv)
    xf = (xc + b_conv).astype(f32)
    xb = xf.reshape(B, T, RG_BLOCKS, RG_BW)
    r = jax.nn.sigmoid(jnp.einsum('btnd,nde->btne', xb, w_a.astype(f32)).reshape(B, T, D_RNN) + b_a.astype(f32))
    i = jax.nn.sigmoid(jnp.einsum('btnd,nde->btne', xb, w_x.astype(f32)).reshape(B, T, D_RNN) + b_x.astype(f32))
    log_a = -RG_C * r * jax.nn.softplus(-lam.astype(f32))
    a = jnp.exp(log_a)
    bt = jnp.sqrt(-jnp.expm1(2.0 * log_a)) * (i * xf)
    bt = bt.at[:, 0].add(a[:, 0] * h0.astype(f32))

    def comb(l, rr):
        return (l[0] * rr[0], rr[0] * l[1] + rr[1])

    _, h = lax.associative_scan(comb, (a, bt), axis=1)
    out = h * jax.nn.gelu(yg.astype(f32))
    return out.astype(xr.dtype), h[:, -1], new_buf


def gated_delta_chunked(q, k, v, beta, g, S0):
    B, T, H, _ = q.shape
    c = min(GDN_CHUNK, T)
    nC = -(-T // c)
    pad = nC * c - T

    def blocks(x):
        x = jnp.pad(x, [(0, 0), (0, pad)] + [(0, 0)] * (x.ndim - 2))
        x = x.reshape((B, nC, c) + x.shape[2:])
        return x.transpose((1, 0, 3, 2) + tuple(range(4, x.ndim)))

    q, k, v, beta, g = (blocks(x) for x in (q, k, v, beta, g))
    gc = jnp.cumsum(g, -1)
    tri = jnp.tril(jnp.ones((c, c), bool))
    strict = jnp.tril(jnp.ones((c, c), bool), -1)
    diff = gc[..., :, None] - gc[..., None, :]
    Lmat = jnp.where(tri, jnp.exp(jnp.where(tri, diff, 0.0)), 0.0)
    kb = k * beta[..., None]
    M = jnp.where(strict, jnp.einsum('...id,...jd->...ij', kb, k) * Lmat, 0.0)
    eye = jnp.eye(c, dtype=jnp.float32)
    Tinv = lax.linalg.triangular_solve(eye + M, jnp.broadcast_to(eye, M.shape), left_side=True, lower=True, unit_diagonal=True)
    u = Tinv @ (v * beta[..., None])
    w = Tinv @ (kb * jnp.exp(gc)[..., None])
    Aqk = jnp.einsum('...id,...jd->...ij', q, k) * Lmat
    qd = q * jnp.exp(gc)[..., None]
    kd = k * jnp.exp(gc[..., -1:] - gc)[..., None]
    glast = jnp.exp(gc[..., -1])

    def step(S, xs):
        u_c, w_c, qd_c, kd_c, A_c, gl_c = xs
        v_new = u_c - jnp.einsum('bhcd,bhde->bhce', w_c, S)
        o = jnp.einsum('bhcd,bhde->bhce', qd_c, S) + jnp.einsum('bhij,bhje->bhie', A_c, v_new)
        S = S * gl_c[..., None, None] + jnp.einsum('bhcd,bhce->bhde', kd_c, v_new)
        return S, o

    S, o = lax.scan(step, S0, (u, w, qd, kd, Aqk, glast))
    o = o.transpose(1, 0, 3, 2, 4).reshape(B, nC * c, H, -1)[:, :T]
    return o, S


def gdn(h, conv_buf, S0, w_in, w_conv, A_log, dt_bias, g_norm, w_out):
    f32 = jnp.float32
    B, T, _ = h.shape
    z_all = h @ w_in
    qkv = z_all[..., :QKV_C]
    z = z_all[..., QKV_C:QKV_C + D_C]
    b_logit = z_all[..., QKV_C + D_C:QKV_C + D_C + H_C]
    a_logit = z_all[..., QKV_C + D_C + H_C:]
    qkv_c, new_buf = causal_dwconv(qkv, conv_buf, w_conv)
    qkv_c = jax.nn.silu(qkv_c.astype(f32))
    q = l2norm(qkv_c[..., :H_C * DK_C].reshape(B, T, H_C, DK_C)) * DK_C ** -0.5
    k = l2norm(qkv_c[..., H_C * DK_C:2 * H_C * DK_C].reshape(B, T, H_C, DK_C))
    v = qkv_c[..., 2 * H_C * DK_C:].reshape(B, T, H_C, DV_C)
    beta = jax.nn.sigmoid(b_logit.astype(f32))
    g = -jnp.exp(A_log.astype(f32)) * jax.nn.softplus(a_logit.astype(f32) + dt_bias.astype(f32))
    o, S_new = gated_delta_chunked(q, k, v, beta, g, S0.astype(f32))
    o = o * lax.rsqrt(jnp.mean(o * o, -1, keepdims=True) + EPS) * g_norm.astype(f32)
    o = o * jax.nn.silu(z.reshape(B, T, H_C, DV_C).astype(f32))
    y = o.reshape(B, T, D_C).astype(h.dtype) @ w_out
    return y, S_new, new_buf


def peer(h, w_q, keys, w_u, w_v):
    shp = h.shape
    x2 = h.reshape(-1, D_MODEL)
    n = x2.shape[0]
    nb = -(-n // P_BLOCK)
    x2 = jnp.pad(x2, ((0, nb * P_BLOCK - n), (0, 0))).reshape(nb, P_BLOCK, D_MODEL)

    def one(xb):
        q = (xb @ w_q).reshape(P_BLOCK, P_HEADS, 2, P_DHALF)
        s = jnp.einsum('thcd,hcnd->thcn', q, keys).astype(jnp.float32)
        s1, i1 = lax.top_k(s[:, :, 0], P_TOPK)
        s2, i2 = lax.top_k(s[:, :, 1], P_TOPK)
        comb = (s1[..., :, None] + s2[..., None, :]).reshape(P_BLOCK, P_HEADS, P_TOPK * P_TOPK)
        sc, ci = lax.top_k(comb, P_TOPK)
        e = jnp.take_along_axis(i1, ci // P_TOPK, -1) * N_KEYS + jnp.take_along_axis(i2, ci % P_TOPK, -1)
        gate = jax.nn.softmax(sc, -1)
        act = jax.nn.gelu(jnp.einsum('td,thkd->thk', xb, w_u[e]).astype(jnp.float32))
        return jnp.einsum('thk,thkd->td', (gate * act).astype(xb.dtype), w_v[e])

    out = lax.map(one, x2).reshape(-1, D_MODEL)[:n]
    return out.reshape(shp)


def setup_inputs(seed: int = 0) -> dict:
    f32 = jnp.float32
    key = jax.random.key(seed)
    keys = iter(jax.random.split(key, 48))

    def nrm(shape, s):
        return jax.random.normal(next(keys), shape, f32) * s

    NA, NC = N_A_LAYERS, N_C_LAYERS
    n_pages = PAST_LEN // PAGE_SIZE
    n_used = DEC_BATCH * n_pages
    n_pool = n_used + max(1, n_used // 4)
    wbuf = min(WINDOW, PAST_LEN)
    page_table = jax.random.permutation(next(keys), n_pool)[:n_used].reshape(DEC_BATCH, n_pages).astype(jnp.int32)
    u = jax.random.uniform(next(keys), (NA, D_RNN), f32, 0.9, 0.999) ** (1.0 / RG_C)
    rg_lambda = jnp.log(u) - jnp.log1p(-u)
    dt = jnp.exp(jax.random.uniform(next(keys), (NC, H_C), f32, math.log(1e-3), math.log(1e-1)))
    c_dt_bias = dt + jnp.log(-jnp.expm1(-dt))
    c_A_log = jnp.log(jax.random.uniform(next(keys), (NC, H_C), f32, 1.0, 16.0))
    return {
        "x_prompt": nrm((BATCH, SEQ, D_MODEL), 1.0),
        "x_sample": nrm((DEC_BATCH, DEC_SEQ, D_MODEL), 1.0),
        "cache_cmp_kv": nrm((NA, n_pool, PAGE_SIZE, 2, G_A, DH_A), 1.0),
        "cache_sel_kv": nrm((NA, n_pool, PAGE_SIZE, 2, G_A, DH_A), 1.0),
        "cache_win_kv": nrm((NA, DEC_BATCH, wbuf, 2, G_A, DH_A), 1.0),
        "state_rg_h": nrm((NA, DEC_BATCH, D_RNN), 0.5),
        "state_rg_conv": nrm((NA, DEC_BATCH, CONV_W - 1, D_RNN), 1.0),
        "state_gdn_S": nrm((NC, DEC_BATCH, H_C, DK_C, DV_C), 0.1),
        "state_gdn_conv": nrm((NC, DEC_BATCH, CONV_W - 1, QKV_C), 1.0),
        "page_table": page_table,
        "norm_mix": 1.0 + nrm((DEPTH, D_MODEL), 0.02),
        "norm_ffn": 1.0 + nrm((DEPTH, D_MODEL), 0.02),
        "norm_final": 1.0 + nrm((D_MODEL,), 0.02),
        "w_in_a": nrm((NA, D_MODEL, IN_A), D_MODEL ** -0.5),
        "w_cmp_pool": 1.0 / CMP_BLOCK + nrm((NA, 2, CMP_BLOCK), 0.01),
        "w_rg_conv": nrm((NA, CONV_W, D_RNN), 0.5),
        "b_rg_conv": nrm((NA, D_RNN), 0.02),
        "w_rg_a": nrm((NA, RG_BLOCKS, RG_BW, RG_BW), RG_BW ** -0.5),
        "b_rg_a": nrm((NA, D_RNN), 0.02),
        "w_rg_x": nrm((NA, RG_BLOCKS, RG_BW, RG_BW), RG_BW ** -0.5),
        "b_rg_x": nrm((NA, D_RNN), 0.02),
        "rg_lambda": rg_lambda,
        "w_out_a": nrm((NA, OUT_A, D_MODEL), OUT_A ** -0.5),
        "w_in_c": nrm((NC, D_MODEL, IN_C), D_MODEL ** -0.5),
        "w_c_conv": nrm((NC, CONV_W, QKV_C), 0.5),
        "c_A_log": c_A_log,
        "c_dt_bias": c_dt_bias,
        "c_norm": 1.0 + nrm((NC, DV_C), 0.02),
        "w_out_c": nrm((NC, D_C, D_MODEL), D_C ** -0.5),
        "peer_w_q": nrm((DEPTH, D_MODEL, P_HEADS * P_DKEY), D_MODEL ** -0.5),
        "peer_keys": nrm((DEPTH, P_HEADS, 2, N_KEYS, P_DHALF), P_DHALF ** -0.5),
        "peer_u": nrm((DEPTH, N_EXP, D_MODEL), D_MODEL ** -0.5),
        "peer_v": nrm((DEPTH, N_EXP, D_MODEL), 0.2),
    }


def reference(x_prompt, x_sample, cache_cmp_kv, cache_sel_kv, cache_win_kv, state_rg_h, state_rg_conv,
              state_gdn_S, state_gdn_conv, page_table, norm_mix, norm_ffn, norm_final, w_in_a, w_cmp_pool,
              w_rg_conv, b_rg_conv, w_rg_a, b_rg_a, w_rg_x, b_rg_x, rg_lambda, w_out_a, w_in_c, w_c_conv,
              c_A_log, c_dt_bias, c_norm, w_out_c, peer_w_q, peer_keys, peer_u, peer_v):
    f32 = jnp.float32
    xp, xs = x_prompt, x_sample
    B, T = xp.shape[:2]
    (cmp_p, cmp_s, sel_p, sel_s, win_p, win_s, rgh_p, rgh_s,
     rgc_p, rgc_s, gS_p, gS_s, gc_p, gc_s) = ([] for _ in range(14))
    for layer in range(DEPTH):
        hp = rmsnorm(xp, norm_mix[layer])
        hs = rmsnorm(xs, norm_mix[layer])
        if layer % 2 == 0:
            a = layer // 2
            qp, kvp, gtp, xrp, ygp = split_even(hp @ w_in_a[a])
            qs, kvs, gts, xrs, ygs = split_even(hs @ w_in_a[a])
            att_p = nsa_prompt(qp, kvp, gtp, w_cmp_pool[a])
            att_s, win_new = nsa_sample(qs, kvs, gts, w_cmp_pool[a], cache_cmp_kv[a], cache_sel_kv[a],
                                        cache_win_kv[a], page_table)
            rg = (w_rg_conv[a], b_rg_conv[a], w_rg_a[a], b_rg_a[a], w_rg_x[a], b_rg_x[a], rg_lambda[a])
            rec_p, hl_p, cb_p = rglru(xrp, ygp, jnp.zeros((B, CONV_W - 1, D_RNN), xp.dtype),
                                      jnp.zeros((B, D_RNN), f32), *rg)
            rec_s, hl_s, cb_s = rglru(xrs, ygs, state_rg_conv[a], state_rg_h[a], *rg)
            mix_p = jnp.concatenate([att_p, rec_p], -1) @ w_out_a[a]
            mix_s = jnp.concatenate([att_s, rec_s], -1) @ w_out_a[a]
            cmp_p.append(kvp[:, :, 0]); cmp_s.append(kvs[:, :, 0])
            sel_p.append(kvp[:, :, 1]); sel_s.append(kvs[:, :, 1])
            win_p.append(kvp[:, -min(WINDOW, T):, 2]); win_s.append(win_new)
            rgh_p.append(hl_p); rgh_s.append(hl_s)
            rgc_p.append(cb_p); rgc_s.append(cb_s)
        else:
            c = layer // 2
            gd = (w_in_c[c], w_c_conv[c], c_A_log[c], c_dt_bias[c], c_norm[c], w_out_c[c])
            mix_p, S_p, cb_p = gdn(hp, jnp.zeros((B, CONV_W - 1, QKV_C), xp.dtype),
                                   jnp.zeros((B, H_C, DK_C, DV_C), f32), *gd)
            mix_s, S_s, cb_s = gdn(hs, state_gdn_conv[c], state_gdn_S[c], *gd)
            gS_p.append(S_p); gS_s.append(S_s)
            gc_p.append(cb_p); gc_s.append(cb_s)
        xp = xp + mix_p.astype(xp.dtype)
        xs = xs + mix_s.astype(xs.dtype)
        pf = (peer_w_q[layer], peer_keys[layer], peer_u[layer], peer_v[layer])
        xp = xp + peer(rmsnorm(xp, norm_ffn[layer]), *pf).astype(xp.dtype)
        xs = xs + peer(rmsnorm(xs, norm_ffn[layer]), *pf).astype(xs.dtype)
    y_prompt = rmsnorm(xp, norm_final)
    y_sample = rmsnorm(xs, norm_final)
    return (y_prompt, y_sample, jnp.stack(cmp_p), jnp.stack(cmp_s), jnp.stack(sel_p), jnp.stack(sel_s),
            jnp.stack(win_p), jnp.stack(win_s), jnp.stack(rgh_p), jnp.stack(rgh_s), jnp.stack(rgc_p),
            jnp.stack(rgc_s), jnp.stack(gS_p), jnp.stack(gS_s), jnp.stack(gc_p), jnp.stack(gc_s))
```

```python
import functools
import math

import jax
import jax.numpy as jnp
from jax import lax
from jax.experimental import pallas as pl
from jax.experimental.pallas import tpu as pltpu

F32 = jnp.float32
BF16 = jnp.bfloat16

D_MODEL = 1024
H_A, G_A, R_A, DH_A = 8, 2, 4, 64
CMP_BLOCK, SEL_BLOCK, TOP_N, WINDOW, Q_BLOCK = 32, 64, 16, 512, 128
FORCE_BONUS = 1.0e3
NEG_INF = -1.0e30
N_SEL_LANES = 128
D_RNN, RG_C, CONV_W = 512, 8.0, 4
H_C, DK_C, DV_C = 8, 128, 128
QKV_C = 3 * H_C * DK_C
GDN_CHUNK = 128
P_HEADS, N_KEYS, P_DHALF, P_TOPK = 8, 128, 128, 16
N_EXP = N_KEYS * N_KEYS
EPS = 1e-6

VMEM_LIMIT = 56 * 1024 * 1024


def _cparams(*sem):
    return pltpu.CompilerParams(dimension_semantics=sem, vmem_limit_bytes=VMEM_LIMIT)


def _rmsnorm(x, g):
    return x * lax.rsqrt(jnp.mean(x * x, -1, keepdims=True) + EPS) * g


def _gelu(x):
    return 0.5 * x * (1.0 + jnp.tanh(math.sqrt(2.0 / math.pi) * (x + 0.044715 * (x * x * x))))


def _softplus(x):
    return jnp.maximum(x, 0.0) + jnp.log1p(jnp.exp(-jnp.abs(x)))


def _silu(x):
    return x * jax.nn.sigmoid(x)


def _neg_expm1(x):
    t = jnp.tanh(0.5 * x)
    return -2.0 * t / (1.0 - t)


def _dot(a, b):
    return jnp.dot(a, b, preferred_element_type=F32)


def _dot_nt(a, b):
    return lax.dot_general(a, b, (((1,), (1,)), ((), ())), preferred_element_type=F32)


def _dot_f32(a, b):
    return jnp.dot(a, b, preferred_element_type=F32, precision=lax.Precision.HIGHEST)


def _shift_rows(x, s, fill):
    rolled = pltpu.roll(x, s, 0)
    row = lax.broadcasted_iota(jnp.int32, x.shape, 0)
    return jnp.where(row >= s, rolled, fill)


def _topk_pos(s, axis, k):
    n = s.shape[axis]
    idx = lax.broadcasted_iota(jnp.int32, s.shape, axis).astype(F32)
    vshape = tuple(k if d == axis else s.shape[d] for d in range(s.ndim))
    vidx = lax.broadcasted_iota(jnp.int32, vshape, axis)
    pos = jnp.full(s.shape, float(k), F32)
    vals = jnp.zeros(vshape, F32)
    for i in range(k):
        m = jnp.max(s, axis=axis, keepdims=True)
        j = jnp.min(jnp.where(s == m, idx, float(n)), axis=axis, keepdims=True)
        oh = idx == j
        pos = jnp.where(oh, float(i), pos)
        s = jnp.where(oh, -jnp.inf, s)
        vals = jnp.where(vidx == i, m, vals)
    return pos, vals


def _norm_proj_body(x_ref, g_ref, w_ref, *out_refs, outs):
    h = _rmsnorm(x_ref[...], g_ref[...]).astype(BF16)
    for o_ref, (off, width, act) in zip(out_refs, outs):
        z = _dot(h, w_ref[:, off:off + width])
        if act == "sigmoid":
            z = jax.nn.sigmoid(z)
        o_ref[...] = z.astype(o_ref.dtype)


def _norm_proj(x, g, w, outs, dtypes, tm):
    n, d = x.shape
    body = functools.partial(_norm_proj_body, outs=outs)
    return pl.pallas_call(
        body,
        grid=(n // tm,),
        in_specs=[pl.BlockSpec((tm, d), lambda i: (i, 0)),
                  pl.BlockSpec((1, d), lambda i: (0, 0)),
                  pl.BlockSpec(w.shape, lambda i: (0, 0))],
        out_specs=[pl.BlockSpec((tm, wd), lambda i: (i, 0)) for (_, wd, _) in outs],
        out_shape=[jax.ShapeDtypeStruct((n, wd), dt) for (_, wd, _), dt in zip(outs, dtypes)],
        compiler_params=_cparams("parallel"),
        name="norm_proj",
    )(x, g.reshape(1, d), w)


def _out_proj_body(*refs, n_in):
    a_refs, w_refs = refs[:n_in], refs[n_in:2 * n_in]
    x_ref, o_ref = refs[2 * n_in], refs[2 * n_in + 1]
    acc = x_ref[...]
    for a, w in zip(a_refs, w_refs):
        acc = acc + _dot(a[...], w[...])
    o_ref[...] = acc


def _out_proj(acts, ws, x, tm):
    n, d = x.shape
    n_in = len(acts)
    return pl.pallas_call(
        functools.partial(_out_proj_body, n_in=n_in),
        grid=(n // tm,),
        in_specs=([pl.BlockSpec((tm, a.shape[1]), lambda i: (i, 0)) for a in acts]
                  + [pl.BlockSpec(w.shape, lambda i: (0, 0)) for w in ws]
                  + [pl.BlockSpec((tm, d), lambda i: (i, 0))]),
        out_specs=pl.BlockSpec((tm, d), lambda i: (i, 0)),
        out_shape=jax.ShapeDtypeStruct((n, d), F32),
        compiler_params=_cparams("parallel"),
        name="out_proj",
    )(*acts, *ws, x)


def _rg_gates(xf, wa, ba, wx, bx, lam):
    xb = xf.astype(BF16)
    r = jax.nn.sigmoid(_dot(xb, wa) + ba)
    i = jax.nn.sigmoid(_dot(xb, wx) + bx)
    log_a = -RG_C * r * _softplus(-lam)
    a = jnp.exp(log_a)
    b = jnp.sqrt(_neg_expm1(2.0 * log_a)) * (i * xf)
    return a, b


def _rglru_prompt_body(xr_ref, yg_ref, wc_ref, bc_ref, wa_ref, ba_ref, wx_ref, bx_ref, lam_ref,
                       rec_ref, hl_ref, xc_scr, h_scr, *, rt):
    t = pl.program_id(1)

    @pl.when(t == 0)
    def _():
        xc_scr[0:8, :] = jnp.zeros((8, D_RNN), F32)
        h_scr[...] = jnp.zeros_like(h_scr)

    xc_scr[8:8 + rt, :] = xr_ref[0]
    wc = wc_ref[...]
    conv = wc[0:1] * xc_scr[pl.ds(5, rt), :]
    for j in range(1, CONV_W):
        conv = conv + wc[j:j + 1] * xc_scr[pl.ds(5 + j, rt), :]
    xc_scr[0:8, :] = xc_scr[rt:rt + 8, :]
    xf = conv + bc_ref[...]
    a, b = _rg_gates(xf, wa_ref[...], ba_ref[...], wx_ref[...], bx_ref[...], lam_ref[...])
    s = 1
    while s < rt:
        b = a * _shift_rows(b, s, 0.0) + b
        a = a * _shift_rows(a, s, 1.0)
        s *= 2
    h = a * h_scr[0:1, :] + b
    h_last = h[rt - 1:rt, :]
    h_scr[0:1, :] = h_last
    hl_ref[0] = h_last
    rec_ref[0] = (h * _gelu(yg_ref[0])).astype(rec_ref.dtype)


def _rglru_prompt(xr, yg, wc, bc, wa, ba, wx, bx, lam, rt=256):
    b, t, d = xr.shape
    vec = lambda: pl.BlockSpec((1, d), lambda i, j: (0, 0))
    mat = lambda: pl.BlockSpec((d, d), lambda i, j: (0, 0))
    rec, hl = pl.pallas_call(
        functools.partial(_rglru_prompt_body, rt=rt),
        grid=(b, t // rt),
        in_specs=[pl.BlockSpec((1, rt, d), lambda i, j: (i, j, 0)),
                  pl.BlockSpec((1, rt, d), lambda i, j: (i, j, 0)),
                  pl.BlockSpec((CONV_W, d), lambda i, j: (0, 0)),
                  vec(), mat(), vec(), mat(), vec(), vec()],
        out_specs=[pl.BlockSpec((1, rt, d), lambda i, j: (i, j, 0)),
                   pl.BlockSpec((1, 1, d), lambda i, j: (i, 0, 0))],
        out_shape=[jax.ShapeDtypeStruct((b, t, d), BF16),
                   jax.ShapeDtypeStruct((b, 1, d), F32)],
        scratch_shapes=[pltpu.VMEM((rt + 8, d), F32), pltpu.VMEM((8, d), F32)],
        compiler_params=_cparams("arbitrary", "arbitrary"),
        name="rglru_prompt",
    )(xr, yg, wc, bc, wa, ba, wx, bx, lam)
    return rec, hl[:, 0]


def _rglru_sample_body(xr_ref, yg_ref, buf_ref, h0_ref, wc_ref, bc_ref, wa_ref, ba_ref, wx_ref,
                       bx_ref, lam_ref, rec_ref, hl_ref, *, t_len):
    xs = [buf_ref[j] for j in range(CONV_W - 1)] + [xr_ref[j] for j in range(t_len)]
    wc = wc_ref[...]
    h = h0_ref[...]
    for t in range(t_len):
        conv = wc[0:1] * xs[t]
        for j in range(1, CONV_W):
            conv = conv + wc[j:j + 1] * xs[t + j]
        xf = conv + bc_ref[...]
        a, b = _rg_gates(xf, wa_ref[...], ba_ref[...], wx_ref[...], bx_ref[...], lam_ref[...])
        h = a * h + b
        rec_ref[t] = h * _gelu(yg_ref[t])
    hl_ref[...] = h


def _rglru_sample(xr, yg, buf, h0, wc, bc, wa, ba, wx, bx, lam):
    t_len, b, d = xr.shape
    return pl.pallas_call(
        functools.partial(_rglru_sample_body, t_len=t_len),
        out_shape=[jax.ShapeDtypeStruct((t_len, b, d), F32),
                   jax.ShapeDtypeStruct((b, d), F32)],
        compiler_params=pltpu.CompilerParams(vmem_limit_bytes=VMEM_LIMIT),
        name="rglru_sample",
    )(xr, yg, buf, h0, wc, bc, wa, ba, wx, bx, lam)


def _pool_blocks(x3, w):
    first = jnp.sum(x3[:, :CMP_BLOCK, :] * w[None], axis=1)
    second = jnp.sum(x3[:, CMP_BLOCK:, :] * w[None], axis=1)
    return first, second


def _nsa_pool_body(x_ref, w_ref, pe_ref, po_ref):
    x = x_ref[0]
    nb = x.shape[0] // SEL_BLOCK
    pe, po = _pool_blocks(x.reshape(nb, SEL_BLOCK, x.shape[1]), w_ref[...])
    pe_ref[0] = pe
    po_ref[0] = po


def _nsa_pool(kv, w, rows=1024):
    b, t, _ = kv.shape
    nb = rows // SEL_BLOCK
    return pl.pallas_call(
        _nsa_pool_body,
        grid=(b, t // rows),
        in_specs=[pl.BlockSpec((1, rows, 256), lambda i, j: (i, j, 0)),
                  pl.BlockSpec((CMP_BLOCK, 256), lambda i, j: (0, 0))],
        out_specs=[pl.BlockSpec((1, nb, 256), lambda i, j: (i, j, 0))] * 2,
        out_shape=[jax.ShapeDtypeStruct((b, t // SEL_BLOCK, 256), F32)] * 2,
        compiler_params=_cparams("parallel", "parallel"),
        name="nsa_pool",
    )(kv, w)


def _cmp_branch(q, qpos, pe, po, nq):
    jl = lax.broadcasted_iota(jnp.int32, (1, N_SEL_LANES), 1)
    vis_e = (SEL_BLOCK * jl + (CMP_BLOCK - 1)) <= qpos
    vis_o = (SEL_BLOCK * jl + (SEL_BLOCK - 1)) <= qpos
    s_e = jnp.where(vis_e, _dot_nt(q, pe[:, :128].astype(BF16)), NEG_INF)
    s_o = jnp.where(vis_o, _dot_nt(q, po[:, :128].astype(BF16)), NEG_INF)
    m = jnp.maximum(jnp.max(s_e, -1, keepdims=True), jnp.max(s_o, -1, keepdims=True))
    p_e = jnp.where(vis_e, jnp.exp(s_e - m), 0.0)
    p_o = jnp.where(vis_o, jnp.exp(s_o - m), 0.0)
    den = jnp.sum(p_e, -1, keepdims=True) + jnp.sum(p_o, -1, keepdims=True)
    inv = jnp.where(den > 0.0, 1.0 / jnp.maximum(den, 1e-30), 0.0)
    p_e = p_e * inv
    p_o = p_o * inv
    o = _dot(p_e.astype(BF16), pe[:, 128:].astype(BF16)) + _dot(p_o.astype(BF16), po[:, 128:].astype(BF16))
    p = p_e + p_o
    imp = p[0:nq]
    for r in range(1, R_A):
        imp = imp + p[r * nq:(r + 1) * nq]
    return o, imp


def _select_blocks(imp, qpos, transposed):
    blk = lax.broadcasted_iota(jnp.int32, (1, N_SEL_LANES), 1)
    valid = blk * SEL_BLOCK <= qpos
    forced = jnp.logical_or(blk == 0, blk == (qpos >> 6))
    score = jnp.where(valid, imp + jnp.where(forced, FORCE_BONUS, 0.0), -jnp.inf)
    if transposed:
        pos, _ = _topk_pos(score.T, 0, TOP_N)
        sel = jnp.where(pos < TOP_N, 1.0, 0.0).T
    else:
        pos, _ = _topk_pos(score, 1, TOP_N)
        sel = jnp.where(pos < TOP_N, 1.0, 0.0)
    return jnp.where(valid, sel, 0.0)


def _expand_sel(sel, first_blk, nkeys):
    bi = lax.broadcasted_iota(jnp.int32, (N_SEL_LANES, nkeys), 0)
    ki = lax.broadcasted_iota(jnp.int32, (N_SEL_LANES, nkeys), 1)
    e = jnp.where(bi - first_blk == (ki >> 6), 1.0, 0.0).astype(BF16)
    x = _dot(sel.astype(BF16), e)
    return jnp.concatenate([x] * R_A, axis=0)


def _masked_attend(q, k, v, ok):
    s = jnp.where(ok, _dot_nt(q, k), NEG_INF)
    m = jnp.max(s, -1, keepdims=True)
    p = jnp.where(ok, jnp.exp(s - m), 0.0)
    l = jnp.sum(p, -1, keepdims=True)
    return _dot(p.astype(BF16), v) / l


def _window_ok(qpos, kpos):
    d = qpos - kpos
    return jnp.logical_and(d >= 0, d <= WINDOW)


def _gate_and_place(gt, branches, nq):
    heads = []
    for g in range(G_A):
        for r in range(R_A):
            h = g * R_A + r
            o = None
            for br in range(3):
                term = gt[:, 3 * h + br:3 * h + br + 1] * branches[g][br][r * nq:(r + 1) * nq]
                o = term if o is None else o + term
            if (h % 2) != g:
                o = pltpu.roll(o, DH_A, 1)
            heads.append(o)
    lane = lax.broadcasted_iota(jnp.int32, (1, 128), 1)
    return [jnp.where(lane < DH_A, heads[2 * j], heads[2 * j + 1]) for j in range(H_A // 2)]


def _nsa_prompt_body(q_ref, gt_ref, pe_ref, po_ref, kv_ref, o_ref):
    i = pl.program_id(1)
    nq = Q_BLOCK
    base = i * nq
    pe, po = pe_ref[0], po_ref[0]
    rowi = lax.broadcasted_iota(jnp.int32, (R_A * nq, 1), 0)
    qpos = base + (rowi & (nq - 1))
    branches = []
    for g in range(G_A):
        q = jnp.concatenate([q_ref[0, :, (g * R_A + r) * 128:(g * R_A + r + 1) * 128] for r in range(R_A)], axis=0)
        q = q * jnp.asarray(DH_A ** -0.5, BF16)
        o_c, imp = _cmp_branch(q, qpos, pe, po, nq)
        sel = _select_blocks(imp, qpos[0:nq], transposed=True)

        def chunk(c, carry, q=q, sel=sel):
            m, l, acc = carry
            k0 = pl.multiple_of(c * 256, 256)
            k = kv_ref[0, pl.ds(k0, 256), 0:128]
            v = kv_ref[0, pl.ds(k0, 256), 128:256]
            kpos = k0 + lax.broadcasted_iota(jnp.int32, (1, 256), 1)
            ok = jnp.logical_and(_expand_sel(sel, 4 * c, 256) > 0.5, kpos <= qpos)
            s = jnp.where(ok, _dot_nt(q, k), NEG_INF)
            m_new = jnp.maximum(m, jnp.max(s, -1, keepdims=True))
            alpha = jnp.exp(m - m_new)
            p = jnp.where(ok, jnp.exp(s - m_new), 0.0)
            l = alpha * l + jnp.sum(p, -1, keepdims=True)
            acc = alpha * acc + _dot(p.astype(BF16), v)
            return m_new, l, acc

        init = (jnp.full((R_A * nq, 1), NEG_INF, F32), jnp.zeros((R_A * nq, 1), F32),
                jnp.zeros((R_A * nq, 128), F32))
        _, l, acc = lax.fori_loop(0, i // 2 + 1, chunk, init)
        o_s = acc / l

        wlen = WINDOW + nq
        ws = pl.multiple_of(jnp.maximum(i - WINDOW // nq, 0) * nq, nq)
        kw = kv_ref[0, pl.ds(ws, wlen), 256:384]
        vw = kv_ref[0, pl.ds(ws, wlen), 384:512]
        kpos = ws + lax.broadcasted_iota(jnp.int32, (1, wlen), 1)
        o_w = _masked_attend(q, kw, vw, _window_ok(qpos, kpos))
        branches.append((o_c, o_s, o_w))
    for j, tile in enumerate(_gate_and_place(gt_ref[0], branches, nq)):
        o_ref[0, :, j * 128:(j + 1) * 128] = tile.astype(o_ref.dtype)


def _nsa_prompt(qpad, gates, pe, po, kvb):
    b, t, _ = qpad.shape
    nq = Q_BLOCK
    return pl.pallas_call(
        _nsa_prompt_body,
        grid=(b, t // nq),
        in_specs=[pl.BlockSpec((1, nq, H_A * 128), lambda i, j: (i, j, 0)),
                  pl.BlockSpec((1, nq, 128), lambda i, j: (i, j, 0)),
                  pl.BlockSpec((1, N_SEL_LANES, 256), lambda i, j: (i, 0, 0)),
                  pl.BlockSpec((1, N_SEL_LANES, 256), lambda i, j: (i, 0, 0)),
                  pl.BlockSpec((1, t, 512), lambda i, j: (i, 0, 0))],
        out_specs=pl.BlockSpec((1, nq, H_A * DH_A), lambda i, j: (i, j, 0)),
        out_shape=jax.ShapeDtypeStruct((b, t, H_A * DH_A), BF16),
        compiler_params=_cparams("parallel", "arbitrary"),
        name="nsa_prompt",
    )(qpad, gates, pe, po, kvb)


def _nsa_sample_body(pt_ref, q_ref, gt_ref, kvn_ref, w_ref, win_ref, *rest, n_pages, page, past, tq):
    del pt_ref
    cmp_pages, sel_pages = rest[:n_pages], rest[n_pages:2 * n_pages]
    o_ref, wout_ref, cmp_scr, sel_scr, win_scr = rest[2 * n_pages:]
    nq = q_ref.shape[1]
    npad = kvn_ref.shape[1]
    lp = cmp_scr.shape[0]
    wbuf = win_ref.shape[1]
    kvn = kvn_ref[0]
    for p in range(n_pages):
        cmp_scr[p * page:(p + 1) * page, :] = cmp_pages[p][0]
        sel_scr[p * page:(p + 1) * page, :] = sel_pages[p][0].astype(BF16)
    cmp_scr[past:past + npad, :] = kvn[:, 0:256]
    sel_scr[past:past + npad, :] = kvn[:, 256:512].astype(BF16)
    cmp_scr[past + npad:lp, :] = jnp.zeros((lp - past - npad, 256), F32)
    sel_scr[past + npad:lp, :] = jnp.zeros((lp - past - npad, 256), BF16)
    win_scr[0:wbuf, :] = win_ref[0]
    win_scr[wbuf:wbuf + npad, :] = kvn[:, 512:768]
    wl = win_scr.shape[0]
    win_scr[wbuf + npad:wl, :] = jnp.zeros((wl - wbuf - npad, 256), F32)
    wout_ref[0] = win_scr[pl.ds(tq, wbuf), :]

    nb = lp // SEL_BLOCK
    pe, po = _pool_blocks(cmp_scr[...].reshape(nb, SEL_BLOCK, 256), w_ref[...])
    zpad = jnp.zeros((N_SEL_LANES - nb, 256), F32)
    pe = jnp.concatenate([pe, zpad], axis=0)
    po = jnp.concatenate([po, zpad], axis=0)

    rowi = lax.broadcasted_iota(jnp.int32, (R_A * nq, 1), 0)
    qpos = past + (rowi & (nq - 1))
    nsel_keys = past + 256
    kpos_s = lax.broadcasted_iota(jnp.int32, (1, nsel_keys), 1)
    kpos_w = (past - wbuf) + lax.broadcasted_iota(jnp.int32, (1, wl), 1)
    branches = []
    for g in range(G_A):
        q = jnp.concatenate([q_ref[0, :, (g * R_A + r) * 128:(g * R_A + r + 1) * 128] for r in range(R_A)], axis=0)
        q = (q * DH_A ** -0.5).astype(BF16)
        o_c, imp = _cmp_branch(q, qpos, pe, po, nq)
        sel = _select_blocks(imp, qpos[0:nq], transposed=False)
        ok = jnp.logical_and(_expand_sel(sel, 0, nsel_keys) > 0.5, kpos_s <= qpos)
        o_s = _masked_attend(q, sel_scr[0:nsel_keys, 0:128], sel_scr[0:nsel_keys, 128:256], ok)
        kw = win_scr[:, 0:128].astype(BF16)
        vw = win_scr[:, 128:256].astype(BF16)
        o_w = _masked_attend(q, kw, vw, _window_ok(qpos, kpos_w))
        branches.append((o_c, o_s, o_w))
    for j, tile in enumerate(_gate_and_place(gt_ref[0], branches, nq)):
        o_ref[0, :, j * 128:(j + 1) * 128] = tile


def _nsa_sample(page_table, q8, gates8, kvn16, w, win, cmp_pages, sel_pages, tq):
    b, n_pages = page_table.shape
    page = cmp_pages.shape[1]
    past = n_pages * page
    wbuf = win.shape[1]
    lp = past + 512
    nq = q8.shape[1]

    def row(shape):
        return pl.BlockSpec((1,) + shape, lambda i, pt: (i, 0, 0))

    def page_spec(p):
        return pl.BlockSpec((1, page, 256), lambda i, pt, p=p: (pt[i, p], 0, 0))

    grid_spec = pltpu.PrefetchScalarGridSpec(
        num_scalar_prefetch=1,
        grid=(b,),
        in_specs=([row((nq, H_A * 128)), row((nq, 128)), row((kvn16.shape[1], 768)),
                   pl.BlockSpec((CMP_BLOCK, 256), lambda i, pt: (0, 0)), row((wbuf, 256))]
                  + [page_spec(p) for p in range(n_pages)] * 2),
        out_specs=[row((nq, H_A * DH_A)), row((wbuf, 256))],
        scratch_shapes=[pltpu.VMEM((lp, 256), F32), pltpu.VMEM((lp, 256), BF16),
                        pltpu.VMEM((wbuf + 128, 256), F32)],
    )
    return pl.pallas_call(
        functools.partial(_nsa_sample_body, n_pages=n_pages, page=page, past=past, tq=tq),
        grid_spec=grid_spec,
        out_shape=[jax.ShapeDtypeStruct((b, nq, H_A * DH_A), F32),
                   jax.ShapeDtypeStruct((b, wbuf, 256), F32)],
        compiler_params=_cparams("arbitrary"),
        name="nsa_sample",
    )(page_table, q8, gates8, kvn16, w, win, *([cmp_pages] * n_pages), *([sel_pages] * n_pages))


def _peer_body(x_ref, g_ref, gf_ref, wq_ref, keys_ref, u_ref, vt_ref, o_ref,
               hnt_scr, n1_scr, f1_scr, q2_scr, e2_scr, acc_scr, *, ec, final_norm):
    c = pl.program_id(1)
    tm = x_ref.shape[0]
    k = P_TOPK

    @pl.when(c == 0)
    def _select():
        hn = _rmsnorm(x_ref[...], g_ref[...])
        hnt_scr[...] = hn.T.astype(BF16)
        acc_scr[...] = jnp.zeros_like(acc_scr)

        def head(h, carry):
            r0 = pl.multiple_of(h * 2 * P_DHALF, 2 * P_DHALF)
            qt = _dot(wq_ref[pl.ds(r0, 2 * P_DHALF), :], hnt_scr[...]).astype(BF16)
            s1 = _dot(keys_ref[2 * h], qt[0:P_DHALF])
            s2 = _dot(keys_ref[2 * h + 1], qt[P_DHALF:2 * P_DHALF])
            pos1, v1 = _topk_pos(s1, 0, k)
            pos2, v2 = _topk_pos(s2, 0, k)
            comb = jnp.concatenate([v1[p:p + 1] + v2 for p in range(k)], axis=0)
            posc, vc = _topk_pos(comb, 0, k)
            z = jnp.sum(jnp.exp(vc - vc[0:1]), axis=0, keepdims=True)
            cnt = jnp.sum(jnp.where(posc < k, 1.0, 0.0).reshape(k, k, tm), axis=1)
            n1 = jnp.zeros((N_KEYS, tm), F32)
            for p in range(k):
                n1 = jnp.where(pos1 == float(p), cnt[p:p + 1], n1)
            n1_scr[h] = n1
            f1_scr[h] = jnp.where(pos1 < k, jnp.exp(s1 - v1[0:1]), 0.0) / z
            q2_scr[h] = pos2
            e2_scr[h] = jnp.exp(s2 - v2[0:1])
            return carry

        lax.fori_loop(0, P_HEADS, head, 0)

    act = _gelu(_dot(u_ref[...], hnt_scr[...]))
    hs = []
    for ai in range(ec // N_KEYS):
        a = c * (ec // N_KEYS) + ai
        w = jnp.zeros((N_KEYS, tm), F32)
        for h in range(P_HEADS):
            n1 = n1_scr[h, pl.ds(a, 1), :]
            f1 = f1_scr[h, pl.ds(a, 1), :]
            w = w + jnp.where(q2_scr[h] < n1, e2_scr[h] * f1, 0.0)
        hs.append((w * act[ai * N_KEYS:(ai + 1) * N_KEYS]).astype(BF16))
    acc_scr[...] += _dot(vt_ref[...], jnp.concatenate(hs, axis=0))

    @pl.when(c == pl.num_programs(1) - 1)
    def _finish():
        y = x_ref[...] + acc_scr[...].T
        if final_norm:
            y = _rmsnorm(y, gf_ref[...])
        o_ref[...] = y


def _peer(x, g, gf, wq_t, keys, u, vt, final_norm, tm=256, ec=1024):
    n, d = x.shape
    const2 = lambda i, c: (0, 0)
    return pl.pallas_call(
        functools.partial(_peer_body, ec=ec, final_norm=final_norm),
        grid=(n // tm, N_EXP // ec),
        in_specs=[pl.BlockSpec((tm, d), lambda i, c: (i, 0)),
                  pl.BlockSpec((1, d), const2), pl.BlockSpec((1, d), const2),
                  pl.BlockSpec(wq_t.shape, const2),
                  pl.BlockSpec(keys.shape, lambda i, c: (0, 0, 0)),
                  pl.BlockSpec((ec, d), lambda i, c: (c, 0)),
                  pl.BlockSpec((d, ec), lambda i, c: (0, c))],
        out_specs=pl.BlockSpec((tm, d), lambda i, c: (i, 0)),
        out_shape=jax.ShapeDtypeStruct((n, d), F32),
        scratch_shapes=[pltpu.VMEM((d, tm), BF16)]
        + [pltpu.VMEM((P_HEADS, N_KEYS, tm), F32)] * 4
        + [pltpu.VMEM((d, tm), F32)],
        compiler_params=_cparams("parallel", "arbitrary"),
        name="peer",
    )(x, g.reshape(1, d), gf.reshape(1, d), wq_t, keys, u, vt)


def _unit_lower_inverse(m):
    c = m.shape[0]
    p = -m
    inv = p
    span = 2
    while span < c:
        p = _dot_f32(p, p)
        inv = inv + p + _dot_f32(inv, p)
        span *= 2
    row = lax.broadcasted_iota(jnp.int32, m.shape, 0)
    col = lax.broadcasted_iota(jnp.int32, m.shape, 1)
    return inv + jnp.where(row == col, 1.0, 0.0)


def _gdn_body(qkv_ref, z_ref, ba_ref, cb_ref, s0_ref, wc_ref, prm_ref, gn_ref, o_ref, s_ref,
              xc_scr, s_scr, *, valid):
    t = pl.program_id(1)
    c = GDN_CHUNK
    r = qkv_ref.shape[1]

    @pl.when(t == 0)
    def _():
        xc_scr[0:8, :] = cb_ref[0]
        s_scr[...] = s0_ref[0]

    xc_scr[8:8 + r, :] = qkv_ref[0]
    if r < c:
        xc_scr[8 + r:8 + c, :] = jnp.zeros((c - r, QKV_C), F32)
    wc = wc_ref[...]
    conv = wc[0:1] * xc_scr[pl.ds(5, c), :]
    for j in range(1, CONV_W):
        conv = conv + wc[j:j + 1] * xc_scr[pl.ds(5 + j, c), :]
    xc_scr[0:8, :] = xc_scr[c:c + 8, :]
    act = _silu(conv)

    row = lax.broadcasted_iota(jnp.int32, (c, 1), 0)
    live = row < valid
    ba = ba_ref[0]
    if r < c:
        ba = jnp.concatenate([ba, jnp.zeros((c - r, 128), F32)], axis=0)
    prm = prm_ref[...]
    beta_all = jnp.where(live, jax.nn.sigmoid(ba), 0.0)
    g_all = jnp.where(live, prm[0:1] * _softplus(ba + prm[1:2]), 0.0)
    gc_all = g_all
    s = 1
    while s < c:
        gc_all = gc_all + _shift_rows(gc_all, s, 0.0)
        s *= 2
    gct = gc_all.T
    ri = lax.broadcasted_iota(jnp.int32, (c, c), 0)
    ci = lax.broadcasted_iota(jnp.int32, (c, c), 1)
    tri = ri >= ci
    strict = ri > ci
    z = z_ref[0]
    if r < c:
        z = jnp.concatenate([z, jnp.zeros((c - r, H_C * DV_C), F32)], axis=0)
    for h in range(H_C):
        q = act[:, h * DK_C:(h + 1) * DK_C]
        k = act[:, (H_C + h) * DK_C:(H_C + h + 1) * DK_C]
        v = act[:, (2 * H_C + h) * DK_C:(2 * H_C + h + 1) * DK_C]
        q = q * lax.rsqrt(jnp.sum(q * q, -1, keepdims=True) + 1e-6) * (DK_C ** -0.5)
        k = k * lax.rsqrt(jnp.sum(k * k, -1, keepdims=True) + 1e-6)
        beta = beta_all[:, h:h + 1]
        gcol = gc_all[:, H_C + h:H_C + h + 1]
        grow = gct[H_C + h:H_C + h + 1, :]
        glast = gc_all[c - 1:c, H_C + h:H_C + h + 1]
        lmat = jnp.where(tri, jnp.exp(jnp.where(tri, gcol - grow, 0.0)), 0.0)
        kb = k * beta
        kbf = k.astype(BF16)
        m = jnp.where(strict, _dot_nt(kb.astype(BF16), kbf) * lmat, 0.0)
        tinv = _unit_lower_inverse(m).astype(BF16)
        eg = jnp.exp(gcol)
        u = _dot(tinv, (v * beta).astype(BF16))
        w = _dot(tinv, (kb * eg).astype(BF16))
        aqk = _dot_nt(q.astype(BF16), kbf) * lmat
        qd = q * eg
        kd = k * jnp.exp(glast - gcol)
        st = s_scr[h]
        stb = st.astype(BF16)
        v_new = u - _dot(w.astype(BF16), stb)
        o = _dot(qd.astype(BF16), stb) + _dot(aqk.astype(BF16), v_new.astype(BF16))
        s_scr[h] = st * jnp.exp(glast) + _dot(kd.T.astype(BF16), v_new.astype(BF16))
        o = o * lax.rsqrt(jnp.mean(o * o, -1, keepdims=True) + EPS) * gn_ref[...]
        o = o * _silu(z[:, h * DV_C:(h + 1) * DV_C])
        o_ref[0, :, h * DV_C:(h + 1) * DV_C] = o[0:r].astype(o_ref.dtype)

    @pl.when(t == pl.num_programs(1) - 1)
    def _():
        s_ref[0] = s_scr[...]


def _gdn(qkv, z, ba, cbuf8, s0, wc, prm, gn, valid):
    b, t, _ = qkv.shape
    r = min(t, GDN_CHUNK)
    tile = lambda wd: pl.BlockSpec((1, r, wd), lambda i, j: (i, j, 0))
    return pl.pallas_call(
        functools.partial(_gdn_body, valid=valid),
        grid=(b, t // r),
        in_specs=[tile(QKV_C), tile(H_C * DV_C), tile(128),
                  pl.BlockSpec((1, 8, QKV_C), lambda i, j: (i, 0, 0)),
                  pl.BlockSpec((1, H_C, DK_C, DV_C), lambda i, j: (i, 0, 0, 0)),
                  pl.BlockSpec((CONV_W, QKV_C), lambda i, j: (0, 0)),
                  pl.BlockSpec((8, 128), lambda i, j: (0, 0)),
                  pl.BlockSpec((1, DV_C), lambda i, j: (0, 0))],
        out_specs=[tile(H_C * DV_C),
                   pl.BlockSpec((1, H_C, DK_C, DV_C), lambda i, j: (i, 0, 0, 0))],
        out_shape=[jax.ShapeDtypeStruct((b, t, H_C * DV_C), F32 if r < GDN_CHUNK else BF16),
                   jax.ShapeDtypeStruct((b, H_C, DK_C, DV_C), F32)],
        scratch_shapes=[pltpu.VMEM((GDN_CHUNK + 8, QKV_C), F32), pltpu.VMEM((H_C, DK_C, DV_C), F32)],
        compiler_params=_cparams("parallel", "arbitrary"),
        name="gdn",
    )(qkv, z, ba, cbuf8, s0, wc, prm, gn)


def _block_diag(w):
    n, d, e = w.shape
    eye = jnp.eye(n, dtype=w.dtype)
    return (w[:, :, None, :] * eye[:, None, :, None]).reshape(n * d, n * e)


def _pad_rows(x, rows, front=0):
    return jnp.pad(x, ((0, 0), (front, rows - x.shape[1] - front), (0, 0)))


def _layer_a_weights(w_in, w_pool):
    d = w_in.shape[0]
    q_a, kv_a = H_A * DH_A, 3 * 2 * G_A * DH_A
    wq = w_in[:, :q_a].reshape(d, G_A, R_A, DH_A)
    zeros = jnp.zeros_like(wq)
    wq = jnp.stack([jnp.concatenate([wq[:, 0], zeros[:, 0]], -1),
                    jnp.concatenate([zeros[:, 1], wq[:, 1]], -1)], axis=1).reshape(d, H_A * 128)
    i1 = q_a + kv_a
    i2 = i1 + 3 * H_A
    wg = jnp.pad(w_in[:, i1:i2], ((0, 0), (0, 128 - 3 * H_A)))
    w = jnp.concatenate([wq, w_in[:, q_a:i1], wg, w_in[:, i2:]], axis=1).astype(BF16)
    wp = jnp.concatenate([jnp.broadcast_to(w_pool[0][:, None], (CMP_BLOCK, 128)),
                          jnp.broadcast_to(w_pool[1][:, None], (CMP_BLOCK, 128))], axis=1)
    return w, wp


def kernel(x_prompt, x_sample, cache_cmp_kv, cache_sel_kv, cache_win_kv, state_rg_h, state_rg_conv, state_gdn_S, state_gdn_conv, page_table, norm_mix, norm_ffn, norm_final, w_in_a, w_cmp_pool, w_rg_conv, b_rg_conv, w_rg_a, b_rg_a, w_rg_x, b_rg_x, rg_lambda, w_out_a, w_in_c, w_c_conv, c_A_log, c_dt_bias, c_norm, w_out_c, peer_w_q, peer_keys, peer_u, peer_v):
    bp, tp, d = x_prompt.shape
    bs, ts, _ = x_sample.shape
    np_, ns = bp * tp, bs * ts
    xp = x_prompt.reshape(np_, d)
    xs = x_sample.reshape(ns, d)
    kv_w = 3 * 2 * G_A * DH_A

    def peer_layer(x, layer, final):
        wq_t = peer_w_q[layer].T.astype(BF16)
        keys = peer_keys[layer].reshape(2 * P_HEADS, N_KEYS, P_DHALF).astype(BF16)
        u = peer_u[layer].astype(BF16)
        vt = peer_v[layer].T.astype(BF16)
        return [_peer(xx, norm_ffn[layer], norm_final, wq_t, keys, u, vt, final) for xx in x]

    w0, wp = _layer_a_weights(w_in_a[0], w_cmp_pool[0])
    qo, ko, go, ro = 0, H_A * 128, H_A * 128 + kv_w, H_A * 128 + kv_w + 128
    outs = ((qo, H_A * 128, None), (ko, kv_w, None), (ko + 256, 512, None), (go, 128, "sigmoid"),
            (ro, D_RNN, None), (ro + D_RNN, D_RNN, None))
    dts = (BF16, F32, BF16, F32, F32, F32)
    qp, kvp, kvbp, gtp, xrp, ygp = _norm_proj(xp, norm_mix[0], w0, outs, dts, 512)
    qs, kvs, _, gts, xrs, ygs = _norm_proj(xs, norm_mix[0], w0, outs, dts, min(512, ns))

    kvp3 = kvp.reshape(bp, tp, kv_w)
    pe, po = _nsa_pool(kvp3, wp)
    att_p = _nsa_prompt(qp.reshape(bp, tp, H_A * 128), gtp.reshape(bp, tp, 128), pe, po,
                        kvbp.reshape(bp, tp, 512)).reshape(np_, H_A * DH_A)

    kvs3 = kvs.reshape(bs, ts, kv_w)
    n_pool, page = cache_cmp_kv.shape[1], cache_cmp_kv.shape[2]
    att_s8, win_new = _nsa_sample(
        page_table,
        _pad_rows(qs.astype(F32).reshape(bs, ts, H_A * 128), 8),
        _pad_rows(gts.reshape(bs, ts, 128), 8),
        _pad_rows(kvs3, 16), wp,
        cache_win_kv[0].reshape(bs, -1, 256),
        cache_cmp_kv[0].reshape(n_pool, page, 256),
        cache_sel_kv[0].reshape(n_pool, page, 256), ts)
    att_s = att_s8[:, :ts].reshape(ns, H_A * DH_A).astype(BF16)

    rg = (w_rg_conv[0], b_rg_conv[0].reshape(1, D_RNN), _block_diag(w_rg_a[0]).astype(BF16),
          b_rg_a[0].reshape(1, D_RNN), _block_diag(w_rg_x[0]).astype(BF16), b_rg_x[0].reshape(1, D_RNN),
          rg_lambda[0].reshape(1, D_RNN))
    xrp3 = xrp.reshape(bp, tp, D_RNN)
    rec_p, hl_p = _rglru_prompt(xrp3, ygp.reshape(bp, tp, D_RNN), *rg)
    xrs3 = xrs.reshape(bs, ts, D_RNN)
    rec_s, hl_s = _rglru_sample(xrs3.transpose(1, 0, 2), ygs.reshape(bs, ts, D_RNN).transpose(1, 0, 2),
                                state_rg_conv[0].transpose(1, 0, 2), state_rg_h[0], *rg)
    rec_s = rec_s.transpose(1, 0, 2).reshape(ns, D_RNN).astype(BF16)

    wo = w_out_a[0].astype(BF16)
    wo_att, wo_rec = wo[:H_A * DH_A], wo[H_A * DH_A:]
    xp = _out_proj([att_p, rec_p.reshape(np_, D_RNN)], [wo_att, wo_rec], xp, 512)
    xs = _out_proj([att_s, rec_s], [wo_att, wo_rec], xs, min(512, ns))
    xp, xs = peer_layer([xp, xs], 0, False)

    d_c = H_C * DV_C
    w1 = jnp.pad(w_in_c[0], ((0, 0), (0, 128 - 2 * H_C))).astype(BF16)
    outs1 = ((0, QKV_C, None), (QKV_C, d_c, None), (QKV_C + d_c, 128, None))
    dts1 = (F32, F32, F32)
    qkv_p, z_p, ba_p = _norm_proj(xp, norm_mix[1], w1, outs1, dts1, 512)
    qkv_s, z_s, ba_s = _norm_proj(xs, norm_mix[1], w1, outs1, dts1, min(512, ns))
    lane = jnp.arange(128)
    hsel = jnp.clip(lane - H_C, 0, H_C - 1)
    in_g = (lane >= H_C) & (lane < 2 * H_C)
    prm = jnp.zeros((8, 128), F32)
    prm = prm.at[0].set(jnp.where(in_g, -jnp.exp(c_A_log[0])[hsel], 0.0))
    prm = prm.at[1].set(jnp.where(in_g, c_dt_bias[0][hsel], 0.0))
    gn = c_norm[0].reshape(1, DV_C)
    qkv_p3 = qkv_p.reshape(bp, tp, QKV_C)
    o_p, gs_p = _gdn(qkv_p3, z_p.reshape(bp, tp, d_c), ba_p.reshape(bp, tp, 128),
                     jnp.zeros((bp, 8, QKV_C), F32), jnp.zeros((bp, H_C, DK_C, DV_C), F32),
                     w_c_conv[0], prm, gn, GDN_CHUNK)
    qkv_s3 = qkv_s.reshape(bs, ts, QKV_C)
    o_s, gs_s = _gdn(_pad_rows(qkv_s3, 8), _pad_rows(z_s.reshape(bs, ts, d_c), 8),
                     _pad_rows(ba_s.reshape(bs, ts, 128), 8),
                     _pad_rows(state_gdn_conv[0], 8, front=8 - (CONV_W - 1)), state_gdn_S[0],
                     w_c_conv[0], prm, gn, ts)
    wo_c = w_out_c[0].astype(BF16)
    xp = _out_proj([o_p.reshape(np_, d_c)], [wo_c], xp, 512)
    xs = _out_proj([o_s[:, :ts].reshape(ns, d_c).astype(BF16)], [wo_c], xs, min(512, ns))
    yp, ys = peer_layer([xp, xs], 1, True)

    kvp6 = kvp3.reshape(bp, tp, 3, 2, G_A, DH_A)
    kvs6 = kvs3.reshape(bs, ts, 3, 2, G_A, DH_A)
    wlen = min(WINDOW, tp)
    cw = CONV_W - 1
    return (yp.reshape(bp, tp, d), ys.reshape(bs, ts, d),
            kvp6[None, :, :, 0], kvs6[None, :, :, 0], kvp6[None, :, :, 1], kvs6[None, :, :, 1],
            kvp6[None, :, tp - wlen:, 2], win_new.reshape(bs, -1, 2, G_A, DH_A)[None],
            hl_p[None], hl_s[None], xrp3[None, :, tp - cw:], xrs3[None, :, ts - cw:],
            gs_p[None], gs_s[None], qkv_p3[None, :, tp - cw:], qkv_s3[None, :, ts - cw:])
```

```python
import functools
import math

import jax
import jax.numpy as jnp
from jax import lax
from jax.experimental import pallas as pl
from jax.experimental.pallas import tpu as pltpu

F32 = jnp.float32
BF16 = jnp.bfloat16

D_MODEL = 1024
H_A, G_A, R_A, DH_A = 8, 2, 4, 64
CMP_BLOCK, SEL_BLOCK, TOP_N, WINDOW, Q_BLOCK = 32, 64, 16, 512, 128
FORCE_BONUS = 1.0e3
NEG_INF = -1.0e30
N_SEL_LANES = 128
D_RNN, RG_C, CONV_W = 512, 8.0, 4
H_C, DK_C, DV_C = 8, 128, 128
QKV_C = 3 * H_C * DK_C
GDN_CHUNK = 128
P_HEADS, N_KEYS, P_DHALF, P_TOPK = 8, 128, 128, 16
N_EXP = N_KEYS * N_KEYS
EPS = 1e-6

VMEM_LIMIT = 56 * 1024 * 1024


def _cparams(*sem):
    return pltpu.CompilerParams(dimension_semantics=sem, vmem_limit_bytes=VMEM_LIMIT)


def _rmsnorm(x, g):
    return x * lax.rsqrt(jnp.mean(x * x, -1, keepdims=True) + EPS) * g


def _gelu(x):
    return 0.5 * x * (1.0 + jnp.tanh(math.sqrt(2.0 / math.pi) * (x + 0.044715 * (x * x * x))))


def _softplus(x):
    return jnp.maximum(x, 0.0) + jnp.log1p(jnp.exp(-jnp.abs(x)))


def _silu(x):
    return x * jax.nn.sigmoid(x)


def _neg_expm1(x):
    t = jnp.tanh(0.5 * x)
    return -2.0 * t / (1.0 - t)


def _dot(a, b):
    return jnp.dot(a, b, preferred_element_type=F32)


def _dot_nt(a, b):
    return lax.dot_general(a, b, (((1,), (1,)), ((), ())), preferred_element_type=F32)


def _dot_x3(a, b):
    a_hi = a.astype(BF16)
    b_hi = b.astype(BF16)
    a_lo = (a - a_hi.astype(F32)).astype(BF16)
    b_lo = (b - b_hi.astype(F32)).astype(BF16)
    return _dot(a_hi, b_hi) + (_dot(a_hi, b_lo) + _dot(a_lo, b_hi))


def _shift_rows(x, s, fill):
    rolled = pltpu.roll(x, s, 0)
    row = lax.broadcasted_iota(jnp.int32, x.shape, 0)
    return jnp.where(row >= s, rolled, fill)


def _topk_pos(s, axis, k, idx=None):
    if idx is None:
        idx = lax.broadcasted_iota(jnp.int32, s.shape, axis).astype(F32)
    vshape = tuple(k if d == axis else s.shape[d] for d in range(s.ndim))
    vidx = lax.broadcasted_iota(jnp.int32, vshape, axis)
    pos = jnp.full(s.shape, float(k), F32)
    vals = jnp.zeros(vshape, F32)
    for i in range(k):
        m = jnp.max(s, axis=axis, keepdims=True)
        j = jnp.min(jnp.where(s == m, idx, 1e9), axis=axis, keepdims=True)
        oh = idx == j
        pos = jnp.where(oh, float(i), pos)
        s = jnp.where(oh, -jnp.inf, s)
        vals = jnp.where(vidx == i, m, vals)
    return pos, vals


def _norm_proj_body(x_ref, g_ref, w_ref, *out_refs, outs):
    h = _rmsnorm(x_ref[...], g_ref[...]).astype(BF16)
    for o_ref, (off, width, act) in zip(out_refs, outs):
        z = _dot(h, w_ref[:, off:off + width])
        if act == "sigmoid":
            z = jax.nn.sigmoid(z)
        o_ref[...] = z.astype(o_ref.dtype)


def _norm_proj(x, g, w, outs, dtypes, tm):
    n, d = x.shape
    body = functools.partial(_norm_proj_body, outs=outs)
    return pl.pallas_call(
        body,
        grid=(n // tm,),
        in_specs=[pl.BlockSpec((tm, d), lambda i: (i, 0)),
                  pl.BlockSpec((1, d), lambda i: (0, 0)),
                  pl.BlockSpec(w.shape, lambda i: (0, 0))],
        out_specs=[pl.BlockSpec((tm, wd), lambda i: (i, 0)) for (_, wd, _) in outs],
        out_shape=[jax.ShapeDtypeStruct((n, wd), dt) for (_, wd, _), dt in zip(outs, dtypes)],
        compiler_params=_cparams("parallel"),
        name="norm_proj",
    )(x, g.reshape(1, d), w)


def _out_proj_body(*refs, n_in):
    a_refs, w_refs = refs[:n_in], refs[n_in:2 * n_in]
    x_ref, o_ref = refs[2 * n_in], refs[2 * n_in + 1]
    acc = x_ref[...]
    for a, w in zip(a_refs, w_refs):
        acc = acc + _dot(a[...], w[...])
    o_ref[...] = acc


def _out_proj(acts, ws, x, tm):
    n, d = x.shape
    n_in = len(acts)
    return pl.pallas_call(
        functools.partial(_out_proj_body, n_in=n_in),
        grid=(n // tm,),
        in_specs=([pl.BlockSpec((tm, a.shape[1]), lambda i: (i, 0)) for a in acts]
                  + [pl.BlockSpec(w.shape, lambda i: (0, 0)) for w in ws]
                  + [pl.BlockSpec((tm, d), lambda i: (i, 0))]),
        out_specs=pl.BlockSpec((tm, d), lambda i: (i, 0)),
        out_shape=jax.ShapeDtypeStruct((n, d), F32),
        compiler_params=_cparams("parallel"),
        name="out_proj",
    )(*acts, *ws, x)


def _rg_gates(xf, wa, ba, wx, bx, lam):
    xb = xf.astype(BF16)
    r = jax.nn.sigmoid(_dot(xb, wa) + ba)
    i = jax.nn.sigmoid(_dot(xb, wx) + bx)
    log_a = -RG_C * r * _softplus(-lam)
    a = jnp.exp(log_a)
    b = jnp.sqrt(_neg_expm1(2.0 * log_a)) * (i * xf)
    return a, b


def _rglru_prompt_body(xr_ref, yg_ref, wc_ref, bc_ref, wa_ref, ba_ref, wx_ref, bx_ref, lam_ref,
                       rec_ref, hl_ref, xc_scr, h_scr, *, rt):
    t = pl.program_id(1)

    @pl.when(t == 0)
    def _():
        xc_scr[0:8, :] = jnp.zeros((8, D_RNN), F32)
        h_scr[...] = jnp.zeros_like(h_scr)

    xc_scr[8:8 + rt, :] = xr_ref[0]
    wc = wc_ref[...]
    conv = wc[0:1] * xc_scr[pl.ds(5, rt), :]
    for j in range(1, CONV_W):
        conv = conv + wc[j:j + 1] * xc_scr[pl.ds(5 + j, rt), :]
    xc_scr[0:8, :] = xc_scr[rt:rt + 8, :]
    xf = conv + bc_ref[...]
    a, b = _rg_gates(xf, wa_ref[...], ba_ref[...], wx_ref[...], bx_ref[...], lam_ref[...])
    s = 1
    while s < rt:
        b = a * _shift_rows(b, s, 0.0) + b
        a = a * _shift_rows(a, s, 1.0)
        s *= 2
    h = a * h_scr[0:1, :] + b
    h_last = h[rt - 1:rt, :]
    h_scr[0:1, :] = h_last
    hl_ref[0] = h_last
    rec_ref[0] = (h * _gelu(yg_ref[0])).astype(rec_ref.dtype)


def _rglru_prompt(xr, yg, wc, bc, wa, ba, wx, bx, lam, rt=256):
    b, t, d = xr.shape
    vec = lambda: pl.BlockSpec((1, d), lambda i, j: (0, 0))
    mat = lambda: pl.BlockSpec((d, d), lambda i, j: (0, 0))
    rec, hl = pl.pallas_call(
        functools.partial(_rglru_prompt_body, rt=rt),
        grid=(b, t // rt),
        in_specs=[pl.BlockSpec((1, rt, d), lambda i, j: (i, j, 0)),
                  pl.BlockSpec((1, rt, d), lambda i, j: (i, j, 0)),
                  pl.BlockSpec((CONV_W, d), lambda i, j: (0, 0)),
                  vec(), mat(), vec(), mat(), vec(), vec()],
        out_specs=[pl.BlockSpec((1, rt, d), lambda i, j: (i, j, 0)),
                   pl.BlockSpec((1, 1, d), lambda i, j: (i, 0, 0))],
        out_shape=[jax.ShapeDtypeStruct((b, t, d), BF16),
                   jax.ShapeDtypeStruct((b, 1, d), F32)],
        scratch_shapes=[pltpu.VMEM((rt + 8, d), F32), pltpu.VMEM((8, d), F32)],
        compiler_params=_cparams("arbitrary", "arbitrary"),
        name="rglru_prompt",
    )(xr, yg, wc, bc, wa, ba, wx, bx, lam)
    return rec, hl[:, 0]


def _rglru_sample_body(xr_ref, yg_ref, buf_ref, h0_ref, wc_ref, bc_ref, wa_ref, ba_ref, wx_ref,
                       bx_ref, lam_ref, rec_ref, hl_ref, *, t_len):
    xs = [buf_ref[j] for j in range(CONV_W - 1)] + [xr_ref[j] for j in range(t_len)]
    wc = wc_ref[...]
    h = h0_ref[...]
    for t in range(t_len):
        conv = wc[0:1] * xs[t]
        for j in range(1, CONV_W):
            conv = conv + wc[j:j + 1] * xs[t + j]
        xf = conv + bc_ref[...]
        a, b = _rg_gates(xf, wa_ref[...], ba_ref[...], wx_ref[...], bx_ref[...], lam_ref[...])
        h = a * h + b
        rec_ref[t] = h * _gelu(yg_ref[t])
    hl_ref[...] = h


def _rglru_sample(xr, yg, buf, h0, wc, bc, wa, ba, wx, bx, lam):
    t_len, b, d = xr.shape
    return pl.pallas_call(
        functools.partial(_rglru_sample_body, t_len=t_len),
        out_shape=[jax.ShapeDtypeStruct((t_len, b, d), F32),
                   jax.ShapeDtypeStruct((b, d), F32)],
        compiler_params=pltpu.CompilerParams(vmem_limit_bytes=VMEM_LIMIT),
        name="rglru_sample",
    )(xr, yg, buf, h0, wc, bc, wa, ba, wx, bx, lam)


def _pool_blocks(x3, w):
    first = jnp.sum(x3[:, :CMP_BLOCK, :] * w[None], axis=1)
    second = jnp.sum(x3[:, CMP_BLOCK:, :] * w[None], axis=1)
    return first, second


def _nsa_pool_body(x_ref, w_ref, pe_ref, po_ref):
    x = x_ref[0]
    nb = x.shape[0] // SEL_BLOCK
    pe, po = _pool_blocks(x.reshape(nb, SEL_BLOCK, x.shape[1]), w_ref[...])
    pe_ref[0] = pe
    po_ref[0] = po


def _nsa_pool(kv, w, rows=1024):
    b, t, _ = kv.shape
    nb = rows // SEL_BLOCK
    return pl.pallas_call(
        _nsa_pool_body,
        grid=(b, t // rows),
        in_specs=[pl.BlockSpec((1, rows, 256), lambda i, j: (i, j, 0)),
                  pl.BlockSpec((CMP_BLOCK, 256), lambda i, j: (0, 0))],
        out_specs=[pl.BlockSpec((1, nb, 256), lambda i, j: (i, j, 0))] * 2,
        out_shape=[jax.ShapeDtypeStruct((b, t // SEL_BLOCK, 256), F32)] * 2,
        compiler_params=_cparams("parallel", "parallel"),
        name="nsa_pool",
    )(kv, w)


def _cmp_branch(q, qpos, pe, po, nq):
    jl = lax.broadcasted_iota(jnp.int32, (1, N_SEL_LANES), 1)
    vis_e = (SEL_BLOCK * jl + (CMP_BLOCK - 1)) <= qpos
    vis_o = (SEL_BLOCK * jl + (SEL_BLOCK - 1)) <= qpos
    s_e = jnp.where(vis_e, _dot_nt(q, pe[:, :128].astype(BF16)), NEG_INF)
    s_o = jnp.where(vis_o, _dot_nt(q, po[:, :128].astype(BF16)), NEG_INF)
    m = jnp.maximum(jnp.max(s_e, -1, keepdims=True), jnp.max(s_o, -1, keepdims=True))
    p_e = jnp.where(vis_e, jnp.exp(s_e - m), 0.0)
    p_o = jnp.where(vis_o, jnp.exp(s_o - m), 0.0)
    den = jnp.sum(p_e, -1, keepdims=True) + jnp.sum(p_o, -1, keepdims=True)
    inv = jnp.where(den > 0.0, 1.0 / jnp.maximum(den, 1e-30), 0.0)
    p_e = p_e * inv
    p_o = p_o * inv
    o = _dot(p_e.astype(BF16), pe[:, 128:].astype(BF16)) + _dot(p_o.astype(BF16), po[:, 128:].astype(BF16))
    p = p_e + p_o
    imp = p[0:nq]
    for r in range(1, R_A):
        imp = imp + p[r * nq:(r + 1) * nq]
    return o, imp


def _select_blocks(imp, qpos, transposed):
    blk = lax.broadcasted_iota(jnp.int32, (1, N_SEL_LANES), 1)
    valid = blk * SEL_BLOCK <= qpos
    forced = jnp.logical_or(blk == 0, blk == (qpos >> 6))
    score = jnp.where(valid, imp + jnp.where(forced, FORCE_BONUS, 0.0), -jnp.inf)
    if transposed:
        pos, _ = _topk_pos(score.T, 0, TOP_N)
        sel = jnp.where(pos < TOP_N, 1.0, 0.0).T
    else:
        pos, _ = _topk_pos(score, 1, TOP_N)
        sel = jnp.where(pos < TOP_N, 1.0, 0.0)
    return jnp.where(valid, sel, 0.0)


def _expand_sel(sel, first_blk, nkeys):
    bi = lax.broadcasted_iota(jnp.int32, (N_SEL_LANES, nkeys), 0)
    ki = lax.broadcasted_iota(jnp.int32, (N_SEL_LANES, nkeys), 1)
    e = jnp.where(bi - first_blk == (ki >> 6), 1.0, 0.0).astype(BF16)
    x = _dot(sel.astype(BF16), e)
    return jnp.concatenate([x] * R_A, axis=0)


def _masked_attend(q, k, v, ok):
    s = jnp.where(ok, _dot_nt(q, k), NEG_INF)
    m = jnp.max(s, -1, keepdims=True)
    p = jnp.where(ok, jnp.exp(s - m), 0.0)
    l = jnp.sum(p, -1, keepdims=True)
    return _dot(p.astype(BF16), v) / l


def _window_ok(qpos, kpos):
    d = qpos - kpos
    return jnp.logical_and(d >= 0, d <= WINDOW)


def _gate_and_place(gt, branches, nq):
    heads = []
    for g in range(G_A):
        for r in range(R_A):
            h = g * R_A + r
            o = None
            for br in range(3):
                term = gt[:, 3 * h + br:3 * h + br + 1] * branches[g][br][r * nq:(r + 1) * nq]
                o = term if o is None else o + term
            if (h % 2) != g:
                o = pltpu.roll(o, DH_A, 1)
            heads.append(o)
    lane = lax.broadcasted_iota(jnp.int32, (1, 128), 1)
    return [jnp.where(lane < DH_A, heads[2 * j], heads[2 * j + 1]) for j in range(H_A // 2)]


def _nsa_prompt_body(q_ref, gt_ref, pe_ref, po_ref, kv_ref, oh_ref, o_ref):
    i = pl.program_id(1)
    nq = Q_BLOCK
    base = i * nq
    pe, po = pe_ref[0], po_ref[0]
    rowi = lax.broadcasted_iota(jnp.int32, (R_A * nq, 1), 0)
    qpos = base + (rowi & (nq - 1))
    branches = []
    for g in range(G_A):
        q = jnp.concatenate([q_ref[0, :, (g * R_A + r) * 128:(g * R_A + r + 1) * 128] for r in range(R_A)], axis=0)
        q = q * jnp.asarray(DH_A ** -0.5, BF16)
        o_c, imp = _cmp_branch(q, qpos, pe, po, nq)
        sel = _select_blocks(imp, qpos[0:nq], transposed=True)
        selterm = jnp.where(sel > 0.5, 0.0, NEG_INF).astype(BF16)
        qa = jnp.concatenate([q, jnp.concatenate([selterm] * R_A, axis=0)], axis=1)

        def scores(c, qa=qa):
            k0 = pl.multiple_of(c * 256, 256)
            ka = jnp.concatenate([kv_ref[0, pl.ds(k0, 256), 0:128], oh_ref[pl.ds(k0, 256), :]], axis=1)
            return k0, _dot_nt(qa, ka)

        def update(carry, s, k0):
            m, l, acc = carry
            m_new = jnp.maximum(m, jnp.max(s, -1, keepdims=True))
            alpha = jnp.exp(m - m_new)
            p = jnp.exp(s - m_new)
            l = alpha * l + jnp.sum(p, -1, keepdims=True)
            acc = alpha * acc + _dot(p.astype(BF16), kv_ref[0, pl.ds(k0, 256), 128:256])
            return m_new, l, acc

        def full_chunk(c, carry):
            k0, s = scores(c)
            return update(carry, s, k0)

        init = (jnp.full((R_A * nq, 1), NEG_INF, F32), jnp.zeros((R_A * nq, 1), F32),
                jnp.zeros((R_A * nq, 128), F32))
        carry = lax.fori_loop(0, i // 2, full_chunk, init)
        k0, s = scores(i // 2)
        kpos = k0 + lax.broadcasted_iota(jnp.int32, (1, 256), 1)
        _, l, acc = update(carry, jnp.where(kpos <= qpos, s, NEG_INF), k0)
        o_s = acc / l

        wlen = WINDOW + nq
        ws = pl.multiple_of(jnp.maximum(i - WINDOW // nq, 0) * nq, nq)
        kw = kv_ref[0, pl.ds(ws, wlen), 256:384]
        vw = kv_ref[0, pl.ds(ws, wlen), 384:512]
        kpos = ws + lax.broadcasted_iota(jnp.int32, (1, wlen), 1)
        o_w = _masked_attend(q, kw, vw, _window_ok(qpos, kpos))
        branches.append((o_c, o_s, o_w))
    for j, tile in enumerate(_gate_and_place(gt_ref[0], branches, nq)):
        o_ref[0, :, j * 128:(j + 1) * 128] = tile.astype(o_ref.dtype)


def _nsa_prompt(qpad, gates, pe, po, kvb):
    b, t, _ = qpad.shape
    nq = Q_BLOCK
    blk_of_key = jnp.arange(t, dtype=jnp.int32)[:, None] // SEL_BLOCK
    onehot = (blk_of_key == jnp.arange(N_SEL_LANES, dtype=jnp.int32)[None, :]).astype(BF16)
    return pl.pallas_call(
        _nsa_prompt_body,
        grid=(b, t // nq),
        in_specs=[pl.BlockSpec((1, nq, H_A * 128), lambda i, j: (i, j, 0)),
                  pl.BlockSpec((1, nq, 128), lambda i, j: (i, j, 0)),
                  pl.BlockSpec((1, N_SEL_LANES, 256), lambda i, j: (i, 0, 0)),
                  pl.BlockSpec((1, N_SEL_LANES, 256), lambda i, j: (i, 0, 0)),
                  pl.BlockSpec((1, t, 512), lambda i, j: (i, 0, 0)),
                  pl.BlockSpec((t, N_SEL_LANES), lambda i, j: (0, 0))],
        out_specs=pl.BlockSpec((1, nq, H_A * DH_A), lambda i, j: (i, j, 0)),
        out_shape=jax.ShapeDtypeStruct((b, t, H_A * DH_A), BF16),
        compiler_params=_cparams("parallel", "arbitrary"),
        name="nsa_prompt",
    )(qpad, gates, pe, po, kvb, onehot)


def _nsa_sample_body(pt_ref, q_ref, gt_ref, kvn_ref, w_ref, win_ref, *rest, n_pages, page, past, tq):
    del pt_ref
    cmp_pages, sel_pages = rest[:n_pages], rest[n_pages:2 * n_pages]
    o_ref, wout_ref, cmp_scr, sel_scr, win_scr = rest[2 * n_pages:]
    nq = q_ref.shape[1]
    npad = kvn_ref.shape[1]
    lp = cmp_scr.shape[0]
    wbuf = win_ref.shape[1]
    kvn = kvn_ref[0]
    for p in range(n_pages):
        cmp_scr[p * page:(p + 1) * page, :] = cmp_pages[p][0]
        sel_scr[p * page:(p + 1) * page, :] = sel_pages[p][0].astype(BF16)
    cmp_scr[past:past + npad, :] = kvn[:, 0:256]
    sel_scr[past:past + npad, :] = kvn[:, 256:512].astype(BF16)
    cmp_scr[past + npad:lp, :] = jnp.zeros((lp - past - npad, 256), F32)
    sel_scr[past + npad:lp, :] = jnp.zeros((lp - past - npad, 256), BF16)
    win_scr[0:wbuf, :] = win_ref[0]
    win_scr[wbuf:wbuf + npad, :] = kvn[:, 512:768]
    wl = win_scr.shape[0]
    win_scr[wbuf + npad:wl, :] = jnp.zeros((wl - wbuf - npad, 256), F32)
    wout_ref[0] = win_scr[pl.ds(tq, wbuf), :]

    nb = lp // SEL_BLOCK
    pe, po = _pool_blocks(cmp_scr[...].reshape(nb, SEL_BLOCK, 256), w_ref[...])
    zpad = jnp.zeros((N_SEL_LANES - nb, 256), F32)
    pe = jnp.concatenate([pe, zpad], axis=0)
    po = jnp.concatenate([po, zpad], axis=0)

    rowi = lax.broadcasted_iota(jnp.int32, (R_A * nq, 1), 0)
    qpos = past + (rowi & (nq - 1))
    nsel_keys = past + 256
    kpos_s = lax.broadcasted_iota(jnp.int32, (1, nsel_keys), 1)
    kpos_w = (past - wbuf) + lax.broadcasted_iota(jnp.int32, (1, wl), 1)
    branches = []
    qs, ocs, imps = [], [], []
    for g in range(G_A):
        q = jnp.concatenate([q_ref[0, :, (g * R_A + r) * 128:(g * R_A + r + 1) * 128] for r in range(R_A)], axis=0)
        q = (q * DH_A ** -0.5).astype(BF16)
        o_c, imp = _cmp_branch(q, qpos, pe, po, nq)
        qs.append(q)
        ocs.append(o_c)
        imps.append(imp)
    sel_all = _select_blocks(jnp.concatenate(imps, axis=0), qpos[0:G_A * nq], transposed=False)
    for g in range(G_A):
        q, o_c, sel = qs[g], ocs[g], sel_all[g * nq:(g + 1) * nq]
        ok = jnp.logical_and(_expand_sel(sel, 0, nsel_keys) > 0.5, kpos_s <= qpos)
        o_s = _masked_attend(q, sel_scr[0:nsel_keys, 0:128], sel_scr[0:nsel_keys, 128:256], ok)
        kw = win_scr[:, 0:128].astype(BF16)
        vw = win_scr[:, 128:256].astype(BF16)
        o_w = _masked_attend(q, kw, vw, _window_ok(qpos, kpos_w))
        branches.append((o_c, o_s, o_w))
    for j, tile in enumerate(_gate_and_place(gt_ref[0], branches, nq)):
        o_ref[0, :, j * 128:(j + 1) * 128] = tile


def _nsa_sample(page_table, q8, gates8, kvn16, w, win, cmp_pages, sel_pages, tq):
    b, n_pages = page_table.shape
    page = cmp_pages.shape[1]
    past = n_pages * page
    wbuf = win.shape[1]
    lp = past + 512
    nq = q8.shape[1]

    def row(shape):
        return pl.BlockSpec((1,) + shape, lambda i, pt: (i, 0, 0))

    def page_spec(p):
        return pl.BlockSpec((1, page, 256), lambda i, pt, p=p: (pt[i, p], 0, 0))

    grid_spec = pltpu.PrefetchScalarGridSpec(
        num_scalar_prefetch=1,
        grid=(b,),
        in_specs=([row((nq, H_A * 128)), row((nq, 128)), row((kvn16.shape[1], 768)),
                   pl.BlockSpec((CMP_BLOCK, 256), lambda i, pt: (0, 0)), row((wbuf, 256))]
                  + [page_spec(p) for p in range(n_pages)] * 2),
        out_specs=[row((nq, H_A * DH_A)), row((wbuf, 256))],
        scratch_shapes=[pltpu.VMEM((lp, 256), F32), pltpu.VMEM((lp, 256), BF16),
                        pltpu.VMEM((wbuf + 128, 256), F32)],
    )
    return pl.pallas_call(
        functools.partial(_nsa_sample_body, n_pages=n_pages, page=page, past=past, tq=tq),
        grid_spec=grid_spec,
        out_shape=[jax.ShapeDtypeStruct((b, nq, H_A * DH_A), F32),
                   jax.ShapeDtypeStruct((b, wbuf, 256), F32)],
        compiler_params=_cparams("arbitrary"),
        name="nsa_sample",
    )(page_table, q8, gates8, kvn16, w, win, *([cmp_pages] * n_pages), *([sel_pages] * n_pages))


def _comb_candidates(v1, v2):
    k, half, lanes = P_TOPK, P_TOPK // 2, v1.shape[1]
    assert k == 16
    qi = lax.broadcasted_iota(jnp.int32, (half, lanes), 0).astype(F32)
    sums = [v1[0:1] + v2]
    idx = [lax.broadcasted_iota(jnp.int32, (k, lanes), 0).astype(F32)]
    for p in range(1, half):
        sums.append(v1[p:p + 1] + v2[0:half])
        idx.append(qi + float(p * k))
    sums.append(v1[half:k] + v2[0:1])
    idx.append((qi + float(half)) * float(k))
    return jnp.concatenate(sums, axis=0), jnp.concatenate(idx, axis=0)


def _peer_select(s1, s2):
    k, half = P_TOPK, P_TOPK // 2
    pos1, v1 = _topk_pos(s1, 0, k)
    pos2, v2 = _topk_pos(s2, 0, k)
    comb, cidx = _comb_candidates(v1, v2)
    posc, vc = _topk_pos(comb, 0, k, cidx)
    z = jnp.sum(jnp.exp(vc - vc[0:1]), axis=0, keepdims=True)
    sel = jnp.where(posc < k, 1.0, 0.0)
    n1 = jnp.where(pos1 == 0.0, jnp.sum(sel[0:k], axis=0, keepdims=True), 0.0)
    for p in range(1, half):
        r0 = k + half * (p - 1)
        n1 = jnp.where(pos1 == float(p), jnp.sum(sel[r0:r0 + half], axis=0, keepdims=True), n1)
    for p in range(half, k):
        r0 = k + half * (half - 1) + (p - half)
        n1 = jnp.where(pos1 == float(p), sel[r0:r0 + 1], n1)
    f1 = jnp.where(pos1 < k, jnp.exp(s1 - v1[0:1]), 0.0) / z
    return n1, f1, pos2, jnp.exp(s2 - v2[0:1])


def _peer_body(x_ref, g_ref, gf_ref, wq_ref, keys_ref, u_ref, vt_ref, o_ref,
               hnt_scr, n1_scr, f1_scr, q2_scr, e2_scr, acc_scr, h_scr, *, ec, final_norm):
    c = pl.program_id(1)
    tm = x_ref.shape[0]

    @pl.when(c == 0)
    def _select():
        hn = _rmsnorm(x_ref[...], g_ref[...])
        hnt_scr[...] = hn.T.astype(BF16)
        acc_scr[...] = jnp.zeros_like(acc_scr)

        def head(h, carry):
            r0 = pl.multiple_of(h * 2 * P_DHALF, 2 * P_DHALF)
            qt = _dot(wq_ref[pl.ds(r0, 2 * P_DHALF), :], hnt_scr[...]).astype(BF16)
            for lt in range(tm // 128):
                sl = slice(lt * 128, (lt + 1) * 128)
                s1 = _dot(keys_ref[2 * h], qt[0:P_DHALF, sl])
                s2 = _dot(keys_ref[2 * h + 1], qt[P_DHALF:2 * P_DHALF, sl])
                n1, f1, q2, e2 = _peer_select(s1, s2)
                n1_scr[h, :, sl] = n1
                f1_scr[h, :, sl] = f1
                q2_scr[h, :, sl] = q2.astype(BF16)
                e2_scr[h, :, sl] = e2.astype(BF16)
            return carry

        lax.fori_loop(0, P_HEADS, head, 0)

    nc = pl.num_programs(1) - 1

    def weighted_acts():
        act = _gelu(_dot(u_ref[...], hnt_scr[...])).astype(BF16)
        hs = []
        for ai in range(ec // N_KEYS):
            a = c * (ec // N_KEYS) + ai
            w = None
            for h in range(P_HEADS):
                n1 = n1_scr[h, pl.ds(a, 1), :].astype(BF16)
                f1 = f1_scr[h, pl.ds(a, 1), :].astype(BF16)
                term = jnp.where(q2_scr[h] < n1, e2_scr[h] * f1, jnp.zeros((), BF16))
                w = term if w is None else w + term
            hs.append(w * act[ai * N_KEYS:(ai + 1) * N_KEYS])
        return jnp.concatenate(hs, axis=0)

    @pl.when(c == 0)
    def _first():
        h_scr[...] = weighted_acts()

    @pl.when(jnp.logical_and(c > 0, c < nc))
    def _middle():
        prev = h_scr[...]
        h_scr[...] = weighted_acts()
        acc_scr[...] += _dot(vt_ref[...], prev)

    @pl.when(c == nc)
    def _finish():
        y = x_ref[...] + (acc_scr[...] + _dot(vt_ref[...], h_scr[...])).T
        if final_norm:
            y = _rmsnorm(y, gf_ref[...])
        o_ref[...] = y


def _peer(x, g, gf, wq_t, keys, u, vt, final_norm, tm=256, ec=1024):
    n, d = x.shape
    nc = N_EXP // ec
    const2 = lambda i, c: (0, 0)
    return pl.pallas_call(
        functools.partial(_peer_body, ec=ec, final_norm=final_norm),
        grid=(n // tm, nc + 1),
        in_specs=[pl.BlockSpec((tm, d), lambda i, c: (i, 0)),
                  pl.BlockSpec((1, d), const2), pl.BlockSpec((1, d), const2),
                  pl.BlockSpec(wq_t.shape, const2),
                  pl.BlockSpec(keys.shape, lambda i, c: (0, 0, 0)),
                  pl.BlockSpec((ec, d), lambda i, c: (jnp.minimum(c, nc - 1), 0)),
                  pl.BlockSpec((d, ec), lambda i, c: (0, jnp.maximum(c - 1, 0)))],
        out_specs=pl.BlockSpec((tm, d), lambda i, c: (i, 0)),
        out_shape=jax.ShapeDtypeStruct((n, d), F32),
        scratch_shapes=[pltpu.VMEM((d, tm), BF16)]
        + [pltpu.VMEM((P_HEADS, N_KEYS, tm), F32)] * 2
        + [pltpu.VMEM((P_HEADS, N_KEYS, tm), BF16)] * 2
        + [pltpu.VMEM((d, tm), F32), pltpu.VMEM((ec, tm), BF16)],
        compiler_params=_cparams("parallel", "arbitrary"),
        name="peer",
    )(x, g.reshape(1, d), gf.reshape(1, d), wq_t, keys, u, vt)


def _unit_lower_inverses(ms, order):
    ps = [-m for m in ms]
    invs = list(ps)
    span = 2
    while span < order:
        ps = [_dot_x3(p, p) for p in ps]
        invs = [inv + p + _dot_x3(inv, p) for inv, p in zip(invs, ps)]
        span *= 2
    row = lax.broadcasted_iota(jnp.int32, ms[0].shape, 0)
    col = lax.broadcasted_iota(jnp.int32, ms[0].shape, 1)
    eye = jnp.where(row == col, 1.0, 0.0)
    return [inv + eye for inv in invs]


def _gdn_body(qkv_ref, z_ref, ba_ref, cb_ref, s0_ref, wc_ref, prm_ref, gn_ref, o_ref, s_ref,
              xc_scr, s_scr, *, valid):
    t = pl.program_id(1)
    c = GDN_CHUNK
    r = qkv_ref.shape[1]

    @pl.when(t == 0)
    def _():
        xc_scr[0:8, :] = cb_ref[0]
        s_scr[...] = s0_ref[0]

    xc_scr[8:8 + r, :] = qkv_ref[0]
    if r < c:
        xc_scr[8 + r:8 + c, :] = jnp.zeros((c - r, QKV_C), F32)
    wc = wc_ref[...]
    conv = wc[0:1] * xc_scr[pl.ds(5, c), :]
    for j in range(1, CONV_W):
        conv = conv + wc[j:j + 1] * xc_scr[pl.ds(5 + j, c), :]
    xc_scr[0:8, :] = xc_scr[c:c + 8, :]
    act = _silu(conv)

    row = lax.broadcasted_iota(jnp.int32, (c, 1), 0)
    live = row < valid
    ba = ba_ref[0]
    if r < c:
        ba = jnp.concatenate([ba, jnp.zeros((c - r, 128), F32)], axis=0)
    prm = prm_ref[...]
    beta_all = jnp.where(live, jax.nn.sigmoid(ba), 0.0)
    g_all = jnp.where(live, prm[0:1] * _softplus(ba + prm[1:2]), 0.0)
    gc_all = g_all
    s = 1
    while s < c:
        gc_all = gc_all + _shift_rows(gc_all, s, 0.0)
        s *= 2
    gct = gc_all.T
    ri = lax.broadcasted_iota(jnp.int32, (c, c), 0)
    ci = lax.broadcasted_iota(jnp.int32, (c, c), 1)
    tri = ri >= ci
    strict = ri > ci
    z = z_ref[0]
    if r < c:
        z = jnp.concatenate([z, jnp.zeros((c - r, H_C * DV_C), F32)], axis=0)
    per_head = []
    for h in range(H_C):
        q = act[:, h * DK_C:(h + 1) * DK_C]
        k = act[:, (H_C + h) * DK_C:(H_C + h + 1) * DK_C]
        v = act[:, (2 * H_C + h) * DK_C:(2 * H_C + h + 1) * DK_C]
        q = q * lax.rsqrt(jnp.sum(q * q, -1, keepdims=True) + 1e-6) * (DK_C ** -0.5)
        k = k * lax.rsqrt(jnp.sum(k * k, -1, keepdims=True) + 1e-6)
        beta = beta_all[:, h:h + 1]
        gcol = gc_all[:, H_C + h:H_C + h + 1]
        grow = gct[H_C + h:H_C + h + 1, :]
        glast = gc_all[c - 1:c, H_C + h:H_C + h + 1]
        lmat = jnp.where(tri, jnp.exp(jnp.where(tri, gcol - grow, 0.0)), 0.0)
        kb = k * beta
        kbf = k.astype(BF16)
        m = jnp.where(strict, _dot_nt(kb.astype(BF16), kbf) * lmat, 0.0)
        eg = jnp.exp(gcol)
        aqk = (_dot_nt(q.astype(BF16), kbf) * lmat).astype(BF16)
        per_head.append(dict(m=m, vb=(v * beta).astype(BF16), kbe=(kb * eg).astype(BF16), aqk=aqk,
                             qd=(q * eg).astype(BF16), kdt=(k * jnp.exp(glast - gcol)).T.astype(BF16),
                             decay=jnp.exp(glast)))
    tinvs = _unit_lower_inverses([ph["m"] for ph in per_head], min(valid, c))
    for h, (ph, tinv) in enumerate(zip(per_head, tinvs)):
        tinv = tinv.astype(BF16)
        u = _dot(tinv, ph["vb"])
        w = _dot(tinv, ph["kbe"])
        st = s_scr[h]
        stb = st.astype(BF16)
        v_new = u - _dot(w.astype(BF16), stb)
        o = _dot(ph["qd"], stb) + _dot(ph["aqk"], v_new.astype(BF16))
        s_scr[h] = st * ph["decay"] + _dot(ph["kdt"], v_new.astype(BF16))
        o = o * lax.rsqrt(jnp.mean(o * o, -1, keepdims=True) + EPS) * gn_ref[...]
        o = o * _silu(z[:, h * DV_C:(h + 1) * DV_C])
        o_ref[0, :, h * DV_C:(h + 1) * DV_C] = o[0:r].astype(o_ref.dtype)

    @pl.when(t == pl.num_programs(1) - 1)
    def _():
        s_ref[0] = s_scr[...]


def _gdn(qkv, z, ba, cbuf8, s0, wc, prm, gn, valid):
    b, t, _ = qkv.shape
    r = min(t, GDN_CHUNK)
    tile = lambda wd: pl.BlockSpec((1, r, wd), lambda i, j: (i, j, 0))
    return pl.pallas_call(
        functools.partial(_gdn_body, valid=valid),
        grid=(b, t // r),
        in_specs=[tile(QKV_C), tile(H_C * DV_C), tile(128),
                  pl.BlockSpec((1, 8, QKV_C), lambda i, j: (i, 0, 0)),
                  pl.BlockSpec((1, H_C, DK_C, DV_C), lambda i, j: (i, 0, 0, 0)),
                  pl.BlockSpec((CONV_W, QKV_C), lambda i, j: (0, 0)),
                  pl.BlockSpec((8, 128), lambda i, j: (0, 0)),
                  pl.BlockSpec((1, DV_C), lambda i, j: (0, 0))],
        out_specs=[tile(H_C * DV_C),
                   pl.BlockSpec((1, H_C, DK_C, DV_C), lambda i, j: (i, 0, 0, 0))],
        out_shape=[jax.ShapeDtypeStruct((b, t, H_C * DV_C), F32 if r < GDN_CHUNK else BF16),
                   jax.ShapeDtypeStruct((b, H_C, DK_C, DV_C), F32)],
        scratch_shapes=[pltpu.VMEM((GDN_CHUNK + 8, QKV_C), F32), pltpu.VMEM((H_C, DK_C, DV_C), F32)],
        compiler_params=_cparams("parallel", "arbitrary"),
        name="gdn",
    )(qkv, z, ba, cbuf8, s0, wc, prm, gn)


def _block_diag(w):
    n, d, e = w.shape
    eye = jnp.eye(n, dtype=w.dtype)
    return (w[:, :, None, :] * eye[:, None, :, None]).reshape(n * d, n * e)


def _pad_rows(x, rows, front=0):
    return jnp.pad(x, ((0, 0), (front, rows - x.shape[1] - front), (0, 0)))


def _layer_a_weights(w_in, w_pool):
    d = w_in.shape[0]
    q_a, kv_a = H_A * DH_A, 3 * 2 * G_A * DH_A
    wq = w_in[:, :q_a].reshape(d, G_A, R_A, DH_A)
    zeros = jnp.zeros_like(wq)
    wq = jnp.stack([jnp.concatenate([wq[:, 0], zeros[:, 0]], -1),
                    jnp.concatenate([zeros[:, 1], wq[:, 1]], -1)], axis=1).reshape(d, H_A * 128)
    i1 = q_a + kv_a
    i2 = i1 + 3 * H_A
    wg = jnp.pad(w_in[:, i1:i2], ((0, 0), (0, 128 - 3 * H_A)))
    w = jnp.concatenate([wq, w_in[:, q_a:i1], wg, w_in[:, i2:]], axis=1).astype(BF16)
    wp = jnp.concatenate([jnp.broadcast_to(w_pool[0][:, None], (CMP_BLOCK, 128)),
                          jnp.broadcast_to(w_pool[1][:, None], (CMP_BLOCK, 128))], axis=1)
    return w, wp


def kernel(x_prompt, x_sample, cache_cmp_kv, cache_sel_kv, cache_win_kv, state_rg_h, state_rg_conv, state_gdn_S, state_gdn_conv, page_table, norm_mix, norm_ffn, norm_final, w_in_a, w_cmp_pool, w_rg_conv, b_rg_conv, w_rg_a, b_rg_a, w_rg_x, b_rg_x, rg_lambda, w_out_a, w_in_c, w_c_conv, c_A_log, c_dt_bias, c_norm, w_out_c, peer_w_q, peer_keys, peer_u, peer_v):
    bp, tp, d = x_prompt.shape
    bs, ts, _ = x_sample.shape
    np_, ns = bp * tp, bs * ts
    xp = x_prompt.reshape(np_, d)
    xs = x_sample.reshape(ns, d)
    kv_w = 3 * 2 * G_A * DH_A

    def peer_layer(x, layer, final):
        wq_t = peer_w_q[layer].T.astype(BF16)
        keys = peer_keys[layer].reshape(2 * P_HEADS, N_KEYS, P_DHALF).astype(BF16)
        u = peer_u[layer].astype(BF16)
        vt = peer_v[layer].T.astype(BF16)
        return [_peer(xx, norm_ffn[layer], norm_final, wq_t, keys, u, vt, final) for xx in x]

    w0, wp = _layer_a_weights(w_in_a[0], w_cmp_pool[0])
    qo, ko, go, ro = 0, H_A * 128, H_A * 128 + kv_w, H_A * 128 + kv_w + 128
    outs = ((qo, H_A * 128, None), (ko, kv_w, None), (ko + 256, 512, None), (go, 128, "sigmoid"),
            (ro, D_RNN, None), (ro + D_RNN, D_RNN, None))
    dts = (BF16, F32, BF16, F32, F32, F32)
    qp, kvp, kvbp, gtp, xrp, ygp = _norm_proj(xp, norm_mix[0], w0, outs, dts, 512)
    qs, kvs, _, gts, xrs, ygs = _norm_proj(xs, norm_mix[0], w0, outs, dts, min(512, ns))

    kvp3 = kvp.reshape(bp, tp, kv_w)
    pe, po = _nsa_pool(kvp3, wp)
    att_p = _nsa_prompt(qp.reshape(bp, tp, H_A * 128), gtp.reshape(bp, tp, 128), pe, po,
                        kvbp.reshape(bp, tp, 512)).reshape(np_, H_A * DH_A)

    kvs3 = kvs.reshape(bs, ts, kv_w)
    n_pool, page = cache_cmp_kv.shape[1], cache_cmp_kv.shape[2]
    att_s8, win_new = _nsa_sample(
        page_table,
        _pad_rows(qs.astype(F32).reshape(bs, ts, H_A * 128), 8),
        _pad_rows(gts.reshape(bs, ts, 128), 8),
        _pad_rows(kvs3, 16), wp,
        cache_win_kv[0].reshape(bs, -1, 256),
        cache_cmp_kv[0].reshape(n_pool, page, 256),
        cache_sel_kv[0].reshape(n_pool, page, 256), ts)
    att_s = att_s8[:, :ts].reshape(ns, H_A * DH_A).astype(BF16)

    rg = (w_rg_conv[0], b_rg_conv[0].reshape(1, D_RNN), _block_diag(w_rg_a[0]).astype(BF16),
          b_rg_a[0].reshape(1, D_RNN), _block_diag(w_rg_x[0]).astype(BF16), b_rg_x[0].reshape(1, D_RNN),
          rg_lambda[0].reshape(1, D_RNN))
    xrp3 = xrp.reshape(bp, tp, D_RNN)
    rec_p, hl_p = _rglru_prompt(xrp3, ygp.reshape(bp, tp, D_RNN), *rg)
    xrs3 = xrs.reshape(bs, ts, D_RNN)
    rec_s, hl_s = _rglru_sample(xrs3.transpose(1, 0, 2), ygs.reshape(bs, ts, D_RNN).transpose(1, 0, 2),
                                state_rg_conv[0].transpose(1, 0, 2), state_rg_h[0], *rg)
    rec_s = rec_s.transpose(1, 0, 2).reshape(ns, D_RNN).astype(BF16)

    wo = w_out_a[0].astype(BF16)
    wo_att, wo_rec = wo[:H_A * DH_A], wo[H_A * DH_A:]
    xp = _out_proj([att_p, rec_p.reshape(np_, D_RNN)], [wo_att, wo_rec], xp, 512)
    xs = _out_proj([att_s, rec_s], [wo_att, wo_rec], xs, min(512, ns))
    xp, xs = peer_layer([xp, xs], 0, False)

    d_c = H_C * DV_C
    w1 = jnp.pad(w_in_c[0], ((0, 0), (0, 128 - 2 * H_C))).astype(BF16)
    outs1 = ((0, QKV_C, None), (QKV_C, d_c, None), (QKV_C + d_c, 128, None))
    dts1 = (F32, F32, F32)
    qkv_p, z_p, ba_p = _norm_proj(xp, norm_mix[1], w1, outs1, dts1, 512)
    qkv_s, z_s, ba_s = _norm_proj(xs, norm_mix[1], w1, outs1, dts1, min(512, ns))
    lane = jnp.arange(128)
    hsel = jnp.clip(lane - H_C, 0, H_C - 1)
    in_g = (lane >= H_C) & (lane < 2 * H_C)
    prm = jnp.zeros((8, 128), F32)
    prm = prm.at[0].set(jnp.where(in_g, -jnp.exp(c_A_log[0])[hsel], 0.0))
    prm = prm.at[1].set(jnp.where(in_g, c_dt_bias[0][hsel], 0.0))
    gn = c_norm[0].reshape(1, DV_C)
    qkv_p3 = qkv_p.reshape(bp, tp, QKV_C)
    o_p, gs_p = _gdn(qkv_p3, z_p.reshape(bp, tp, d_c), ba_p.reshape(bp, tp, 128),
                     jnp.zeros((bp, 8, QKV_C), F32), jnp.zeros((bp, H_C, DK_C, DV_C), F32),
                     w_c_conv[0], prm, gn, GDN_CHUNK)
    qkv_s3 = qkv_s.reshape(bs, ts, QKV_C)
    o_s, gs_s = _gdn(_pad_rows(qkv_s3, 8), _pad_rows(z_s.reshape(bs, ts, d_c), 8),
                     _pad_rows(ba_s.reshape(bs, ts, 128), 8),
                     _pad_rows(state_gdn_conv[0], 8, front=8 - (CONV_W - 1)), state_gdn_S[0],
                     w_c_conv[0], prm, gn, ts)
    wo_c = w_out_c[0].astype(BF16)
    xp = _out_proj([o_p.reshape(np_, d_c)], [wo_c], xp, 512)
    xs = _out_proj([o_s[:, :ts].reshape(ns, d_c).astype(BF16)], [wo_c], xs, min(512, ns))
    yp, ys = peer_layer([xp, xs], 1, True)

    kvp6 = kvp3.reshape(bp, tp, 3, 2, G_A, DH_A)
    kvs6 = kvs3.reshape(bs, ts, 3, 2, G_A, DH_A)
    wlen = min(WINDOW, tp)
    cw = CONV_W - 1
    return (yp.reshape(bp, tp, d), ys.reshape(bs, ts, d),
            kvp6[None, :, :, 0], kvs6[None, :, :, 0], kvp6[None, :, :, 1], kvs6[None, :, :, 1],
            kvp6[None, :, tp - wlen:, 2], win_new.reshape(bs, -1, 2, G_A, DH_A)[None],
            hl_p[None], hl_s[None], xrp3[None, :, tp - cw:], xrs3[None, :, ts - cw:],
            gs_p[None], gs_s[None], qkv_p3[None, :, tp - cw:], qkv_s3[None, :, ts - cw:])
```

```python
import functools
import math

import jax
import jax.numpy as jnp
from jax import lax
from jax.experimental import pallas as pl
from jax.experimental.pallas import tpu as pltpu

F32 = jnp.float32
BF16 = jnp.bfloat16

D_MODEL = 1024
H_A, G_A, R_A, DH_A = 8, 2, 4, 64
CMP_BLOCK, SEL_BLOCK, TOP_N, WINDOW, Q_BLOCK = 32, 64, 16, 512, 128
FORCE_BONUS = 1.0e3
NEG_INF = -1.0e30
N_SEL_LANES = 128
D_RNN, RG_C, CONV_W = 512, 8.0, 4
H_C, DK_C, DV_C = 8, 128, 128
QKV_C = 3 * H_C * DK_C
GDN_CHUNK = 128
P_HEADS, N_KEYS, P_DHALF, P_TOPK = 8, 128, 128, 16
N_EXP = N_KEYS * N_KEYS
EPS = 1e-6

VMEM_LIMIT = 56 * 1024 * 1024


def _cparams(*sem):
    return pltpu.CompilerParams(dimension_semantics=sem, vmem_limit_bytes=VMEM_LIMIT)


def _rmsnorm(x, g):
    return x * lax.rsqrt(jnp.mean(x * x, -1, keepdims=True) + EPS) * g


def _gelu(x):
    return 0.5 * x * (1.0 + jnp.tanh(math.sqrt(2.0 / math.pi) * (x + 0.044715 * (x * x * x))))


def _softplus(x):
    return jnp.maximum(x, 0.0) + jnp.log1p(jnp.exp(-jnp.abs(x)))


def _silu(x):
    return x * jax.nn.sigmoid(x)


def _neg_expm1(x):
    t = jnp.tanh(0.5 * x)
    return -2.0 * t / (1.0 - t)


def _dot(a, b):
    return jnp.dot(a, b, preferred_element_type=F32)


def _dot_nt(a, b):
    return lax.dot_general(a, b, (((1,), (1,)), ((), ())), preferred_element_type=F32)


def _dot_x3(a, b):
    a_hi = a.astype(BF16)
    b_hi = b.astype(BF16)
    a_lo = (a - a_hi.astype(F32)).astype(BF16)
    b_lo = (b - b_hi.astype(F32)).astype(BF16)
    return _dot(a_hi, b_hi) + (_dot(a_hi, b_lo) + _dot(a_lo, b_hi))


def _shift_rows(x, s, fill):
    rolled = pltpu.roll(x, s, 0)
    row = lax.broadcasted_iota(jnp.int32, x.shape, 0)
    return jnp.where(row >= s, rolled, fill)


def _topk_pos(s, axis, k, idx=None):
    if idx is None:
        idx = lax.broadcasted_iota(jnp.int32, s.shape, axis).astype(F32)
    vshape = tuple(k if d == axis else s.shape[d] for d in range(s.ndim))
    vidx = lax.broadcasted_iota(jnp.int32, vshape, axis)
    pos = jnp.full(s.shape, float(k), F32)
    vals = jnp.zeros(vshape, F32)
    for i in range(k):
        m = jnp.max(s, axis=axis, keepdims=True)
        j = jnp.min(jnp.where(s == m, idx, 1e9), axis=axis, keepdims=True)
        oh = idx == j
        pos = jnp.where(oh, float(i), pos)
        s = jnp.where(oh, -jnp.inf, s)
        vals = jnp.where(vidx == i, m, vals)
    return pos, vals


def _norm_proj_body(x_ref, g_ref, w_ref, *out_refs, outs):
    h = _rmsnorm(x_ref[...], g_ref[...]).astype(BF16)
    for o_ref, (off, width, act) in zip(out_refs, outs):
        z = _dot(h, w_ref[:, off:off + width])
        if act == "sigmoid":
            z = jax.nn.sigmoid(z)
        o_ref[...] = z.astype(o_ref.dtype)


def _norm_proj(x, g, w, outs, dtypes, tm):
    n, d = x.shape
    body = functools.partial(_norm_proj_body, outs=outs)
    return pl.pallas_call(
        body,
        grid=(n // tm,),
        in_specs=[pl.BlockSpec((tm, d), lambda i: (i, 0)),
                  pl.BlockSpec((1, d), lambda i: (0, 0)),
                  pl.BlockSpec(w.shape, lambda i: (0, 0))],
        out_specs=[pl.BlockSpec((tm, wd), lambda i: (i, 0)) for (_, wd, _) in outs],
        out_shape=[jax.ShapeDtypeStruct((n, wd), dt) for (_, wd, _), dt in zip(outs, dtypes)],
        compiler_params=_cparams("parallel"),
        name="norm_proj",
    )(x, g.reshape(1, d), w)


def _out_proj_body(*refs, n_in):
    a_refs, w_refs = refs[:n_in], refs[n_in:2 * n_in]
    x_ref, o_ref = refs[2 * n_in], refs[2 * n_in + 1]
    acc = x_ref[...]
    for a, w in zip(a_refs, w_refs):
        acc = acc + _dot(a[...], w[...])
    o_ref[...] = acc


def _out_proj(acts, ws, x, tm):
    n, d = x.shape
    n_in = len(acts)
    return pl.pallas_call(
        functools.partial(_out_proj_body, n_in=n_in),
        grid=(n // tm,),
        in_specs=([pl.BlockSpec((tm, a.shape[1]), lambda i: (i, 0)) for a in acts]
                  + [pl.BlockSpec(w.shape, lambda i: (0, 0)) for w in ws]
                  + [pl.BlockSpec((tm, d), lambda i: (i, 0))]),
        out_specs=pl.BlockSpec((tm, d), lambda i: (i, 0)),
        out_shape=jax.ShapeDtypeStruct((n, d), F32),
        compiler_params=_cparams("parallel"),
        name="out_proj",
    )(*acts, *ws, x)


def _rg_gates(xf, wa, ba, wx, bx, lam):
    xb = xf.astype(BF16)
    r = jax.nn.sigmoid(_dot(xb, wa) + ba)
    i = jax.nn.sigmoid(_dot(xb, wx) + bx)
    log_a = -RG_C * r * _softplus(-lam)
    a = jnp.exp(log_a)
    b = jnp.sqrt(_neg_expm1(2.0 * log_a)) * (i * xf)
    return a, b


def _rglru_prompt_body(xr_ref, yg_ref, wc_ref, bc_ref, wa_ref, ba_ref, wx_ref, bx_ref, lam_ref,
                       rec_ref, hl_ref, xc_scr, h_scr, *, rt):
    t = pl.program_id(1)

    @pl.when(t == 0)
    def _():
        xc_scr[0:8, :] = jnp.zeros((8, D_RNN), F32)
        h_scr[...] = jnp.zeros_like(h_scr)

    xc_scr[8:8 + rt, :] = xr_ref[0]
    wc = wc_ref[...]
    conv = wc[0:1] * xc_scr[pl.ds(5, rt), :]
    for j in range(1, CONV_W):
        conv = conv + wc[j:j + 1] * xc_scr[pl.ds(5 + j, rt), :]
    xc_scr[0:8, :] = xc_scr[rt:rt + 8, :]
    xf = conv + bc_ref[...]
    a, b = _rg_gates(xf, wa_ref[...], ba_ref[...], wx_ref[...], bx_ref[...], lam_ref[...])
    s = 1
    while s < rt:
        b = a * _shift_rows(b, s, 0.0) + b
        a = a * _shift_rows(a, s, 1.0)
        s *= 2
    h = a * h_scr[0:1, :] + b
    h_last = h[rt - 1:rt, :]
    h_scr[0:1, :] = h_last
    hl_ref[0] = h_last
    rec_ref[0] = (h * _gelu(yg_ref[0])).astype(rec_ref.dtype)


def _rglru_prompt(xr, yg, wc, bc, wa, ba, wx, bx, lam, rt=256):
    b, t, d = xr.shape
    vec = lambda: pl.BlockSpec((1, d), lambda i, j: (0, 0))
    mat = lambda: pl.BlockSpec((d, d), lambda i, j: (0, 0))
    rec, hl = pl.pallas_call(
        functools.partial(_rglru_prompt_body, rt=rt),
        grid=(b, t // rt),
        in_specs=[pl.BlockSpec((1, rt, d), lambda i, j: (i, j, 0)),
                  pl.BlockSpec((1, rt, d), lambda i, j: (i, j, 0)),
                  pl.BlockSpec((CONV_W, d), lambda i, j: (0, 0)),
                  vec(), mat(), vec(), mat(), vec(), vec()],
        out_specs=[pl.BlockSpec((1, rt, d), lambda i, j: (i, j, 0)),
                   pl.BlockSpec((1, 1, d), lambda i, j: (i, 0, 0))],
        out_shape=[jax.ShapeDtypeStruct((b, t, d), BF16),
                   jax.ShapeDtypeStruct((b, 1, d), F32)],
        scratch_shapes=[pltpu.VMEM((rt + 8, d), F32), pltpu.VMEM((8, d), F32)],
        compiler_params=_cparams("arbitrary", "arbitrary"),
        name="rglru_prompt",
    )(xr, yg, wc, bc, wa, ba, wx, bx, lam)
    return rec, hl[:, 0]


def _rglru_sample_body(xr_ref, yg_ref, buf_ref, h0_ref, wc_ref, bc_ref, wa_ref, ba_ref, wx_ref,
                       bx_ref, lam_ref, rec_ref, hl_ref, *, t_len):
    xs = [buf_ref[j] for j in range(CONV_W - 1)] + [xr_ref[j] for j in range(t_len)]
    wc = wc_ref[...]
    h = h0_ref[...]
    for t in range(t_len):
        conv = wc[0:1] * xs[t]
        for j in range(1, CONV_W):
            conv = conv + wc[j:j + 1] * xs[t + j]
        xf = conv + bc_ref[...]
        a, b = _rg_gates(xf, wa_ref[...], ba_ref[...], wx_ref[...], bx_ref[...], lam_ref[...])
        h = a * h + b
        rec_ref[t] = h * _gelu(yg_ref[t])
    hl_ref[...] = h


def _rglru_sample(xr, yg, buf, h0, wc, bc, wa, ba, wx, bx, lam):
    t_len, b, d = xr.shape
    return pl.pallas_call(
        functools.partial(_rglru_sample_body, t_len=t_len),
        out_shape=[jax.ShapeDtypeStruct((t_len, b, d), F32),
                   jax.ShapeDtypeStruct((b, d), F32)],
        compiler_params=pltpu.CompilerParams(vmem_limit_bytes=VMEM_LIMIT),
        name="rglru_sample",
    )(xr, yg, buf, h0, wc, bc, wa, ba, wx, bx, lam)


def _pool_blocks(x3, w):
    first = jnp.sum(x3[:, :CMP_BLOCK, :] * w[None], axis=1)
    second = jnp.sum(x3[:, CMP_BLOCK:, :] * w[None], axis=1)
    return first, second


def _nsa_pool_body(x_ref, w_ref, pe_ref, po_ref):
    x = x_ref[0]
    nb = x.shape[0] // SEL_BLOCK
    pe, po = _pool_blocks(x.reshape(nb, SEL_BLOCK, x.shape[1]), w_ref[...])
    pe_ref[0] = pe
    po_ref[0] = po


def _nsa_pool(kv, w, rows=1024):
    b, t, _ = kv.shape
    nb = rows // SEL_BLOCK
    return pl.pallas_call(
        _nsa_pool_body,
        grid=(b, t // rows),
        in_specs=[pl.BlockSpec((1, rows, 256), lambda i, j: (i, j, 0)),
                  pl.BlockSpec((CMP_BLOCK, 256), lambda i, j: (0, 0))],
        out_specs=[pl.BlockSpec((1, nb, 256), lambda i, j: (i, j, 0))] * 2,
        out_shape=[jax.ShapeDtypeStruct((b, t // SEL_BLOCK, 256), F32)] * 2,
        compiler_params=_cparams("parallel", "parallel"),
        name="nsa_pool",
    )(kv, w)


def _cmp_branch(q, qpos, pe, po, nq):
    jl = lax.broadcasted_iota(jnp.int32, (1, N_SEL_LANES), 1)
    vis_e = (SEL_BLOCK * jl + (CMP_BLOCK - 1)) <= qpos
    vis_o = (SEL_BLOCK * jl + (SEL_BLOCK - 1)) <= qpos
    s_e = jnp.where(vis_e, _dot_nt(q, pe[:, :128].astype(BF16)), NEG_INF)
    s_o = jnp.where(vis_o, _dot_nt(q, po[:, :128].astype(BF16)), NEG_INF)
    m = jnp.maximum(jnp.max(s_e, -1, keepdims=True), jnp.max(s_o, -1, keepdims=True))
    p_e = jnp.where(vis_e, jnp.exp(s_e - m), 0.0)
    p_o = jnp.where(vis_o, jnp.exp(s_o - m), 0.0)
    den = jnp.sum(p_e, -1, keepdims=True) + jnp.sum(p_o, -1, keepdims=True)
    inv = jnp.where(den > 0.0, 1.0 / jnp.maximum(den, 1e-30), 0.0)
    p_e = p_e * inv
    p_o = p_o * inv
    o = _dot(p_e.astype(BF16), pe[:, 128:].astype(BF16)) + _dot(p_o.astype(BF16), po[:, 128:].astype(BF16))
    p = p_e + p_o
    imp = p[0:nq]
    for r in range(1, R_A):
        imp = imp + p[r * nq:(r + 1) * nq]
    return o, imp


def _select_blocks(imp, qpos, blk_axis):
    shape = (N_SEL_LANES, 1) if blk_axis == 0 else (1, N_SEL_LANES)
    blk = lax.broadcasted_iota(jnp.int32, shape, blk_axis)
    valid = blk * SEL_BLOCK <= qpos
    forced = jnp.logical_or(blk == 0, blk == (qpos >> 6))
    score = jnp.where(valid, imp + jnp.where(forced, FORCE_BONUS, 0.0), -jnp.inf)
    pos, _ = _topk_pos(score, blk_axis, TOP_N)
    return jnp.where(valid, jnp.where(pos < TOP_N, 1.0, 0.0), 0.0)


def _expand_sel(sel, first_blk, nkeys):
    bi = lax.broadcasted_iota(jnp.int32, (N_SEL_LANES, nkeys), 0)
    ki = lax.broadcasted_iota(jnp.int32, (N_SEL_LANES, nkeys), 1)
    e = jnp.where(bi - first_blk == (ki >> 6), 1.0, 0.0).astype(BF16)
    x = _dot(sel.astype(BF16), e)
    return jnp.concatenate([x] * R_A, axis=0)


def _masked_attend(q, k, v, ok):
    s = jnp.where(ok, _dot_nt(q, k), NEG_INF)
    m = jnp.max(s, -1, keepdims=True)
    p = jnp.where(ok, jnp.exp(s - m), 0.0)
    l = jnp.sum(p, -1, keepdims=True)
    return _dot(p.astype(BF16), v) / l


def _window_ok(qpos, kpos):
    d = qpos - kpos
    return jnp.logical_and(d >= 0, d <= WINDOW)


def _gate_and_place(gt, branches):
    heads = []
    for h in range(H_A):
        o = None
        for br in range(3):
            term = gt[:, 3 * h + br:3 * h + br + 1] * branches[br][h]
            o = term if o is None else o + term
        if (h % 2) != h // R_A:
            o = pltpu.roll(o, DH_A, 1)
        heads.append(o)
    lane = lax.broadcasted_iota(jnp.int32, (1, 128), 1)
    return [jnp.where(lane < DH_A, heads[2 * j], heads[2 * j + 1]) for j in range(H_A // 2)]


def _split_heads(o, nq):
    return [o[r * nq:(r + 1) * nq] for r in range(R_A)]


SEL_CHUNK = 512


def _nsa_prompt_body(q_ref, gt_ref, pe_ref, po_ref, kv_ref, oh_ref, vt_ref, o_ref, acc_scr):
    i = pl.program_id(1)
    nq = Q_BLOCK
    base = i * nq
    pe, po = pe_ref[0], po_ref[0]
    rowi = lax.broadcasted_iota(jnp.int32, (R_A * nq, 1), 0)
    qpos = base + (rowi & (nq - 1))
    qpos_row = base + lax.broadcasted_iota(jnp.int32, (1, nq), 1)
    heads_c, heads_w, qats = [], [], []
    for g in range(G_A):
        q = jnp.concatenate([q_ref[0, :, (g * R_A + r) * 128:(g * R_A + r + 1) * 128] for r in range(R_A)], axis=0)
        q = q * jnp.asarray(DH_A ** -0.5, BF16)
        o_c, imp = _cmp_branch(q, qpos, pe, po, nq)
        heads_c += _split_heads(o_c, nq)
        sel_t = _select_blocks(imp.T, qpos_row, 0)
        selterm_t = jnp.where(sel_t > 0.5, 0.0, NEG_INF).astype(BF16)
        qh_t = [jnp.concatenate([qh.astype(F32).T.astype(BF16), selterm_t], axis=0) for qh in _split_heads(q, nq)]
        qats += [jnp.concatenate(qh_t[2 * j:2 * j + 2], axis=1) for j in range(R_A // 2)]

        wlen = WINDOW + nq
        ws = pl.multiple_of(jnp.maximum(i - WINDOW // nq, 0) * nq, nq)
        kw = kv_ref[0, pl.ds(ws, wlen), 256:384]
        vw = kv_ref[0, pl.ds(ws, wlen), 384:512]
        kpos = ws + lax.broadcasted_iota(jnp.int32, (1, wlen), 1)
        heads_w += _split_heads(_masked_attend(q, kw, vw, _window_ok(qpos, kpos)), nq)

    acc_scr[...] = jnp.zeros_like(acc_scr)

    def step(k0, ms, ls, diagonal):
        ka = jnp.concatenate([kv_ref[0, pl.ds(k0, SEL_CHUNK), 0:128], oh_ref[pl.ds(k0, SEL_CHUNK), :]], axis=1)
        vt = vt_ref[0, :, pl.ds(k0, SEL_CHUNK)]
        if diagonal:
            qpos_pair = base + (lax.broadcasted_iota(jnp.int32, (1, 2 * nq), 1) & (nq - 1))
            visible = k0 + lax.broadcasted_iota(jnp.int32, (SEL_CHUNK, 1), 0) <= qpos_pair
        ss = [_dot(ka, qat) for qat in qats]
        if diagonal:
            ss = [jnp.where(visible, s, NEG_INF) for s in ss]
        ms_new = [jnp.maximum(m, jnp.max(s, axis=0, keepdims=True)) for m, s in zip(ms, ss)]
        ps = [jnp.exp(s - m) for s, m in zip(ss, ms_new)]
        pvs = [_dot(vt, p.astype(BF16)) for p in ps]
        ls_new = []
        for j in range(n_pairs):
            alpha = jnp.exp(ms[j] - ms_new[j])
            ls_new.append(alpha * ls[j] + jnp.sum(ps[j], axis=0, keepdims=True))
            acc_scr[j] = alpha * acc_scr[j] + pvs[j]
        return tuple(ms_new), tuple(ls_new)

    def past_chunk(c, carry):
        return step(pl.multiple_of(c * SEL_CHUNK, SEL_CHUNK), carry[0], carry[1], False)

    n_pairs = H_A // 2
    init = (tuple(jnp.full((1, 2 * nq), NEG_INF, F32) for _ in range(n_pairs)),
            tuple(jnp.zeros((1, 2 * nq), F32) for _ in range(n_pairs)))
    n_past = (base + nq - 1) // SEL_CHUNK
    ms, ls = lax.fori_loop(0, n_past, past_chunk, init)
    _, ls = step(pl.multiple_of(n_past * SEL_CHUNK, SEL_CHUNK), ms, ls, True)
    heads_s = []
    for j in range(n_pairs):
        o_pair = acc_scr[j] / ls[j]
        heads_s += [o_pair[:, 0:nq].T, o_pair[:, nq:2 * nq].T]
    for j, tile in enumerate(_gate_and_place(gt_ref[0], (heads_c, heads_s, heads_w))):
        o_ref[0, :, j * 128:(j + 1) * 128] = tile.astype(o_ref.dtype)


def _nsa_prompt(qpad, gates, pe, po, kvb):
    b, t, _ = qpad.shape
    nq = Q_BLOCK
    blk_of_key = jnp.arange(t, dtype=jnp.int32)[:, None] // SEL_BLOCK
    onehot = (blk_of_key == jnp.arange(N_SEL_LANES, dtype=jnp.int32)[None, :]).astype(BF16)
    v_sel_t = kvb[:, :, 128:256].transpose(0, 2, 1)
    return pl.pallas_call(
        _nsa_prompt_body,
        grid=(b, t // nq),
        in_specs=[pl.BlockSpec((1, nq, H_A * 128), lambda i, j: (i, j, 0)),
                  pl.BlockSpec((1, nq, 128), lambda i, j: (i, j, 0)),
                  pl.BlockSpec((1, N_SEL_LANES, 256), lambda i, j: (i, 0, 0)),
                  pl.BlockSpec((1, N_SEL_LANES, 256), lambda i, j: (i, 0, 0)),
                  pl.BlockSpec((1, t, 512), lambda i, j: (i, 0, 0)),
                  pl.BlockSpec((t, N_SEL_LANES), lambda i, j: (0, 0)),
                  pl.BlockSpec((1, 128, t), lambda i, j: (i, 0, 0))],
        out_specs=pl.BlockSpec((1, nq, H_A * DH_A), lambda i, j: (i, j, 0)),
        out_shape=jax.ShapeDtypeStruct((b, t, H_A * DH_A), BF16),
        scratch_shapes=[pltpu.VMEM((H_A // 2, 128, 2 * nq), F32)],
        compiler_params=_cparams("parallel", "arbitrary"),
        name="nsa_prompt",
    )(qpad, gates, pe, po, kvb, onehot, v_sel_t)


def _nsa_sample_body(pt_ref, q_ref, gt_ref, kvn_ref, w_ref, win_ref, *rest, n_pages, page, past, tq):
    del pt_ref
    cmp_pages, sel_pages = rest[:n_pages], rest[n_pages:2 * n_pages]
    o_ref, wout_ref, cmp_scr, sel_scr, win_scr = rest[2 * n_pages:]
    nq = q_ref.shape[1]
    npad = kvn_ref.shape[1]
    lp = cmp_scr.shape[0]
    wbuf = win_ref.shape[1]
    kvn = kvn_ref[0]
    for p in range(n_pages):
        cmp_scr[p * page:(p + 1) * page, :] = cmp_pages[p][0]
        sel_scr[p * page:(p + 1) * page, :] = sel_pages[p][0].astype(BF16)
    cmp_scr[past:past + npad, :] = kvn[:, 0:256]
    sel_scr[past:past + npad, :] = kvn[:, 256:512].astype(BF16)
    cmp_scr[past + npad:lp, :] = jnp.zeros((lp - past - npad, 256), F32)
    sel_scr[past + npad:lp, :] = jnp.zeros((lp - past - npad, 256), BF16)
    win_scr[0:wbuf, :] = win_ref[0]
    win_scr[wbuf:wbuf + npad, :] = kvn[:, 512:768]
    wl = win_scr.shape[0]
    win_scr[wbuf + npad:wl, :] = jnp.zeros((wl - wbuf - npad, 256), F32)
    wout_ref[0] = win_scr[pl.ds(tq, wbuf), :]

    nb = lp // SEL_BLOCK
    pe, po = _pool_blocks(cmp_scr[...].reshape(nb, SEL_BLOCK, 256), w_ref[...])
    zpad = jnp.zeros((N_SEL_LANES - nb, 256), F32)
    pe = jnp.concatenate([pe, zpad], axis=0)
    po = jnp.concatenate([po, zpad], axis=0)

    rowi = lax.broadcasted_iota(jnp.int32, (R_A * nq, 1), 0)
    qpos = past + (rowi & (nq - 1))
    nsel_keys = past + 256
    kpos_s = lax.broadcasted_iota(jnp.int32, (1, nsel_keys), 1)
    kpos_w = (past - wbuf) + lax.broadcasted_iota(jnp.int32, (1, wl), 1)
    heads_c, heads_s, heads_w = [], [], []
    qs, imps = [], []
    for g in range(G_A):
        q = jnp.concatenate([q_ref[0, :, (g * R_A + r) * 128:(g * R_A + r + 1) * 128] for r in range(R_A)], axis=0)
        q = (q * DH_A ** -0.5).astype(BF16)
        o_c, imp = _cmp_branch(q, qpos, pe, po, nq)
        heads_c += _split_heads(o_c, nq)
        qs.append(q)
        imps.append(imp)
    sel_all = _select_blocks(jnp.concatenate(imps, axis=0), qpos[0:G_A * nq], 1)
    for g in range(G_A):
        q, sel = qs[g], sel_all[g * nq:(g + 1) * nq]
        ok = jnp.logical_and(_expand_sel(sel, 0, nsel_keys) > 0.5, kpos_s <= qpos)
        o_s = _masked_attend(q, sel_scr[0:nsel_keys, 0:128], sel_scr[0:nsel_keys, 128:256], ok)
        kw = win_scr[:, 0:128].astype(BF16)
        vw = win_scr[:, 128:256].astype(BF16)
        o_w = _masked_attend(q, kw, vw, _window_ok(qpos, kpos_w))
        heads_s += _split_heads(o_s, nq)
        heads_w += _split_heads(o_w, nq)
    for j, tile in enumerate(_gate_and_place(gt_ref[0], (heads_c, heads_s, heads_w))):
        o_ref[0, :, j * 128:(j + 1) * 128] = tile


def _nsa_sample(page_table, q8, gates8, kvn16, w, win, cmp_pages, sel_pages, tq):
    b, n_pages = page_table.shape
    page = cmp_pages.shape[1]
    past = n_pages * page
    wbuf = win.shape[1]
    lp = past + 512
    nq = q8.shape[1]

    def row(shape):
        return pl.BlockSpec((1,) + shape, lambda i, pt: (i, 0, 0))

    def page_spec(p):
        return pl.BlockSpec((1, page, 256), lambda i, pt, p=p: (pt[i, p], 0, 0))

    grid_spec = pltpu.PrefetchScalarGridSpec(
        num_scalar_prefetch=1,
        grid=(b,),
        in_specs=([row((nq, H_A * 128)), row((nq, 128)), row((kvn16.shape[1], 768)),
                   pl.BlockSpec((CMP_BLOCK, 256), lambda i, pt: (0, 0)), row((wbuf, 256))]
                  + [page_spec(p) for p in range(n_pages)] * 2),
        out_specs=[row((nq, H_A * DH_A)), row((wbuf, 256))],
        scratch_shapes=[pltpu.VMEM((lp, 256), F32), pltpu.VMEM((lp, 256), BF16),
                        pltpu.VMEM((wbuf + 128, 256), F32)],
    )
    return pl.pallas_call(
        functools.partial(_nsa_sample_body, n_pages=n_pages, page=page, past=past, tq=tq),
        grid_spec=grid_spec,
        out_shape=[jax.ShapeDtypeStruct((b, nq, H_A * DH_A), F32),
                   jax.ShapeDtypeStruct((b, wbuf, 256), F32)],
        compiler_params=_cparams("arbitrary"),
        name="nsa_sample",
    )(page_table, q8, gates8, kvn16, w, win, *([cmp_pages] * n_pages), *([sel_pages] * n_pages))


def _comb_candidates(v1, v2):
    k, half, lanes = P_TOPK, P_TOPK // 2, v1.shape[1]
    assert k == 16
    qi = lax.broadcasted_iota(jnp.int32, (half, lanes), 0).astype(F32)
    sums = [v1[0:1] + v2]
    idx = [lax.broadcasted_iota(jnp.int32, (k, lanes), 0).astype(F32)]
    for p in range(1, half):
        sums.append(v1[p:p + 1] + v2[0:half])
        idx.append(qi + float(p * k))
    sums.append(v1[half:k] + v2[0:1])
    idx.append((qi + float(half)) * float(k))
    return jnp.concatenate(sums, axis=0), jnp.concatenate(idx, axis=0)


def _peer_select(s1, s2):
    k, half = P_TOPK, P_TOPK // 2
    pos1, v1 = _topk_pos(s1, 0, k)
    pos2, v2 = _topk_pos(s2, 0, k)
    comb, cidx = _comb_candidates(v1, v2)
    posc, vc = _topk_pos(comb, 0, k, cidx)
    z = jnp.sum(jnp.exp(vc - vc[0:1]), axis=0, keepdims=True)
    sel = jnp.where(posc < k, 1.0, 0.0)
    n1 = jnp.where(pos1 == 0.0, jnp.sum(sel[0:k], axis=0, keepdims=True), 0.0)
    for p in range(1, half):
        r0 = k + half * (p - 1)
        n1 = jnp.where(pos1 == float(p), jnp.sum(sel[r0:r0 + half], axis=0, keepdims=True), n1)
    for p in range(half, k):
        r0 = k + half * (half - 1) + (p - half)
        n1 = jnp.where(pos1 == float(p), sel[r0:r0 + 1], n1)
    f1 = jnp.where(pos1 < k, jnp.exp(s1 - v1[0:1]), 0.0) / z
    return n1, f1, pos2, jnp.exp(s2 - v2[0:1])


def _peer_body(x_ref, g_ref, gf_ref, wq_ref, keys_ref, u_ref, vt_ref, o_ref,
               hnt_scr, n1_scr, f1_scr, q2_scr, e2_scr, acc_scr, h_even, h_odd, *, ec, final_norm):
    c = pl.program_id(1)
    tm = x_ref.shape[0]

    @pl.when(c == 0)
    def _select():
        hn = _rmsnorm(x_ref[...], g_ref[...])
        hnt_scr[...] = hn.T.astype(BF16)
        acc_scr[...] = jnp.zeros_like(acc_scr)

        def head(h, carry):
            r0 = pl.multiple_of(h * 2 * P_DHALF, 2 * P_DHALF)
            qt = _dot(wq_ref[pl.ds(r0, 2 * P_DHALF), :], hnt_scr[...]).astype(BF16)
            for lt in range(tm // 128):
                sl = slice(lt * 128, (lt + 1) * 128)
                s1 = _dot(keys_ref[2 * h], qt[0:P_DHALF, sl])
                s2 = _dot(keys_ref[2 * h + 1], qt[P_DHALF:2 * P_DHALF, sl])
                n1, f1, q2, e2 = _peer_select(s1, s2)
                n1_scr[h, :, sl] = n1
                f1_scr[h, :, sl] = f1
                q2_scr[h, :, sl] = q2.astype(BF16)
                e2_scr[h, :, sl] = e2.astype(BF16)
            return carry

        lax.fori_loop(0, P_HEADS, head, 0)

    nc = N_EXP // ec
    assert nc % 2 == 0

    def weighted_acts(h_out):
        act = _gelu(_dot(u_ref[...], hnt_scr[...])).astype(BF16)
        for ai in range(ec // N_KEYS):
            a = c * (ec // N_KEYS) + ai
            w = None
            for h in range(P_HEADS):
                n1 = n1_scr[h, pl.ds(a, 1), :].astype(BF16)
                f1 = f1_scr[h, pl.ds(a, 1), :].astype(BF16)
                term = jnp.where(q2_scr[h] < n1, e2_scr[h] * f1, jnp.zeros((), BF16))
                w = term if w is None else w + term
            rows = slice(ai * N_KEYS, (ai + 1) * N_KEYS)
            h_out[rows, :] = w * act[rows]

    @pl.when(c == 0)
    def _first():
        weighted_acts(h_even)

    @pl.when(jnp.logical_and(c < nc, c % 2 == 1))
    def _odd():
        acc_scr[...] += _dot(vt_ref[...], h_even[...])
        weighted_acts(h_odd)

    @pl.when(jnp.logical_and(jnp.logical_and(c > 0, c < nc), c % 2 == 0))
    def _even():
        acc_scr[...] += _dot(vt_ref[...], h_odd[...])
        weighted_acts(h_even)

    @pl.when(c == nc)
    def _finish():
        y = x_ref[...] + (acc_scr[...] + _dot(vt_ref[...], h_odd[...])).T
        if final_norm:
            y = _rmsnorm(y, gf_ref[...])
        o_ref[...] = y


def _peer(x, g, gf, wq_t, keys, u, vt, final_norm, tm=256, ec=1024):
    n, d = x.shape
    nc = N_EXP // ec
    const2 = lambda i, c: (0, 0)
    return pl.pallas_call(
        functools.partial(_peer_body, ec=ec, final_norm=final_norm),
        grid=(n // tm, nc + 1),
        in_specs=[pl.BlockSpec((tm, d), lambda i, c: (i, 0)),
                  pl.BlockSpec((1, d), const2), pl.BlockSpec((1, d), const2),
                  pl.BlockSpec(wq_t.shape, const2),
                  pl.BlockSpec(keys.shape, lambda i, c: (0, 0, 0)),
                  pl.BlockSpec((ec, d), lambda i, c: (jnp.minimum(c, nc - 1), 0)),
                  pl.BlockSpec((d, ec), lambda i, c: (0, jnp.maximum(c - 1, 0)))],
        out_specs=pl.BlockSpec((tm, d), lambda i, c: (i, 0)),
        out_shape=jax.ShapeDtypeStruct((n, d), F32),
        scratch_shapes=[pltpu.VMEM((d, tm), BF16)]
        + [pltpu.VMEM((P_HEADS, N_KEYS, tm), F32)] * 2
        + [pltpu.VMEM((P_HEADS, N_KEYS, tm), BF16)] * 2
        + [pltpu.VMEM((d, tm), F32), pltpu.VMEM((ec, tm), BF16), pltpu.VMEM((ec, tm), BF16)],
        compiler_params=_cparams("parallel", "arbitrary"),
        name="peer",
    )(x, g.reshape(1, d), gf.reshape(1, d), wq_t, keys, u, vt)


def _unit_lower_inverses(ms, order):
    ps = [-m for m in ms]
    invs = list(ps)
    span = 2
    while span < order:
        ps = [_dot_x3(p, p) for p in ps]
        invs = [inv + p + _dot_x3(inv, p) for inv, p in zip(invs, ps)]
        span *= 2
    row = lax.broadcasted_iota(jnp.int32, ms[0].shape, 0)
    col = lax.broadcasted_iota(jnp.int32, ms[0].shape, 1)
    eye = jnp.where(row == col, 1.0, 0.0)
    return [inv + eye for inv in invs]


def _gdn_body(qkv_ref, z_ref, ba_ref, cb_ref, s0_ref, wc_ref, prm_ref, gn_ref, o_ref, s_ref,
              xc_scr, s_scr, *, valid):
    t = pl.program_id(1)
    c = GDN_CHUNK
    r = qkv_ref.shape[1]

    @pl.when(t == 0)
    def _():
        xc_scr[0:8, :] = cb_ref[0]
        s_scr[...] = s0_ref[0]

    xc_scr[8:8 + r, :] = qkv_ref[0]
    if r < c:
        xc_scr[8 + r:8 + c, :] = jnp.zeros((c - r, QKV_C), F32)
    wc = wc_ref[...]
    conv = wc[0:1] * xc_scr[pl.ds(5, c), :]
    for j in range(1, CONV_W):
        conv = conv + wc[j:j + 1] * xc_scr[pl.ds(5 + j, c), :]
    xc_scr[0:8, :] = xc_scr[c:c + 8, :]
    act = _silu(conv)

    row = lax.broadcasted_iota(jnp.int32, (c, 1), 0)
    live = row < valid
    ba = ba_ref[0]
    if r < c:
        ba = jnp.concatenate([ba, jnp.zeros((c - r, 128), F32)], axis=0)
    prm = prm_ref[...]
    beta_all = jnp.where(live, jax.nn.sigmoid(ba), 0.0)
    g_all = jnp.where(live, prm[0:1] * _softplus(ba + prm[1:2]), 0.0)
    gc_all = g_all
    s = 1
    while s < c:
        gc_all = gc_all + _shift_rows(gc_all, s, 0.0)
        s *= 2
    gct = gc_all.T
    ri = lax.broadcasted_iota(jnp.int32, (c, c), 0)
    ci = lax.broadcasted_iota(jnp.int32, (c, c), 1)
    tri = ri >= ci
    strict = ri > ci
    z = z_ref[0]
    if r < c:
        z = jnp.concatenate([z, jnp.zeros((c - r, H_C * DV_C), F32)], axis=0)
    per_head = []
    for h in range(H_C):
        q = act[:, h * DK_C:(h + 1) * DK_C]
        k = act[:, (H_C + h) * DK_C:(H_C + h + 1) * DK_C]
        v = act[:, (2 * H_C + h) * DK_C:(2 * H_C + h + 1) * DK_C]
        q = q * lax.rsqrt(jnp.sum(q * q, -1, keepdims=True) + 1e-6) * (DK_C ** -0.5)
        k = k * lax.rsqrt(jnp.sum(k * k, -1, keepdims=True) + 1e-6)
        beta = beta_all[:, h:h + 1]
        gcol = gc_all[:, H_C + h:H_C + h + 1]
        grow = gct[H_C + h:H_C + h + 1, :]
        glast = gc_all[c - 1:c, H_C + h:H_C + h + 1]
        lmat = jnp.where(tri, jnp.exp(jnp.where(tri, gcol - grow, 0.0)), 0.0)
        kb = k * beta
        kbf = k.astype(BF16)
        m = jnp.where(strict, _dot_nt(kb.astype(BF16), kbf) * lmat, 0.0)
        eg = jnp.exp(gcol)
        aqk = (_dot_nt(q.astype(BF16), kbf) * lmat).astype(BF16)
        per_head.append(dict(m=m, vb=(v * beta).astype(BF16), kbe=(kb * eg).astype(BF16), aqk=aqk,
                             qd=(q * eg).astype(BF16), kdt=(k * jnp.exp(glast - gcol)).T.astype(BF16),
                             decay=jnp.exp(glast)))
    tinvs = _unit_lower_inverses([ph["m"] for ph in per_head], min(valid, c))
    for h, (ph, tinv) in enumerate(zip(per_head, tinvs)):
        tinv = tinv.astype(BF16)
        u = _dot(tinv, ph["vb"])
        w = _dot(tinv, ph["kbe"])
        st = s_scr[h]
        stb = st.astype(BF16)
        v_new = u - _dot(w.astype(BF16), stb)
        o = _dot(ph["qd"], stb) + _dot(ph["aqk"], v_new.astype(BF16))
        s_scr[h] = st * ph["decay"] + _dot(ph["kdt"], v_new.astype(BF16))
        o = o * lax.rsqrt(jnp.mean(o * o, -1, keepdims=True) + EPS) * gn_ref[...]
        o = o * _silu(z[:, h * DV_C:(h + 1) * DV_C])
        o_ref[0, :, h * DV_C:(h + 1) * DV_C] = o[0:r].astype(o_ref.dtype)

    @pl.when(t == pl.num_programs(1) - 1)
    def _():
        s_ref[0] = s_scr[...]


def _gdn(qkv, z, ba, cbuf8, s0, wc, prm, gn, valid):
    b, t, _ = qkv.shape
    r = min(t, GDN_CHUNK)
    tile = lambda wd: pl.BlockSpec((1, r, wd), lambda i, j: (i, j, 0))
    return pl.pallas_call(
        functools.partial(_gdn_body, valid=valid),
        grid=(b, t // r),
        in_specs=[tile(QKV_C), tile(H_C * DV_C), tile(128),
                  pl.BlockSpec((1, 8, QKV_C), lambda i, j: (i, 0, 0)),
                  pl.BlockSpec((1, H_C, DK_C, DV_C), lambda i, j: (i, 0, 0, 0)),
                  pl.BlockSpec((CONV_W, QKV_C), lambda i, j: (0, 0)),
                  pl.BlockSpec((8, 128), lambda i, j: (0, 0)),
                  pl.BlockSpec((1, DV_C), lambda i, j: (0, 0))],
        out_specs=[tile(H_C * DV_C),
                   pl.BlockSpec((1, H_C, DK_C, DV_C), lambda i, j: (i, 0, 0, 0))],
        out_shape=[jax.ShapeDtypeStruct((b, t, H_C * DV_C), F32 if r < GDN_CHUNK else BF16),
                   jax.ShapeDtypeStruct((b, H_C, DK_C, DV_C), F32)],
        scratch_shapes=[pltpu.VMEM((GDN_CHUNK + 8, QKV_C), F32), pltpu.VMEM((H_C, DK_C, DV_C), F32)],
        compiler_params=_cparams("parallel", "arbitrary"),
        name="gdn",
    )(qkv, z, ba, cbuf8, s0, wc, prm, gn)


def _block_diag(w):
    n, d, e = w.shape
    eye = jnp.eye(n, dtype=w.dtype)
    return (w[:, :, None, :] * eye[:, None, :, None]).reshape(n * d, n * e)


def _pad_rows(x, rows, front=0):
    return jnp.pad(x, ((0, 0), (front, rows - x.shape[1] - front), (0, 0)))


def _layer_a_weights(w_in, w_pool):
    d = w_in.shape[0]
    q_a, kv_a = H_A * DH_A, 3 * 2 * G_A * DH_A
    wq = w_in[:, :q_a].reshape(d, G_A, R_A, DH_A)
    zeros = jnp.zeros_like(wq)
    wq = jnp.stack([jnp.concatenate([wq[:, 0], zeros[:, 0]], -1),
                    jnp.concatenate([zeros[:, 1], wq[:, 1]], -1)], axis=1).reshape(d, H_A * 128)
    i1 = q_a + kv_a
    i2 = i1 + 3 * H_A
    wg = jnp.pad(w_in[:, i1:i2], ((0, 0), (0, 128 - 3 * H_A)))
    w = jnp.concatenate([wq, w_in[:, q_a:i1], wg, w_in[:, i2:]], axis=1).astype(BF16)
    wp = jnp.concatenate([jnp.broadcast_to(w_pool[0][:, None], (CMP_BLOCK, 128)),
                          jnp.broadcast_to(w_pool[1][:, None], (CMP_BLOCK, 128))], axis=1)
    return w, wp


def kernel(x_prompt, x_sample, cache_cmp_kv, cache_sel_kv, cache_win_kv, state_rg_h, state_rg_conv, state_gdn_S, state_gdn_conv, page_table, norm_mix, norm_ffn, norm_final, w_in_a, w_cmp_pool, w_rg_conv, b_rg_conv, w_rg_a, b_rg_a, w_rg_x, b_rg_x, rg_lambda, w_out_a, w_in_c, w_c_conv, c_A_log, c_dt_bias, c_norm, w_out_c, peer_w_q, peer_keys, peer_u, peer_v):
    bp, tp, d = x_prompt.shape
    bs, ts, _ = x_sample.shape
    np_, ns = bp * tp, bs * ts
    xp = x_prompt.reshape(np_, d)
    xs = x_sample.reshape(ns, d)
    kv_w = 3 * 2 * G_A * DH_A

    def peer_layer(x, layer, final):
        wq_t = peer_w_q[layer].astype(BF16).T
        keys = peer_keys[layer].reshape(2 * P_HEADS, N_KEYS, P_DHALF).astype(BF16)
        u = peer_u[layer].astype(BF16)
        vt = peer_v[layer].astype(BF16).T
        return [_peer(xx, norm_ffn[layer], norm_final, wq_t, keys, u, vt, final) for xx in x]

    w0, wp = _layer_a_weights(w_in_a[0], w_cmp_pool[0])
    qo, ko, go, ro = 0, H_A * 128, H_A * 128 + kv_w, H_A * 128 + kv_w + 128
    outs = ((qo, H_A * 128, None), (ko, kv_w, None), (ko + 256, 512, None), (go, 128, "sigmoid"),
            (ro, D_RNN, None), (ro + D_RNN, D_RNN, None))
    dts = (BF16, F32, BF16, F32, F32, F32)
    qp, kvp, kvbp, gtp, xrp, ygp = _norm_proj(xp, norm_mix[0], w0, outs, dts, 512)
    qs, kvs, _, gts, xrs, ygs = _norm_proj(xs, norm_mix[0], w0, outs, dts, min(512, ns))

    kvp3 = kvp.reshape(bp, tp, kv_w)
    pe, po = _nsa_pool(kvp3, wp)
    att_p = _nsa_prompt(qp.reshape(bp, tp, H_A * 128), gtp.reshape(bp, tp, 128), pe, po,
                        kvbp.reshape(bp, tp, 512)).reshape(np_, H_A * DH_A)

    kvs3 = kvs.reshape(bs, ts, kv_w)
    n_pool, page = cache_cmp_kv.shape[1], cache_cmp_kv.shape[2]
    att_s8, win_new = _nsa_sample(
        page_table,
        _pad_rows(qs.astype(F32).reshape(bs, ts, H_A * 128), 8),
        _pad_rows(gts.reshape(bs, ts, 128), 8),
        _pad_rows(kvs3, 16), wp,
        cache_win_kv[0].reshape(bs, -1, 256),
        cache_cmp_kv[0].reshape(n_pool, page, 256),
        cache_sel_kv[0].reshape(n_pool, page, 256), ts)
    att_s = att_s8[:, :ts].reshape(ns, H_A * DH_A).astype(BF16)

    rg = (w_rg_conv[0], b_rg_conv[0].reshape(1, D_RNN), _block_diag(w_rg_a[0]).astype(BF16),
          b_rg_a[0].reshape(1, D_RNN), _block_diag(w_rg_x[0]).astype(BF16), b_rg_x[0].reshape(1, D_RNN),
          rg_lambda[0].reshape(1, D_RNN))
    xrp3 = xrp.reshape(bp, tp, D_RNN)
    rec_p, hl_p = _rglru_prompt(xrp3, ygp.reshape(bp, tp, D_RNN), *rg)
    xrs3 = xrs.reshape(bs, ts, D_RNN)
    rec_s, hl_s = _rglru_sample(xrs3.transpose(1, 0, 2), ygs.reshape(bs, ts, D_RNN).transpose(1, 0, 2),
                                state_rg_conv[0].transpose(1, 0, 2), state_rg_h[0], *rg)
    rec_s = rec_s.transpose(1, 0, 2).reshape(ns, D_RNN).astype(BF16)

    wo = w_out_a[0].astype(BF16)
    wo_att, wo_rec = wo[:H_A * DH_A], wo[H_A * DH_A:]
    xp = _out_proj([att_p, rec_p.reshape(np_, D_RNN)], [wo_att, wo_rec], xp, 512)
    xs = _out_proj([att_s, rec_s], [wo_att, wo_rec], xs, min(512, ns))
    xp, xs = peer_layer([xp, xs], 0, False)

    d_c = H_C * DV_C
    w1 = jnp.pad(w_in_c[0], ((0, 0), (0, 128 - 2 * H_C))).astype(BF16)
    outs1 = ((0, QKV_C, None), (QKV_C, d_c, None), (QKV_C + d_c, 128, None))
    dts1 = (F32, F32, F32)
    qkv_p, z_p, ba_p = _norm_proj(xp, norm_mix[1], w1, outs1, dts1, 512)
    qkv_s, z_s, ba_s = _norm_proj(xs, norm_mix[1], w1, outs1, dts1, min(512, ns))
    lane = jnp.arange(128)
    hsel = jnp.clip(lane - H_C, 0, H_C - 1)
    in_g = (lane >= H_C) & (lane < 2 * H_C)
    prm = jnp.zeros((8, 128), F32)
    prm = prm.at[0].set(jnp.where(in_g, -jnp.exp(c_A_log[0])[hsel], 0.0))
    prm = prm.at[1].set(jnp.where(in_g, c_dt_bias[0][hsel], 0.0))
    gn = c_norm[0].reshape(1, DV_C)
    qkv_p3 = qkv_p.reshape(bp, tp, QKV_C)
    o_p, gs_p = _gdn(qkv_p3, z_p.reshape(bp, tp, d_c), ba_p.reshape(bp, tp, 128),
                     jnp.zeros((bp, 8, QKV_C), F32), jnp.zeros((bp, H_C, DK_C, DV_C), F32),
                     w_c_conv[0], prm, gn, GDN_CHUNK)
    qkv_s3 = qkv_s.reshape(bs, ts, QKV_C)
    o_s, gs_s = _gdn(_pad_rows(qkv_s3, 8), _pad_rows(z_s.reshape(bs, ts, d_c), 8),
                     _pad_rows(ba_s.reshape(bs, ts, 128), 8),
                     _pad_rows(state_gdn_conv[0], 8, front=8 - (CONV_W - 1)), state_gdn_S[0],
                     w_c_conv[0], prm, gn, ts)
    wo_c = w_out_c[0].astype(BF16)
    xp = _out_proj([o_p.reshape(np_, d_c)], [wo_c], xp, 512)
    xs = _out_proj([o_s[:, :ts].reshape(ns, d_c).astype(BF16)], [wo_c], xs, min(512, ns))
    yp, ys = peer_layer([xp, xs], 1, True)

    kvp6 = kvp3.reshape(bp, tp, 3, 2, G_A, DH_A)
    kvs6 = kvs3.reshape(bs, ts, 3, 2, G_A, DH_A)
    wlen = min(WINDOW, tp)
    cw = CONV_W - 1
    return (yp.reshape(bp, tp, d), ys.reshape(bs, ts, d),
            kvp6[None, :, :, 0], kvs6[None, :, :, 0], kvp6[None, :, :, 1], kvs6[None, :, :, 1],
            kvp6[None, :, tp - wlen:, 2], win_new.reshape(bs, -1, 2, G_A, DH_A)[None],
            hl_p[None], hl_s[None], xrp3[None, :, tp - cw:], xrs3[None, :, ts - cw:],
            gs_p[None], gs_s[None], qkv_p3[None, :, tp - cw:], qkv_s3[None, :, ts - cw:])
```

```python
import functools
import math

import jax
import jax.numpy as jnp
from jax import lax
from jax.experimental import pallas as pl
from jax.experimental.pallas import tpu as pltpu

F32 = jnp.float32
BF16 = jnp.bfloat16

D_MODEL = 1024
H_A, G_A, R_A, DH_A = 8, 2, 4, 64
CMP_BLOCK, SEL_BLOCK, TOP_N, WINDOW, Q_BLOCK = 32, 64, 16, 512, 128
FORCE_BONUS = 1.0e3
NEG_INF = -1.0e30
N_SEL_LANES = 128
D_RNN, RG_C, CONV_W = 512, 8.0, 4
H_C, DK_C, DV_C = 8, 128, 128
QKV_C = 3 * H_C * DK_C
GDN_CHUNK = 128
P_HEADS, N_KEYS, P_DHALF, P_TOPK = 8, 128, 128, 16
N_EXP = N_KEYS * N_KEYS
EPS = 1e-6

VMEM_LIMIT = 56 * 1024 * 1024


def _cparams(*sem):
    return pltpu.CompilerParams(dimension_semantics=sem, vmem_limit_bytes=VMEM_LIMIT)


def _rmsnorm(x, g):
    return x * lax.rsqrt(jnp.mean(x * x, -1, keepdims=True) + EPS) * g


def _gelu(x):
    return 0.5 * x * (1.0 + jnp.tanh(math.sqrt(2.0 / math.pi) * (x + 0.044715 * (x * x * x))))


def _softplus(x):
    return jnp.maximum(x, 0.0) + jnp.log1p(jnp.exp(-jnp.abs(x)))


def _silu(x):
    return x * jax.nn.sigmoid(x)


def _neg_expm1(x):
    t = jnp.tanh(0.5 * x)
    return -2.0 * t / (1.0 - t)


def _dot(a, b):
    return jnp.dot(a, b, preferred_element_type=F32)


def _dot_nt(a, b):
    return lax.dot_general(a, b, (((1,), (1,)), ((), ())), preferred_element_type=F32)


def _dot_x3(a, b):
    a_hi = a.astype(BF16)
    b_hi = b.astype(BF16)
    a_lo = (a - a_hi.astype(F32)).astype(BF16)
    b_lo = (b - b_hi.astype(F32)).astype(BF16)
    return _dot(a_hi, b_hi) + (_dot(a_hi, b_lo) + _dot(a_lo, b_hi))


def _shift_rows(x, s, fill):
    rolled = pltpu.roll(x, s, 0)
    row = lax.broadcasted_iota(jnp.int32, x.shape, 0)
    return jnp.where(row >= s, rolled, fill)


def _topk_pos_multi(ss, axis, k, idx=None):
    shape = ss[0].shape
    if idx is None:
        idx = lax.broadcasted_iota(jnp.int32, shape, axis).astype(F32)
    vshape = tuple(k if d == axis else shape[d] for d in range(len(shape)))
    vidx = lax.broadcasted_iota(jnp.int32, vshape, axis)
    n = len(ss)
    ss = list(ss)
    poss = [jnp.full(shape, float(k), F32) for _ in range(n)]
    valss = [jnp.zeros(vshape, F32) for _ in range(n)]
    for i in range(k):
        ms = [jnp.max(s, axis=axis, keepdims=True) for s in ss]
        js = [jnp.min(jnp.where(s == m, idx, 1e9), axis=axis, keepdims=True) for s, m in zip(ss, ms)]
        ohs = [idx == j for j in js]
        poss = [jnp.where(oh, float(i), pos) for oh, pos in zip(ohs, poss)]
        ss = [jnp.where(oh, -jnp.inf, s) for oh, s in zip(ohs, ss)]
        valss = [jnp.where(vidx == i, m, vals) for m, vals in zip(ms, valss)]
    return poss, valss


def _topk_pos(s, axis, k, idx=None):
    poss, valss = _topk_pos_multi([s], axis, k, idx)
    return poss[0], valss[0]


def _topk_pos_multi_fast(ss, axis, k, idx=None):
    shape = ss[0].shape
    vshape = tuple(k if d == axis else shape[d] for d in range(len(shape)))
    vidx = lax.broadcasted_iota(jnp.int32, vshape, axis)
    n = len(ss)
    rest = list(ss)
    poss = [jnp.full(shape, float(k), F32) for _ in range(n)]
    valss = [jnp.zeros(vshape, F32) for _ in range(n)]
    for i in range(k):
        ms = [jnp.max(s, axis=axis, keepdims=True) for s in rest]
        ohs = [s == m for s, m in zip(rest, ms)]
        poss = [jnp.where(oh, float(i), pos) for oh, pos in zip(ohs, poss)]
        rest = [jnp.where(oh, -jnp.inf, s) for oh, s in zip(ohs, rest)]
        valss = [jnp.where(vidx == i, m, vals) for m, vals in zip(ms, valss)]
    removed = sum(jnp.sum(jnp.where(pos < k, 1.0, 0.0)) for pos in poss)
    expected = float(n * k * math.prod(shape) // shape[axis])
    return lax.cond(removed == expected,
                    lambda: (tuple(poss), tuple(valss)),
                    lambda: tuple(tuple(x) for x in _topk_pos_multi(ss, axis, k, idx)))


def _norm_proj_body(x_ref, g_ref, w_ref, *out_refs, outs):
    h = _rmsnorm(x_ref[...], g_ref[...]).astype(BF16)
    for o_ref, (off, width, act) in zip(out_refs, outs):
        z = _dot(h, w_ref[:, off:off + width])
        if act == "sigmoid":
            z = jax.nn.sigmoid(z)
        o_ref[...] = z.astype(o_ref.dtype)


def _norm_proj(x, g, w, outs, dtypes, tm):
    n, d = x.shape
    body = functools.partial(_norm_proj_body, outs=outs)
    return pl.pallas_call(
        body,
        grid=(n // tm,),
        in_specs=[pl.BlockSpec((tm, d), lambda i: (i, 0)),
                  pl.BlockSpec((1, d), lambda i: (0, 0)),
                  pl.BlockSpec(w.shape, lambda i: (0, 0))],
        out_specs=[pl.BlockSpec((tm, wd), lambda i: (i, 0)) for (_, wd, _) in outs],
        out_shape=[jax.ShapeDtypeStruct((n, wd), dt) for (_, wd, _), dt in zip(outs, dtypes)],
        compiler_params=_cparams("parallel"),
        name="norm_proj",
    )(x, g.reshape(1, d), w)


def _out_proj_body(*refs, n_in):
    a_refs, w_refs = refs[:n_in], refs[n_in:2 * n_in]
    x_ref, o_ref = refs[2 * n_in], refs[2 * n_in + 1]
    acc = x_ref[...]
    for a, w in zip(a_refs, w_refs):
        acc = acc + _dot(a[...], w[...])
    o_ref[...] = acc


def _out_proj(acts, ws, x, tm):
    n, d = x.shape
    n_in = len(acts)
    return pl.pallas_call(
        functools.partial(_out_proj_body, n_in=n_in),
        grid=(n // tm,),
        in_specs=([pl.BlockSpec((tm, a.shape[1]), lambda i: (i, 0)) for a in acts]
                  + [pl.BlockSpec(w.shape, lambda i: (0, 0)) for w in ws]
                  + [pl.BlockSpec((tm, d), lambda i: (i, 0))]),
        out_specs=pl.BlockSpec((tm, d), lambda i: (i, 0)),
        out_shape=jax.ShapeDtypeStruct((n, d), F32),
        compiler_params=_cparams("parallel"),
        name="out_proj",
    )(*acts, *ws, x)


def _rg_gates(xf, wa, ba, wx, bx, lam):
    xb = xf.astype(BF16)
    r = jax.nn.sigmoid(_dot(xb, wa) + ba)
    i = jax.nn.sigmoid(_dot(xb, wx) + bx)
    log_a = -RG_C * r * _softplus(-lam)
    a = jnp.exp(log_a)
    b = jnp.sqrt(_neg_expm1(2.0 * log_a)) * (i * xf)
    return a, b


def _rglru_prompt_body(xr_ref, yg_ref, wc_ref, bc_ref, wa_ref, ba_ref, wx_ref, bx_ref, lam_ref,
                       rec_ref, hl_ref, xc_scr, h_scr, *, rt):
    t = pl.program_id(1)

    @pl.when(t == 0)
    def _():
        xc_scr[0:8, :] = jnp.zeros((8, D_RNN), F32)
        h_scr[...] = jnp.zeros_like(h_scr)

    xc_scr[8:8 + rt, :] = xr_ref[0]
    wc = wc_ref[...]
    conv = wc[0:1] * xc_scr[pl.ds(5, rt), :]
    for j in range(1, CONV_W):
        conv = conv + wc[j:j + 1] * xc_scr[pl.ds(5 + j, rt), :]
    xc_scr[0:8, :] = xc_scr[rt:rt + 8, :]
    xf = conv + bc_ref[...]
    a, b = _rg_gates(xf, wa_ref[...], ba_ref[...], wx_ref[...], bx_ref[...], lam_ref[...])
    s = 1
    while s < rt:
        b = a * _shift_rows(b, s, 0.0) + b
        a = a * _shift_rows(a, s, 1.0)
        s *= 2
    h = a * h_scr[0:1, :] + b
    h_last = h[rt - 1:rt, :]
    h_scr[0:1, :] = h_last
    hl_ref[0] = h_last
    rec_ref[0] = (h * _gelu(yg_ref[0])).astype(rec_ref.dtype)


def _rglru_prompt(xr, yg, wc, bc, wa, ba, wx, bx, lam, rt=256):
    b, t, d = xr.shape
    vec = lambda: pl.BlockSpec((1, d), lambda i, j: (0, 0))
    mat = lambda: pl.BlockSpec((d, d), lambda i, j: (0, 0))
    rec, hl = pl.pallas_call(
        functools.partial(_rglru_prompt_body, rt=rt),
        grid=(b, t // rt),
        in_specs=[pl.BlockSpec((1, rt, d), lambda i, j: (i, j, 0)),
                  pl.BlockSpec((1, rt, d), lambda i, j: (i, j, 0)),
                  pl.BlockSpec((CONV_W, d), lambda i, j: (0, 0)),
                  vec(), mat(), vec(), mat(), vec(), vec()],
        out_specs=[pl.BlockSpec((1, rt, d), lambda i, j: (i, j, 0)),
                   pl.BlockSpec((1, 1, d), lambda i, j: (i, 0, 0))],
        out_shape=[jax.ShapeDtypeStruct((b, t, d), BF16),
                   jax.ShapeDtypeStruct((b, 1, d), F32)],
        scratch_shapes=[pltpu.VMEM((rt + 8, d), F32), pltpu.VMEM((8, d), F32)],
        compiler_params=_cparams("arbitrary", "arbitrary"),
        name="rglru_prompt",
    )(xr, yg, wc, bc, wa, ba, wx, bx, lam)
    return rec, hl[:, 0]


def _rglru_sample_body(xr_ref, yg_ref, buf_ref, h0_ref, wc_ref, bc_ref, wa_ref, ba_ref, wx_ref,
                       bx_ref, lam_ref, rec_ref, hl_ref, *, t_len):
    xs = [buf_ref[j] for j in range(CONV_W - 1)] + [xr_ref[j] for j in range(t_len)]
    wc = wc_ref[...]
    h = h0_ref[...]
    for t in range(t_len):
        conv = wc[0:1] * xs[t]
        for j in range(1, CONV_W):
            conv = conv + wc[j:j + 1] * xs[t + j]
        xf = conv + bc_ref[...]
        a, b = _rg_gates(xf, wa_ref[...], ba_ref[...], wx_ref[...], bx_ref[...], lam_ref[...])
        h = a * h + b
        rec_ref[t] = h * _gelu(yg_ref[t])
    hl_ref[...] = h


def _rglru_sample(xr, yg, buf, h0, wc, bc, wa, ba, wx, bx, lam):
    t_len, b, d = xr.shape
    return pl.pallas_call(
        functools.partial(_rglru_sample_body, t_len=t_len),
        out_shape=[jax.ShapeDtypeStruct((t_len, b, d), F32),
                   jax.ShapeDtypeStruct((b, d), F32)],
        compiler_params=pltpu.CompilerParams(vmem_limit_bytes=VMEM_LIMIT),
        name="rglru_sample",
    )(xr, yg, buf, h0, wc, bc, wa, ba, wx, bx, lam)


def _pool_blocks(x3, w):
    first = jnp.sum(x3[:, :CMP_BLOCK, :] * w[None], axis=1)
    second = jnp.sum(x3[:, CMP_BLOCK:, :] * w[None], axis=1)
    return first, second


def _nsa_pool_body(x_ref, w_ref, pe_ref, po_ref):
    x = x_ref[0]
    nb = x.shape[0] // SEL_BLOCK
    pe, po = _pool_blocks(x.reshape(nb, SEL_BLOCK, x.shape[1]), w_ref[...])
    pe_ref[0] = pe
    po_ref[0] = po


def _nsa_pool(kv, w, rows=1024):
    b, t, _ = kv.shape
    nb = rows // SEL_BLOCK
    return pl.pallas_call(
        _nsa_pool_body,
        grid=(b, t // rows),
        in_specs=[pl.BlockSpec((1, rows, 256), lambda i, j: (i, j, 0)),
                  pl.BlockSpec((CMP_BLOCK, 256), lambda i, j: (0, 0))],
        out_specs=[pl.BlockSpec((1, nb, 256), lambda i, j: (i, j, 0))] * 2,
        out_shape=[jax.ShapeDtypeStruct((b, t // SEL_BLOCK, 256), F32)] * 2,
        compiler_params=_cparams("parallel", "parallel"),
        name="nsa_pool",
    )(kv, w)


def _cmp_branch(q, qpos, pe, po, nq):
    jl = lax.broadcasted_iota(jnp.int32, (1, N_SEL_LANES), 1)
    vis_e = (SEL_BLOCK * jl + (CMP_BLOCK - 1)) <= qpos
    vis_o = (SEL_BLOCK * jl + (SEL_BLOCK - 1)) <= qpos
    s_e = jnp.where(vis_e, _dot_nt(q, pe[:, :128].astype(BF16)), NEG_INF)
    s_o = jnp.where(vis_o, _dot_nt(q, po[:, :128].astype(BF16)), NEG_INF)
    m = jnp.maximum(jnp.max(s_e, -1, keepdims=True), jnp.max(s_o, -1, keepdims=True))
    p_e = jnp.where(vis_e, jnp.exp(s_e - m), 0.0)
    p_o = jnp.where(vis_o, jnp.exp(s_o - m), 0.0)
    den = jnp.sum(p_e, -1, keepdims=True) + jnp.sum(p_o, -1, keepdims=True)
    inv = jnp.where(den > 0.0, 1.0 / jnp.maximum(den, 1e-30), 0.0)
    p_e = p_e * inv
    p_o = p_o * inv
    o = _dot(p_e.astype(BF16), pe[:, 128:].astype(BF16)) + _dot(p_o.astype(BF16), po[:, 128:].astype(BF16))
    p = p_e + p_o
    imp = p[0:nq]
    for r in range(1, R_A):
        imp = imp + p[r * nq:(r + 1) * nq]
    return o, imp


def _select_blocks(imp, qpos, blk_axis):
    shape = (N_SEL_LANES, 1) if blk_axis == 0 else (1, N_SEL_LANES)
    blk = lax.broadcasted_iota(jnp.int32, shape, blk_axis)
    valid = blk * SEL_BLOCK <= qpos
    forced = jnp.logical_or(blk == 0, blk == (qpos >> 6))
    score = jnp.where(valid, imp + jnp.where(forced, FORCE_BONUS, 0.0), -jnp.inf)
    poss, _ = _topk_pos_multi_fast([score], blk_axis, TOP_N)
    return jnp.where(valid, jnp.where(poss[0] < TOP_N, 1.0, 0.0), 0.0)


def _expand_sel(sel, first_blk, nkeys):
    bi = lax.broadcasted_iota(jnp.int32, (N_SEL_LANES, nkeys), 0)
    ki = lax.broadcasted_iota(jnp.int32, (N_SEL_LANES, nkeys), 1)
    e = jnp.where(bi - first_blk == (ki >> 6), 1.0, 0.0).astype(BF16)
    x = _dot(sel.astype(BF16), e)
    return jnp.concatenate([x] * R_A, axis=0)


def _masked_attend(q, k, v, ok):
    s = jnp.where(ok, _dot_nt(q, k), NEG_INF)
    m = jnp.max(s, -1, keepdims=True)
    p = jnp.where(ok, jnp.exp(s - m), 0.0)
    l = jnp.sum(p, -1, keepdims=True)
    return _dot(p.astype(BF16), v) / l


def _window_ok(qpos, kpos):
    d = qpos - kpos
    return jnp.logical_and(d >= 0, d <= WINDOW)


def _gate_and_place(gt, branches):
    heads = []
    for h in range(H_A):
        o = None
        for br in range(3):
            term = gt[:, 3 * h + br:3 * h + br + 1] * branches[br][h]
            o = term if o is None else o + term
        if (h % 2) != h // R_A:
            o = pltpu.roll(o, DH_A, 1)
        heads.append(o)
    lane = lax.broadcasted_iota(jnp.int32, (1, 128), 1)
    return [jnp.where(lane < DH_A, heads[2 * j], heads[2 * j + 1]) for j in range(H_A // 2)]


def _split_heads(o, nq):
    return [o[r * nq:(r + 1) * nq] for r in range(R_A)]


SEL_CHUNK = 512


def _nsa_prompt_body(q_ref, gt_ref, pe_ref, po_ref, kv_ref, oh_ref, vt_ref, o_ref, acc_scr):
    i = pl.program_id(1)
    nq = Q_BLOCK
    base = i * nq
    pe, po = pe_ref[0], po_ref[0]
    rowi = lax.broadcasted_iota(jnp.int32, (R_A * nq, 1), 0)
    qpos = base + (rowi & (nq - 1))
    qpos_row = base + lax.broadcasted_iota(jnp.int32, (1, nq), 1)
    heads_c, heads_w, qats = [], [], []
    for g in range(G_A):
        q = jnp.concatenate([q_ref[0, :, (g * R_A + r) * 128:(g * R_A + r + 1) * 128] for r in range(R_A)], axis=0)
        q = q * jnp.asarray(DH_A ** -0.5, BF16)
        o_c, imp = _cmp_branch(q, qpos, pe, po, nq)
        heads_c += _split_heads(o_c, nq)
        sel_t = _select_blocks(imp.T, qpos_row, 0)
        selterm_t = jnp.where(sel_t > 0.5, 0.0, NEG_INF).astype(BF16)
        qh_t = [jnp.concatenate([qh.astype(F32).T.astype(BF16), selterm_t], axis=0) for qh in _split_heads(q, nq)]
        qats += [jnp.concatenate(qh_t[2 * j:2 * j + 2], axis=1) for j in range(R_A // 2)]

        wlen = WINDOW + nq
        ws = pl.multiple_of(jnp.maximum(i - WINDOW // nq, 0) * nq, nq)
        kw = kv_ref[0, pl.ds(ws, wlen), 256:384]
        vw = kv_ref[0, pl.ds(ws, wlen), 384:512]
        kpos = ws + lax.broadcasted_iota(jnp.int32, (1, wlen), 1)
        heads_w += _split_heads(_masked_attend(q, kw, vw, _window_ok(qpos, kpos)), nq)

    acc_scr[...] = jnp.zeros_like(acc_scr)

    def step(k0, ms, ls, diagonal):
        ka = jnp.concatenate([kv_ref[0, pl.ds(k0, SEL_CHUNK), 0:128], oh_ref[pl.ds(k0, SEL_CHUNK), :]], axis=1)
        vt = vt_ref[0, :, pl.ds(k0, SEL_CHUNK)]
        if diagonal:
            qpos_pair = base + (lax.broadcasted_iota(jnp.int32, (1, 2 * nq), 1) & (nq - 1))
            visible = k0 + lax.broadcasted_iota(jnp.int32, (SEL_CHUNK, 1), 0) <= qpos_pair
        ss = [_dot(ka, qat) for qat in qats]
        if diagonal:
            ss = [jnp.where(visible, s, NEG_INF) for s in ss]
        ms_new = [jnp.maximum(m, jnp.max(s, axis=0, keepdims=True)) for m, s in zip(ms, ss)]
        ps = [jnp.exp(s - m) for s, m in zip(ss, ms_new)]
        pvs = [_dot(vt, p.astype(BF16)) for p in ps]
        ls_new = []
        for j in range(n_pairs):
            alpha = jnp.exp(ms[j] - ms_new[j])
            ls_new.append(alpha * ls[j] + jnp.sum(ps[j], axis=0, keepdims=True))
            acc_scr[j] = alpha * acc_scr[j] + pvs[j]
        return tuple(ms_new), tuple(ls_new)

    def past_chunk(c, carry):
        return step(pl.multiple_of(c * SEL_CHUNK, SEL_CHUNK), carry[0], carry[1], False)

    n_pairs = H_A // 2
    init = (tuple(jnp.full((1, 2 * nq), NEG_INF, F32) for _ in range(n_pairs)),
            tuple(jnp.zeros((1, 2 * nq), F32) for _ in range(n_pairs)))
    n_past = (base + nq - 1) // SEL_CHUNK
    ms, ls = lax.fori_loop(0, n_past, past_chunk, init)
    _, ls = step(pl.multiple_of(n_past * SEL_CHUNK, SEL_CHUNK), ms, ls, True)
    heads_s = []
    for j in range(n_pairs):
        o_pair = acc_scr[j] / ls[j]
        heads_s += [o_pair[:, 0:nq].T, o_pair[:, nq:2 * nq].T]
    for j, tile in enumerate(_gate_and_place(gt_ref[0], (heads_c, heads_s, heads_w))):
        o_ref[0, :, j * 128:(j + 1) * 128] = tile.astype(o_ref.dtype)


def _nsa_prompt(qpad, gates, pe, po, kvb):
    b, t, _ = qpad.shape
    nq = Q_BLOCK
    blk_of_key = jnp.arange(t, dtype=jnp.int32)[:, None] // SEL_BLOCK
    onehot = (blk_of_key == jnp.arange(N_SEL_LANES, dtype=jnp.int32)[None, :]).astype(BF16)
    v_sel_t = kvb[:, :, 128:256].transpose(0, 2, 1)
    return pl.pallas_call(
        _nsa_prompt_body,
        grid=(b, t // nq),
        in_specs=[pl.BlockSpec((1, nq, H_A * 128), lambda i, j: (i, j, 0)),
                  pl.BlockSpec((1, nq, 128), lambda i, j: (i, j, 0)),
                  pl.BlockSpec((1, N_SEL_LANES, 256), lambda i, j: (i, 0, 0)),
                  pl.BlockSpec((1, N_SEL_LANES, 256), lambda i, j: (i, 0, 0)),
                  pl.BlockSpec((1, t, 512), lambda i, j: (i, 0, 0)),
                  pl.BlockSpec((t, N_SEL_LANES), lambda i, j: (0, 0)),
                  pl.BlockSpec((1, 128, t), lambda i, j: (i, 0, 0))],
        out_specs=pl.BlockSpec((1, nq, H_A * DH_A), lambda i, j: (i, j, 0)),
        out_shape=jax.ShapeDtypeStruct((b, t, H_A * DH_A), BF16),
        scratch_shapes=[pltpu.VMEM((H_A // 2, 128, 2 * nq), F32)],
        compiler_params=_cparams("parallel", "arbitrary"),
        name="nsa_prompt",
    )(qpad, gates, pe, po, kvb, onehot, v_sel_t)


def _nsa_sample_body(pt_ref, q_ref, gt_ref, kvn_ref, w_ref, win_ref, *rest, n_pages, page, past, tq):
    del pt_ref
    cmp_pages, sel_pages = rest[:n_pages], rest[n_pages:2 * n_pages]
    o_ref, wout_ref, cmp_scr, sel_scr, win_scr = rest[2 * n_pages:]
    nq = q_ref.shape[1]
    npad = kvn_ref.shape[1]
    lp = cmp_scr.shape[0]
    wbuf = win_ref.shape[1]
    kvn = kvn_ref[0]
    for p in range(n_pages):
        cmp_scr[p * page:(p + 1) * page, :] = cmp_pages[p][0]
        sel_scr[p * page:(p + 1) * page, :] = sel_pages[p][0].astype(BF16)
    cmp_scr[past:past + npad, :] = kvn[:, 0:256]
    sel_scr[past:past + npad, :] = kvn[:, 256:512].astype(BF16)
    cmp_scr[past + npad:lp, :] = jnp.zeros((lp - past - npad, 256), F32)
    sel_scr[past + npad:lp, :] = jnp.zeros((lp - past - npad, 256), BF16)
    win_scr[0:wbuf, :] = win_ref[0]
    win_scr[wbuf:wbuf + npad, :] = kvn[:, 512:768]
    wl = win_scr.shape[0]
    win_scr[wbuf + npad:wl, :] = jnp.zeros((wl - wbuf - npad, 256), F32)
    wout_ref[0] = win_scr[pl.ds(tq, wbuf), :]

    nb = lp // SEL_BLOCK
    pe, po = _pool_blocks(cmp_scr[...].reshape(nb, SEL_BLOCK, 256), w_ref[...])
    zpad = jnp.zeros((N_SEL_LANES - nb, 256), F32)
    pe = jnp.concatenate([pe, zpad], axis=0)
    po = jnp.concatenate([po, zpad], axis=0)

    rowi = lax.broadcasted_iota(jnp.int32, (R_A * nq, 1), 0)
    qpos = past + (rowi & (nq - 1))
    nsel_keys = past + 256
    kpos_s = lax.broadcasted_iota(jnp.int32, (1, nsel_keys), 1)
    kpos_w = (past - wbuf) + lax.broadcasted_iota(jnp.int32, (1, wl), 1)
    heads_c, heads_s, heads_w = [], [], []
    qs, imps = [], []
    for g in range(G_A):
        q = jnp.concatenate([q_ref[0, :, (g * R_A + r) * 128:(g * R_A + r + 1) * 128] for r in range(R_A)], axis=0)
        q = (q * DH_A ** -0.5).astype(BF16)
        o_c, imp = _cmp_branch(q, qpos, pe, po, nq)
        heads_c += _split_heads(o_c, nq)
        qs.append(q)
        imps.append(imp)
    sel_all = _select_blocks(jnp.concatenate(imps, axis=0), qpos[0:G_A * nq], 1)
    for g in range(G_A):
        q, sel = qs[g], sel_all[g * nq:(g + 1) * nq]
        ok = jnp.logical_and(_expand_sel(sel, 0, nsel_keys) > 0.5, kpos_s <= qpos)
        o_s = _masked_attend(q, sel_scr[0:nsel_keys, 0:128], sel_scr[0:nsel_keys, 128:256], ok)
        kw = win_scr[:, 0:128].astype(BF16)
        vw = win_scr[:, 128:256].astype(BF16)
        o_w = _masked_attend(q, kw, vw, _window_ok(qpos, kpos_w))
        heads_s += _split_heads(o_s, nq)
        heads_w += _split_heads(o_w, nq)
    for j, tile in enumerate(_gate_and_place(gt_ref[0], (heads_c, heads_s, heads_w))):
        o_ref[0, :, j * 128:(j + 1) * 128] = tile


def _nsa_sample(page_table, q8, gates8, kvn16, w, win, cmp_pages, sel_pages, tq):
    b, n_pages = page_table.shape
    page = cmp_pages.shape[1]
    past = n_pages * page
    wbuf = win.shape[1]
    lp = past + 512
    nq = q8.shape[1]

    def row(shape):
        return pl.BlockSpec((1,) + shape, lambda i, pt: (i, 0, 0))

    def page_spec(p):
        return pl.BlockSpec((1, page, 256), lambda i, pt, p=p: (pt[i, p], 0, 0))

    grid_spec = pltpu.PrefetchScalarGridSpec(
        num_scalar_prefetch=1,
        grid=(b,),
        in_specs=([row((nq, H_A * 128)), row((nq, 128)), row((kvn16.shape[1], 768)),
                   pl.BlockSpec((CMP_BLOCK, 256), lambda i, pt: (0, 0)), row((wbuf, 256))]
                  + [page_spec(p) for p in range(n_pages)] * 2),
        out_specs=[row((nq, H_A * DH_A)), row((wbuf, 256))],
        scratch_shapes=[pltpu.VMEM((lp, 256), F32), pltpu.VMEM((lp, 256), BF16),
                        pltpu.VMEM((wbuf + 128, 256), F32)],
    )
    return pl.pallas_call(
        functools.partial(_nsa_sample_body, n_pages=n_pages, page=page, past=past, tq=tq),
        grid_spec=grid_spec,
        out_shape=[jax.ShapeDtypeStruct((b, nq, H_A * DH_A), F32),
                   jax.ShapeDtypeStruct((b, wbuf, 256), F32)],
        compiler_params=_cparams("arbitrary"),
        name="nsa_sample",
    )(page_table, q8, gates8, kvn16, w, win, *([cmp_pages] * n_pages), *([sel_pages] * n_pages))


def _comb_candidates(v1, v2):
    k, half, lanes = P_TOPK, P_TOPK // 2, v1.shape[1]
    assert k == 16
    qi = lax.broadcasted_iota(jnp.int32, (half, lanes), 0).astype(F32)
    sums = [v1[0:1] + v2]
    idx = [lax.broadcasted_iota(jnp.int32, (k, lanes), 0).astype(F32)]
    for p in range(1, half):
        sums.append(v1[p:p + 1] + v2[0:half])
        idx.append(qi + float(p * k))
    sums.append(v1[half:k] + v2[0:1])
    idx.append((qi + float(half)) * float(k))
    return jnp.concatenate(sums, axis=0), jnp.concatenate(idx, axis=0)


def _peer_select(s1s, s2s):
    k, half = P_TOPK, P_TOPK // 2
    nt = len(s1s)
    poss, vals = _topk_pos_multi_fast(list(s1s) + list(s2s), 0, k)
    combs = [_comb_candidates(vals[t], vals[nt + t]) for t in range(nt)]
    poscs, vcs = _topk_pos_multi_fast([cb[0] for cb in combs], 0, k, combs[0][1])
    out = []
    for t in range(nt):
        pos1, v1, pos2, v2 = poss[t], vals[t], poss[nt + t], vals[nt + t]
        z = jnp.sum(jnp.exp(vcs[t] - vcs[t][0:1]), axis=0, keepdims=True)
        sel = jnp.where(poscs[t] < k, 1.0, 0.0)
        n1 = jnp.where(pos1 == 0.0, jnp.sum(sel[0:k], axis=0, keepdims=True), 0.0)
        for p in range(1, half):
            r0 = k + half * (p - 1)
            n1 = jnp.where(pos1 == float(p), jnp.sum(sel[r0:r0 + half], axis=0, keepdims=True), n1)
        for p in range(half, k):
            r0 = k + half * (half - 1) + (p - half)
            n1 = jnp.where(pos1 == float(p), sel[r0:r0 + 1], n1)
        f1 = jnp.where(pos1 < k, jnp.exp(s1s[t] - v1[0:1]), 0.0) / z
        out.append((n1, f1, pos2, jnp.exp(s2s[t] - v2[0:1])))
    return out


def _peer_body(x_ref, g_ref, gf_ref, wq_ref, keys_ref, u_ref, vt_ref, o_ref,
               hnt_scr, n1_scr, f1_scr, q2_scr, e2_scr, acc_scr, h_even, h_odd, *, ec, final_norm):
    c = pl.program_id(1)
    tm = x_ref.shape[0]

    @pl.when(c == 0)
    def _select():
        hn = _rmsnorm(x_ref[...], g_ref[...])
        hnt_scr[...] = hn.T.astype(BF16)
        acc_scr[...] = jnp.zeros_like(acc_scr)

        def head(h, carry):
            r0 = pl.multiple_of(h * 2 * P_DHALF, 2 * P_DHALF)
            qt = _dot(wq_ref[pl.ds(r0, 2 * P_DHALF), :], hnt_scr[...]).astype(BF16)
            lane_tiles = [slice(lt * 128, (lt + 1) * 128) for lt in range(tm // 128)]
            s1s = [_dot(keys_ref[2 * h], qt[0:P_DHALF, sl]) for sl in lane_tiles]
            s2s = [_dot(keys_ref[2 * h + 1], qt[P_DHALF:2 * P_DHALF, sl]) for sl in lane_tiles]
            for sl, (n1, f1, q2, e2) in zip(lane_tiles, _peer_select(s1s, s2s)):
                n1_scr[h, :, sl] = n1
                f1_scr[h, :, sl] = f1
                q2_scr[h, :, sl] = q2.astype(BF16)
                e2_scr[h, :, sl] = e2.astype(BF16)
            return carry

        lax.fori_loop(0, P_HEADS, head, 0)

    nc = N_EXP // ec
    assert nc % 2 == 0

    def weighted_acts(h_out):
        act = _gelu(_dot(u_ref[...], hnt_scr[...])).astype(BF16)
        for ai in range(ec // N_KEYS):
            a = c * (ec // N_KEYS) + ai
            w = None
            for h in range(P_HEADS):
                n1 = n1_scr[h, pl.ds(a, 1), :].astype(BF16)
                f1 = f1_scr[h, pl.ds(a, 1), :].astype(BF16)
                term = jnp.where(q2_scr[h] < n1, e2_scr[h] * f1, jnp.zeros((), BF16))
                w = term if w is None else w + term
            rows = slice(ai * N_KEYS, (ai + 1) * N_KEYS)
            h_out[rows, :] = w * act[rows]

    @pl.when(c == 0)
    def _first():
        weighted_acts(h_even)

    @pl.when(jnp.logical_and(c < nc, c % 2 == 1))
    def _odd():
        acc_scr[...] += _dot(vt_ref[...], h_even[...])
        weighted_acts(h_odd)

    @pl.when(jnp.logical_and(jnp.logical_and(c > 0, c < nc), c % 2 == 0))
    def _even():
        acc_scr[...] += _dot(vt_ref[...], h_odd[...])
        weighted_acts(h_even)

    @pl.when(c == nc)
    def _finish():
        y = x_ref[...] + (acc_scr[...] + _dot(vt_ref[...], h_odd[...])).T
        if final_norm:
            y = _rmsnorm(y, gf_ref[...])
        o_ref[...] = y


def _peer(x, g, gf, wq_t, keys, u, vt, final_norm, tm=256, ec=1024):
    n, d = x.shape
    nc = N_EXP // ec
    const2 = lambda i, c: (0, 0)
    return pl.pallas_call(
        functools.partial(_peer_body, ec=ec, final_norm=final_norm),
        grid=(n // tm, nc + 1),
        in_specs=[pl.BlockSpec((tm, d), lambda i, c: (i, 0)),
                  pl.BlockSpec((1, d), const2), pl.BlockSpec((1, d), const2),
                  pl.BlockSpec(wq_t.shape, const2),
                  pl.BlockSpec(keys.shape, lambda i, c: (0, 0, 0)),
                  pl.BlockSpec((ec, d), lambda i, c: (jnp.minimum(c, nc - 1), 0)),
                  pl.BlockSpec((d, ec), lambda i, c: (0, jnp.maximum(c - 1, 0)))],
        out_specs=pl.BlockSpec((tm, d), lambda i, c: (i, 0)),
        out_shape=jax.ShapeDtypeStruct((n, d), F32),
        scratch_shapes=[pltpu.VMEM((d, tm), BF16)]
        + [pltpu.VMEM((P_HEADS, N_KEYS, tm), F32)] * 2
        + [pltpu.VMEM((P_HEADS, N_KEYS, tm), BF16)] * 2
        + [pltpu.VMEM((d, tm), F32), pltpu.VMEM((ec, tm), BF16), pltpu.VMEM((ec, tm), BF16)],
        compiler_params=_cparams("parallel", "arbitrary"),
        name="peer",
    )(x, g.reshape(1, d), gf.reshape(1, d), wq_t, keys, u, vt)


def _unit_lower_inverses(ms, order):
    ps = [-m for m in ms]
    invs = list(ps)
    span = 2
    while span < order:
        ps = [_dot_x3(p, p) for p in ps]
        invs = [inv + p + _dot_x3(inv, p) for inv, p in zip(invs, ps)]
        span *= 2
    row = lax.broadcasted_iota(jnp.int32, ms[0].shape, 0)
    col = lax.broadcasted_iota(jnp.int32, ms[0].shape, 1)
    eye = jnp.where(row == col, 1.0, 0.0)
    return [inv + eye for inv in invs]


def _gdn_body(qkv_ref, z_ref, ba_ref, cb_ref, s0_ref, wc_ref, prm_ref, gn_ref, o_ref, s_ref,
              xc_scr, s_scr, *, valid):
    t = pl.program_id(1)
    c = GDN_CHUNK
    r = qkv_ref.shape[1]

    @pl.when(t == 0)
    def _():
        xc_scr[0:8, :] = cb_ref[0]
        s_scr[...] = s0_ref[0]

    xc_scr[8:8 + r, :] = qkv_ref[0]
    if r < c:
        xc_scr[8 + r:8 + c, :] = jnp.zeros((c - r, QKV_C), F32)
    wc = wc_ref[...]
    conv = wc[0:1] * xc_scr[pl.ds(5, c), :]
    for j in range(1, CONV_W):
        conv = conv + wc[j:j + 1] * xc_scr[pl.ds(5 + j, c), :]
    xc_scr[0:8, :] = xc_scr[c:c + 8, :]
    act = _silu(conv)

    row = lax.broadcasted_iota(jnp.int32, (c, 1), 0)
    live = row < valid
    ba = ba_ref[0]
    if r < c:
        ba = jnp.concatenate([ba, jnp.zeros((c - r, 128), F32)], axis=0)
    prm = prm_ref[...]
    beta_all = jnp.where(live, jax.nn.sigmoid(ba), 0.0)
    g_all = jnp.where(live, prm[0:1] * _softplus(ba + prm[1:2]), 0.0)
    gc_all = g_all
    s = 1
    while s < c:
        gc_all = gc_all + _shift_rows(gc_all, s, 0.0)
        s *= 2
    gct = gc_all.T
    ri = lax.broadcasted_iota(jnp.int32, (c, c), 0)
    ci = lax.broadcasted_iota(jnp.int32, (c, c), 1)
    tri = ri >= ci
    strict = ri > ci
    z = z_ref[0]
    if r < c:
        z = jnp.concatenate([z, jnp.zeros((c - r, H_C * DV_C), F32)], axis=0)
    per_head = []
    for h in range(H_C):
        q = act[:, h * DK_C:(h + 1) * DK_C]
        k = act[:, (H_C + h) * DK_C:(H_C + h + 1) * DK_C]
        v = act[:, (2 * H_C + h) * DK_C:(2 * H_C + h + 1) * DK_C]
        q = q * lax.rsqrt(jnp.sum(q * q, -1, keepdims=True) + 1e-6) * (DK_C ** -0.5)
        k = k * lax.rsqrt(jnp.sum(k * k, -1, keepdims=True) + 1e-6)
        beta = beta_all[:, h:h + 1]
        gcol = gc_all[:, H_C + h:H_C + h + 1]
        grow = gct[H_C + h:H_C + h + 1, :]
        glast = gc_all[c - 1:c, H_C + h:H_C + h + 1]
        lmat = jnp.where(tri, jnp.exp(jnp.where(tri, gcol - grow, 0.0)), 0.0)
        kb = k * beta
        kbf = k.astype(BF16)
        m = jnp.where(strict, _dot_nt(kb.astype(BF16), kbf) * lmat, 0.0)
        eg = jnp.exp(gcol)
        aqk = (_dot_nt(q.astype(BF16), kbf) * lmat).astype(BF16)
        per_head.append(dict(m=m, vb=(v * beta).astype(BF16), kbe=(kb * eg).astype(BF16), aqk=aqk,
                             qd=(q * eg).astype(BF16), kdt=(k * jnp.exp(glast - gcol)).T.astype(BF16),
                             decay=jnp.exp(glast)))
    tinvs = _unit_lower_inverses([ph["m"] for ph in per_head], min(valid, c))
    for h, (ph, tinv) in enumerate(zip(per_head, tinvs)):
        tinv = tinv.astype(BF16)
        u = _dot(tinv, ph["vb"])
        w = _dot(tinv, ph["kbe"])
        st = s_scr[h]
        stb = st.astype(BF16)
        v_new = u - _dot(w.astype(BF16), stb)
        o = _dot(ph["qd"], stb) + _dot(ph["aqk"], v_new.astype(BF16))
        s_scr[h] = st * ph["decay"] + _dot(ph["kdt"], v_new.astype(BF16))
        o = o * lax.rsqrt(jnp.mean(o * o, -1, keepdims=True) + EPS) * gn_ref[...]
        o = o * _silu(z[:, h * DV_C:(h + 1) * DV_C])
        o_ref[0, :, h * DV_C:(h + 1) * DV_C] = o[0:r].astype(o_ref.dtype)

    @pl.when(t == pl.num_programs(1) - 1)
    def _():
        s_ref[0] = s_scr[...]


def _gdn(qkv, z, ba, cbuf8, s0, wc, prm, gn, valid):
    b, t, _ = qkv.shape
    r = min(t, GDN_CHUNK)
    tile = lambda wd: pl.BlockSpec((1, r, wd), lambda i, j: (i, j, 0))
    return pl.pallas_call(
        functools.partial(_gdn_body, valid=valid),
        grid=(b, t // r),
        in_specs=[tile(QKV_C), tile(H_C * DV_C), tile(128),
                  pl.BlockSpec((1, 8, QKV_C), lambda i, j: (i, 0, 0)),
                  pl.BlockSpec((1, H_C, DK_C, DV_C), lambda i, j: (i, 0, 0, 0)),
                  pl.BlockSpec((CONV_W, QKV_C), lambda i, j: (0, 0)),
                  pl.BlockSpec((8, 128), lambda i, j: (0, 0)),
                  pl.BlockSpec((1, DV_C), lambda i, j: (0, 0))],
        out_specs=[tile(H_C * DV_C),
                   pl.BlockSpec((1, H_C, DK_C, DV_C), lambda i, j: (i, 0, 0, 0))],
        out_shape=[jax.ShapeDtypeStruct((b, t, H_C * DV_C), F32 if r < GDN_CHUNK else BF16),
                   jax.ShapeDtypeStruct((b, H_C, DK_C, DV_C), F32)],
        scratch_shapes=[pltpu.VMEM((GDN_CHUNK + 8, QKV_C), F32), pltpu.VMEM((H_C, DK_C, DV_C), F32)],
        compiler_params=_cparams("parallel", "arbitrary"),
        name="gdn",
    )(qkv, z, ba, cbuf8, s0, wc, prm, gn)


def _block_diag(w):
    n, d, e = w.shape
    eye = jnp.eye(n, dtype=w.dtype)
    return (w[:, :, None, :] * eye[:, None, :, None]).reshape(n * d, n * e)


def _pad_rows(x, rows, front=0):
    return jnp.pad(x, ((0, 0), (front, rows - x.shape[1] - front), (0, 0)))


def _layer_a_weights(w_in, w_pool):
    d = w_in.shape[0]
    q_a, kv_a = H_A * DH_A, 3 * 2 * G_A * DH_A
    wq = w_in[:, :q_a].reshape(d, G_A, R_A, DH_A)
    zeros = jnp.zeros_like(wq)
    wq = jnp.stack([jnp.concatenate([wq[:, 0], zeros[:, 0]], -1),
                    jnp.concatenate([zeros[:, 1], wq[:, 1]], -1)], axis=1).reshape(d, H_A * 128)
    i1 = q_a + kv_a
    i2 = i1 + 3 * H_A
    wg = jnp.pad(w_in[:, i1:i2], ((0, 0), (0, 128 - 3 * H_A)))
    w = jnp.concatenate([wq, w_in[:, q_a:i1], wg, w_in[:, i2:]], axis=1).astype(BF16)
    wp = jnp.concatenate([jnp.broadcast_to(w_pool[0][:, None], (CMP_BLOCK, 128)),
                          jnp.broadcast_to(w_pool[1][:, None], (CMP_BLOCK, 128))], axis=1)
    return w, wp


def kernel(x_prompt, x_sample, cache_cmp_kv, cache_sel_kv, cache_win_kv, state_rg_h, state_rg_conv, state_gdn_S, state_gdn_conv, page_table, norm_mix, norm_ffn, norm_final, w_in_a, w_cmp_pool, w_rg_conv, b_rg_conv, w_rg_a, b_rg_a, w_rg_x, b_rg_x, rg_lambda, w_out_a, w_in_c, w_c_conv, c_A_log, c_dt_bias, c_norm, w_out_c, peer_w_q, peer_keys, peer_u, peer_v):
    bp, tp, d = x_prompt.shape
    bs, ts, _ = x_sample.shape
    np_, ns = bp * tp, bs * ts
    xp = x_prompt.reshape(np_, d)
    xs = x_sample.reshape(ns, d)
    kv_w = 3 * 2 * G_A * DH_A

    def peer_layer(x, layer, final):
        wq_t = peer_w_q[layer].astype(BF16).T
        keys = peer_keys[layer].reshape(2 * P_HEADS, N_KEYS, P_DHALF).astype(BF16)
        u = peer_u[layer].astype(BF16)
        vt = peer_v[layer].astype(BF16).T
        return [_peer(xx, norm_ffn[layer], norm_final, wq_t, keys, u, vt, final) for xx in x]

    w0, wp = _layer_a_weights(w_in_a[0], w_cmp_pool[0])
    qo, ko, go, ro = 0, H_A * 128, H_A * 128 + kv_w, H_A * 128 + kv_w + 128
    outs = ((qo, H_A * 128, None), (ko, kv_w, None), (ko + 256, 512, None), (go, 128, "sigmoid"),
            (ro, D_RNN, None), (ro + D_RNN, D_RNN, None))
    dts = (BF16, F32, BF16, F32, F32, F32)
    qp, kvp, kvbp, gtp, xrp, ygp = _norm_proj(xp, norm_mix[0], w0, outs, dts, 512)
    qs, kvs, _, gts, xrs, ygs = _norm_proj(xs, norm_mix[0], w0, outs, dts, min(512, ns))

    kvp3 = kvp.reshape(bp, tp, kv_w)
    pe, po = _nsa_pool(kvp3, wp)
    att_p = _nsa_prompt(qp.reshape(bp, tp, H_A * 128), gtp.reshape(bp, tp, 128), pe, po,
                        kvbp.reshape(bp, tp, 512)).reshape(np_, H_A * DH_A)

    kvs3 = kvs.reshape(bs, ts, kv_w)
    n_pool, page = cache_cmp_kv.shape[1], cache_cmp_kv.shape[2]
    att_s8, win_new = _nsa_sample(
        page_table,
        _pad_rows(qs.astype(F32).reshape(bs, ts, H_A * 128), 8),
        _pad_rows(gts.reshape(bs, ts, 128), 8),
        _pad_rows(kvs3, 16), wp,
        cache_win_kv[0].reshape(bs, -1, 256),
        cache_cmp_kv[0].reshape(n_pool, page, 256),
        cache_sel_kv[0].reshape(n_pool, page, 256), ts)
    att_s = att_s8[:, :ts].reshape(ns, H_A * DH_A).astype(BF16)

    rg = (w_rg_conv[0], b_rg_conv[0].reshape(1, D_RNN), _block_diag(w_rg_a[0]).astype(BF16),
          b_rg_a[0].reshape(1, D_RNN), _block_diag(w_rg_x[0]).astype(BF16), b_rg_x[0].reshape(1, D_RNN),
          rg_lambda[0].reshape(1, D_RNN))
    xrp3 = xrp.reshape(bp, tp, D_RNN)
    rec_p, hl_p = _rglru_prompt(xrp3, ygp.reshape(bp, tp, D_RNN), *rg)
    xrs3 = xrs.reshape(bs, ts, D_RNN)
    rec_s, hl_s = _rglru_sample(xrs3.transpose(1, 0, 2), ygs.reshape(bs, ts, D_RNN).transpose(1, 0, 2),
                                state_rg_conv[0].transpose(1, 0, 2), state_rg_h[0], *rg)
    rec_s = rec_s.transpose(1, 0, 2).reshape(ns, D_RNN).astype(BF16)

    wo = w_out_a[0].astype(BF16)
    wo_att, wo_rec = wo[:H_A * DH_A], wo[H_A * DH_A:]
    xp = _out_proj([att_p, rec_p.reshape(np_, D_RNN)], [wo_att, wo_rec], xp, 512)
    xs = _out_proj([att_s, rec_s], [wo_att, wo_rec], xs, min(512, ns))
    xp, xs = peer_layer([xp, xs], 0, False)

    d_c = H_C * DV_C
    w1 = jnp.pad(w_in_c[0], ((0, 0), (0, 128 - 2 * H_C))).astype(BF16)
    outs1 = ((0, QKV_C, None), (QKV_C, d_c, None), (QKV_C + d_c, 128, None))
    dts1 = (F32, F32, F32)
    qkv_p, z_p, ba_p = _norm_proj(xp, norm_mix[1], w1, outs1, dts1, 512)
    qkv_s, z_s, ba_s = _norm_proj(xs, norm_mix[1], w1, outs1, dts1, min(512, ns))
    lane = jnp.arange(128)
    hsel = jnp.clip(lane - H_C, 0, H_C - 1)
    in_g = (lane >= H_C) & (lane < 2 * H_C)
    prm = jnp.zeros((8, 128), F32)
    prm = prm.at[0].set(jnp.where(in_g, -jnp.exp(c_A_log[0])[hsel], 0.0))
    prm = prm.at[1].set(jnp.where(in_g, c_dt_bias[0][hsel], 0.0))
    gn = c_norm[0].reshape(1, DV_C)
    qkv_p3 = qkv_p.reshape(bp, tp, QKV_C)
    o_p, gs_p = _gdn(qkv_p3, z_p.reshape(bp, tp, d_c), ba_p.reshape(bp, tp, 128),
                     jnp.zeros((bp, 8, QKV_C), F32), jnp.zeros((bp, H_C, DK_C, DV_C), F32),
                     w_c_conv[0], prm, gn, GDN_CHUNK)
    qkv_s3 = qkv_s.reshape(bs, ts, QKV_C)
    o_s, gs_s = _gdn(_pad_rows(qkv_s3, 8), _pad_rows(z_s.reshape(bs, ts, d_c), 8),
                     _pad_rows(ba_s.reshape(bs, ts, 128), 8),
                     _pad_rows(state_gdn_conv[0], 8, front=8 - (CONV_W - 1)), state_gdn_S[0],
                     w_c_conv[0], prm, gn, ts)
    wo_c = w_out_c[0].astype(BF16)
    xp = _out_proj([o_p.reshape(np_, d_c)], [wo_c], xp, 512)
    xs = _out_proj([o_s[:, :ts].reshape(ns, d_c).astype(BF16)], [wo_c], xs, min(512, ns))
    yp, ys = peer_layer([xp, xs], 1, True)

    kvp6 = kvp3.reshape(bp, tp, 3, 2, G_A, DH_A)
    kvs6 = kvs3.reshape(bs, ts, 3, 2, G_A, DH_A)
    wlen = min(WINDOW, tp)
    cw = CONV_W - 1
    return (yp.reshape(bp, tp, d), ys.reshape(bs, ts, d),
            kvp6[None, :, :, 0], kvs6[None, :, :, 0], kvp6[None, :, :, 1], kvs6[None, :, :, 1],
            kvp6[None, :, tp - wlen:, 2], win_new.reshape(bs, -1, 2, G_A, DH_A)[None],
            hl_p[None], hl_s[None], xrp3[None, :, tp - cw:], xrs3[None, :, ts - cw:],
            gs_p[None], gs_s[None], qkv_p3[None, :, tp - cw:], qkv_s3[None, :, ts - cw:])
```

```python
import functools
import math

import jax
import jax.numpy as jnp
from jax import lax
from jax.experimental import pallas as pl
from jax.experimental.pallas import tpu as pltpu

F32 = jnp.float32
BF16 = jnp.bfloat16

D_MODEL = 1024
H_A, G_A, R_A, DH_A = 8, 2, 4, 64
CMP_BLOCK, SEL_BLOCK, TOP_N, WINDOW, Q_BLOCK = 32, 64, 16, 512, 128
FORCE_BONUS = 1.0e3
NEG_INF = -1.0e30
N_SEL_LANES = 128
D_RNN, RG_C, CONV_W = 512, 8.0, 4
H_C, DK_C, DV_C = 8, 128, 128
QKV_C = 3 * H_C * DK_C
GDN_CHUNK = 128
P_HEADS, N_KEYS, P_DHALF, P_TOPK = 8, 128, 128, 16
N_EXP = N_KEYS * N_KEYS
EPS = 1e-6

VMEM_LIMIT = 56 * 1024 * 1024


def _cparams(*sem):
    return pltpu.CompilerParams(dimension_semantics=sem, vmem_limit_bytes=VMEM_LIMIT)


def _rmsnorm(x, g):
    return x * lax.rsqrt(jnp.mean(x * x, -1, keepdims=True) + EPS) * g


def _gelu(x):
    k = -2.0 * math.sqrt(2.0 / math.pi) * math.log2(math.e)
    return x / (1.0 + jnp.exp2(x * (k + (k * 0.044715) * (x * x))))


def _softplus(x):
    return jnp.maximum(x, 0.0) + jnp.log1p(jnp.exp(-jnp.abs(x)))


def _silu(x):
    return x * jax.nn.sigmoid(x)


def _neg_expm1(x):
    t = jnp.tanh(0.5 * x)
    return -2.0 * t / (1.0 - t)


def _dot(a, b):
    return jnp.dot(a, b, preferred_element_type=F32)


def _dot_nt(a, b):
    return lax.dot_general(a, b, (((1,), (1,)), ((), ())), preferred_element_type=F32)


def _dot_x3(a, b):
    a_hi = a.astype(BF16)
    b_hi = b.astype(BF16)
    a_lo = (a - a_hi.astype(F32)).astype(BF16)
    b_lo = (b - b_hi.astype(F32)).astype(BF16)
    return _dot(a_hi, b_hi) + (_dot(a_hi, b_lo) + _dot(a_lo, b_hi))


def _shift_rows(x, s, fill):
    rolled = pltpu.roll(x, s, 0)
    row = lax.broadcasted_iota(jnp.int32, x.shape, 0)
    return jnp.where(row >= s, rolled, fill)


def _topk_pos_multi(ss, axis, k, idx=None):
    shape = ss[0].shape
    if idx is None:
        idx = lax.broadcasted_iota(jnp.int32, shape, axis).astype(F32)
    vshape = tuple(k if d == axis else shape[d] for d in range(len(shape)))
    vidx = lax.broadcasted_iota(jnp.int32, vshape, axis)
    n = len(ss)
    ss = list(ss)
    poss = [jnp.full(shape, float(k), F32) for _ in range(n)]
    valss = [jnp.zeros(vshape, F32) for _ in range(n)]
    for i in range(k):
        ms = [jnp.max(s, axis=axis, keepdims=True) for s in ss]
        js = [jnp.min(jnp.where(s == m, idx, 1e9), axis=axis, keepdims=True) for s, m in zip(ss, ms)]
        ohs = [idx == j for j in js]
        poss = [jnp.where(oh, float(i), pos) for oh, pos in zip(ohs, poss)]
        ss = [jnp.where(oh, -jnp.inf, s) for oh, s in zip(ohs, ss)]
        valss = [jnp.where(vidx == i, m, vals) for m, vals in zip(ms, valss)]
    return poss, valss


def _topk_pos(s, axis, k, idx=None):
    poss, valss = _topk_pos_multi([s], axis, k, idx)
    return poss[0], valss[0]


def _topk_pos_multi_fast(ss, axis, k, idx=None):
    shape = ss[0].shape
    vshape = tuple(k if d == axis else shape[d] for d in range(len(shape)))
    vidx = lax.broadcasted_iota(jnp.int32, vshape, axis)
    n = len(ss)
    rest = list(ss)
    poss = [jnp.full(shape, float(k), F32) for _ in range(n)]
    valss = [jnp.zeros(vshape, F32) for _ in range(n)]
    for i in range(k):
        ms = [jnp.max(s, axis=axis, keepdims=True) for s in rest]
        ohs = [s == m for s, m in zip(rest, ms)]
        poss = [jnp.where(oh, float(i), pos) for oh, pos in zip(ohs, poss)]
        rest = [jnp.where(oh, -jnp.inf, s) for oh, s in zip(ohs, rest)]
        valss = [jnp.where(vidx == i, m, vals) for m, vals in zip(ms, valss)]
    removed = sum(jnp.sum(jnp.where(pos < k, 1.0, 0.0)) for pos in poss)
    expected = float(n * k * math.prod(shape) // shape[axis])
    return lax.cond(removed == expected,
                    lambda: (tuple(poss), tuple(valss)),
                    lambda: tuple(tuple(x) for x in _topk_pos_multi(ss, axis, k, idx)))


def _norm_proj_body(x_ref, g_ref, w_ref, *out_refs, outs):
    h = _rmsnorm(x_ref[...], g_ref[...]).astype(BF16)
    for o_ref, (off, width, act) in zip(out_refs, outs):
        z = _dot(h, w_ref[:, off:off + width])
        if act == "sigmoid":
            z = jax.nn.sigmoid(z)
        o_ref[...] = z.astype(o_ref.dtype)


def _norm_proj(x, g, w, outs, dtypes, tm):
    n, d = x.shape
    body = functools.partial(_norm_proj_body, outs=outs)
    return pl.pallas_call(
        body,
        grid=(n // tm,),
        in_specs=[pl.BlockSpec((tm, d), lambda i: (i, 0)),
                  pl.BlockSpec((1, d), lambda i: (0, 0)),
                  pl.BlockSpec(w.shape, lambda i: (0, 0))],
        out_specs=[pl.BlockSpec((tm, wd), lambda i: (i, 0)) for (_, wd, _) in outs],
        out_shape=[jax.ShapeDtypeStruct((n, wd), dt) for (_, wd, _), dt in zip(outs, dtypes)],
        compiler_params=_cparams("parallel"),
        name="norm_proj",
    )(x, g.reshape(1, d), w)


def _out_proj_body(*refs, n_in):
    a_refs, w_refs = refs[:n_in], refs[n_in:2 * n_in]
    x_ref, o_ref = refs[2 * n_in], refs[2 * n_in + 1]
    acc = x_ref[...]
    for a, w in zip(a_refs, w_refs):
        acc = acc + _dot(a[...], w[...])
    o_ref[...] = acc


def _out_proj(acts, ws, x, tm):
    n, d = x.shape
    n_in = len(acts)
    return pl.pallas_call(
        functools.partial(_out_proj_body, n_in=n_in),
        grid=(n // tm,),
        in_specs=([pl.BlockSpec((tm, a.shape[1]), lambda i: (i, 0)) for a in acts]
                  + [pl.BlockSpec(w.shape, lambda i: (0, 0)) for w in ws]
                  + [pl.BlockSpec((tm, d), lambda i: (i, 0))]),
        out_specs=pl.BlockSpec((tm, d), lambda i: (i, 0)),
        out_shape=jax.ShapeDtypeStruct((n, d), F32),
        compiler_params=_cparams("parallel"),
        name="out_proj",
    )(*acts, *ws, x)


def _rg_gates(xf, wa, ba, wx, bx, lam):
    xb = xf.astype(BF16)
    r = jax.nn.sigmoid(_dot(xb, wa) + ba)
    i = jax.nn.sigmoid(_dot(xb, wx) + bx)
    log_a = -RG_C * r * _softplus(-lam)
    a = jnp.exp(log_a)
    b = jnp.sqrt(_neg_expm1(2.0 * log_a)) * (i * xf)
    return a, b


def _rglru_prompt_body(xr_ref, yg_ref, wc_ref, bc_ref, wa_ref, ba_ref, wx_ref, bx_ref, lam_ref,
                       rec_ref, hl_ref, xc_scr, h_scr, *, rt):
    t = pl.program_id(1)

    @pl.when(t == 0)
    def _():
        xc_scr[0:8, :] = jnp.zeros((8, D_RNN), F32)
        h_scr[...] = jnp.zeros_like(h_scr)

    xc_scr[8:8 + rt, :] = xr_ref[0]
    wc = wc_ref[...]
    conv = wc[0:1] * xc_scr[pl.ds(5, rt), :]
    for j in range(1, CONV_W):
        conv = conv + wc[j:j + 1] * xc_scr[pl.ds(5 + j, rt), :]
    xc_scr[0:8, :] = xc_scr[rt:rt + 8, :]
    xf = conv + bc_ref[...]
    a, b = _rg_gates(xf, wa_ref[...], ba_ref[...], wx_ref[...], bx_ref[...], lam_ref[...])
    s = 1
    while s < rt:
        b = a * _shift_rows(b, s, 0.0) + b
        a = a * _shift_rows(a, s, 1.0)
        s *= 2
    h = a * h_scr[0:1, :] + b
    h_last = h[rt - 1:rt, :]
    h_scr[0:1, :] = h_last
    hl_ref[0] = h_last
    rec_ref[0] = (h * _gelu(yg_ref[0])).astype(rec_ref.dtype)


def _rglru_prompt(xr, yg, wc, bc, wa, ba, wx, bx, lam, rt=256):
    b, t, d = xr.shape
    vec = lambda: pl.BlockSpec((1, d), lambda i, j: (0, 0))
    mat = lambda: pl.BlockSpec((d, d), lambda i, j: (0, 0))
    rec, hl = pl.pallas_call(
        functools.partial(_rglru_prompt_body, rt=rt),
        grid=(b, t // rt),
        in_specs=[pl.BlockSpec((1, rt, d), lambda i, j: (i, j, 0)),
                  pl.BlockSpec((1, rt, d), lambda i, j: (i, j, 0)),
                  pl.BlockSpec((CONV_W, d), lambda i, j: (0, 0)),
                  vec(), mat(), vec(), mat(), vec(), vec()],
        out_specs=[pl.BlockSpec((1, rt, d), lambda i, j: (i, j, 0)),
                   pl.BlockSpec((1, 1, d), lambda i, j: (i, 0, 0))],
        out_shape=[jax.ShapeDtypeStruct((b, t, d), BF16),
                   jax.ShapeDtypeStruct((b, 1, d), F32)],
        scratch_shapes=[pltpu.VMEM((rt + 8, d), F32), pltpu.VMEM((8, d), F32)],
        compiler_params=_cparams("arbitrary", "arbitrary"),
        name="rglru_prompt",
    )(xr, yg, wc, bc, wa, ba, wx, bx, lam)
    return rec, hl[:, 0]


def _rglru_sample_body(xr_ref, yg_ref, buf_ref, h0_ref, wc_ref, bc_ref, wa_ref, ba_ref, wx_ref,
                       bx_ref, lam_ref, rec_ref, hl_ref, *, t_len):
    xs = [buf_ref[j] for j in range(CONV_W - 1)] + [xr_ref[j] for j in range(t_len)]
    wc = wc_ref[...]
    h = h0_ref[...]
    for t in range(t_len):
        conv = wc[0:1] * xs[t]
        for j in range(1, CONV_W):
            conv = conv + wc[j:j + 1] * xs[t + j]
        xf = conv + bc_ref[...]
        a, b = _rg_gates(xf, wa_ref[...], ba_ref[...], wx_ref[...], bx_ref[...], lam_ref[...])
        h = a * h + b
        rec_ref[t] = h * _gelu(yg_ref[t])
    hl_ref[...] = h


def _rglru_sample(xr, yg, buf, h0, wc, bc, wa, ba, wx, bx, lam):
    t_len, b, d = xr.shape
    return pl.pallas_call(
        functools.partial(_rglru_sample_body, t_len=t_len),
        out_shape=[jax.ShapeDtypeStruct((t_len, b, d), F32),
                   jax.ShapeDtypeStruct((b, d), F32)],
        compiler_params=pltpu.CompilerParams(vmem_limit_bytes=VMEM_LIMIT),
        name="rglru_sample",
    )(xr, yg, buf, h0, wc, bc, wa, ba, wx, bx, lam)


def _pool_blocks(x3, w):
    first = jnp.sum(x3[:, :CMP_BLOCK, :] * w[None], axis=1)
    second = jnp.sum(x3[:, CMP_BLOCK:, :] * w[None], axis=1)
    return first, second


def _nsa_pool_body(x_ref, w_ref, pe_ref, po_ref):
    x = x_ref[0]
    nb = x.shape[0] // SEL_BLOCK
    pe, po = _pool_blocks(x.reshape(nb, SEL_BLOCK, x.shape[1]), w_ref[...])
    pe_ref[0] = pe
    po_ref[0] = po


def _nsa_pool(kv, w, rows=1024):
    b, t, _ = kv.shape
    nb = rows // SEL_BLOCK
    return pl.pallas_call(
        _nsa_pool_body,
        grid=(b, t // rows),
        in_specs=[pl.BlockSpec((1, rows, 256), lambda i, j: (i, j, 0)),
                  pl.BlockSpec((CMP_BLOCK, 256), lambda i, j: (0, 0))],
        out_specs=[pl.BlockSpec((1, nb, 256), lambda i, j: (i, j, 0))] * 2,
        out_shape=[jax.ShapeDtypeStruct((b, t // SEL_BLOCK, 256), F32)] * 2,
        compiler_params=_cparams("parallel", "parallel"),
        name="nsa_pool",
    )(kv, w)


def _cmp_branch(q, qpos, pe, po, nq):
    jl = lax.broadcasted_iota(jnp.int32, (1, N_SEL_LANES), 1)
    vis_e = (SEL_BLOCK * jl + (CMP_BLOCK - 1)) <= qpos
    vis_o = (SEL_BLOCK * jl + (SEL_BLOCK - 1)) <= qpos
    s_e = jnp.where(vis_e, _dot_nt(q, pe[:, :128].astype(BF16)), NEG_INF)
    s_o = jnp.where(vis_o, _dot_nt(q, po[:, :128].astype(BF16)), NEG_INF)
    m = jnp.maximum(jnp.max(s_e, -1, keepdims=True), jnp.max(s_o, -1, keepdims=True))
    p_e = jnp.where(vis_e, jnp.exp(s_e - m), 0.0)
    p_o = jnp.where(vis_o, jnp.exp(s_o - m), 0.0)
    den = jnp.sum(p_e, -1, keepdims=True) + jnp.sum(p_o, -1, keepdims=True)
    inv = jnp.where(den > 0.0, 1.0 / jnp.maximum(den, 1e-30), 0.0)
    p_e = p_e * inv
    p_o = p_o * inv
    o = _dot(p_e.astype(BF16), pe[:, 128:].astype(BF16)) + _dot(p_o.astype(BF16), po[:, 128:].astype(BF16))
    p = p_e + p_o
    imp = p[0:nq]
    for r in range(1, R_A):
        imp = imp + p[r * nq:(r + 1) * nq]
    return o, imp


def _select_blocks(imp, qpos, blk_axis):
    shape = (N_SEL_LANES, 1) if blk_axis == 0 else (1, N_SEL_LANES)
    blk = lax.broadcasted_iota(jnp.int32, shape, blk_axis)
    valid = blk * SEL_BLOCK <= qpos
    forced = jnp.logical_or(blk == 0, blk == (qpos >> 6))
    score = jnp.where(valid, imp + jnp.where(forced, FORCE_BONUS, 0.0), -jnp.inf)
    pos, _ = _topk_pos(score, blk_axis, TOP_N)
    return jnp.where(valid, jnp.where(pos < TOP_N, 1.0, 0.0), 0.0)


def _expand_sel(sel, first_blk, nkeys):
    bi = lax.broadcasted_iota(jnp.int32, (N_SEL_LANES, nkeys), 0)
    ki = lax.broadcasted_iota(jnp.int32, (N_SEL_LANES, nkeys), 1)
    e = jnp.where(bi - first_blk == (ki >> 6), 1.0, 0.0).astype(BF16)
    x = _dot(sel.astype(BF16), e)
    return jnp.concatenate([x] * R_A, axis=0)


def _masked_attend(q, k, v, ok):
    s = jnp.where(ok, _dot_nt(q, k), NEG_INF)
    m = jnp.max(s, -1, keepdims=True)
    p = jnp.where(ok, jnp.exp(s - m), 0.0)
    l = jnp.sum(p, -1, keepdims=True)
    return _dot(p.astype(BF16), v) / l


def _window_ok(qpos, kpos):
    d = qpos - kpos
    return jnp.logical_and(d >= 0, d <= WINDOW)


def _gate_and_place(gt, branches):
    heads = []
    for h in range(H_A):
        o = None
        for br in range(3):
            term = gt[:, 3 * h + br:3 * h + br + 1] * branches[br][h]
            o = term if o is None else o + term
        if (h % 2) != h // R_A:
            o = pltpu.roll(o, DH_A, 1)
        heads.append(o)
    lane = lax.broadcasted_iota(jnp.int32, (1, 128), 1)
    return [jnp.where(lane < DH_A, heads[2 * j], heads[2 * j + 1]) for j in range(H_A // 2)]


def _split_heads(o, nq):
    return [o[r * nq:(r + 1) * nq] for r in range(R_A)]


SEL_CHUNK = 512


def _nsa_prompt_body(q_ref, gt_ref, pe_ref, po_ref, kv_ref, oh_ref, vt_ref, o_ref, acc_scr):
    i = pl.program_id(1)
    nq = Q_BLOCK
    base = i * nq
    pe, po = pe_ref[0], po_ref[0]
    rowi = lax.broadcasted_iota(jnp.int32, (R_A * nq, 1), 0)
    qpos = base + (rowi & (nq - 1))
    qpos_row = base + lax.broadcasted_iota(jnp.int32, (1, nq), 1)
    heads_c, heads_w, qats = [], [], []
    for g in range(G_A):
        q = jnp.concatenate([q_ref[0, :, (g * R_A + r) * 128:(g * R_A + r + 1) * 128] for r in range(R_A)], axis=0)
        q = q * jnp.asarray(DH_A ** -0.5, BF16)
        o_c, imp = _cmp_branch(q, qpos, pe, po, nq)
        heads_c += _split_heads(o_c, nq)
        sel_t = _select_blocks(imp.T, qpos_row, 0)
        selterm_t = jnp.where(sel_t > 0.5, 0.0, NEG_INF).astype(BF16)
        qh_t = [jnp.concatenate([qh.astype(F32).T.astype(BF16), selterm_t], axis=0) for qh in _split_heads(q, nq)]
        qats += [jnp.concatenate(qh_t[2 * j:2 * j + 2], axis=1) for j in range(R_A // 2)]

        wlen = WINDOW + nq
        ws = pl.multiple_of(jnp.maximum(i - WINDOW // nq, 0) * nq, nq)
        kw = kv_ref[0, pl.ds(ws, wlen), 256:384]
        vw = kv_ref[0, pl.ds(ws, wlen), 384:512]
        kpos = ws + lax.broadcasted_iota(jnp.int32, (1, wlen), 1)
        heads_w += _split_heads(_masked_attend(q, kw, vw, _window_ok(qpos, kpos)), nq)

    acc_scr[...] = jnp.zeros_like(acc_scr)

    def step(k0, ms, ls, diagonal):
        ka = jnp.concatenate([kv_ref[0, pl.ds(k0, SEL_CHUNK), 0:128], oh_ref[pl.ds(k0, SEL_CHUNK), :]], axis=1)
        vt = vt_ref[0, :, pl.ds(k0, SEL_CHUNK)]
        if diagonal:
            qpos_pair = base + (lax.broadcasted_iota(jnp.int32, (1, 2 * nq), 1) & (nq - 1))
            visible = k0 + lax.broadcasted_iota(jnp.int32, (SEL_CHUNK, 1), 0) <= qpos_pair
        ss = [_dot(ka, qat) for qat in qats]
        if diagonal:
            ss = [jnp.where(visible, s, NEG_INF) for s in ss]
        ms_new = [jnp.maximum(m, jnp.max(s, axis=0, keepdims=True)) for m, s in zip(ms, ss)]
        ps = [jnp.exp(s - m) for s, m in zip(ss, ms_new)]
        pvs = [_dot(vt, p.astype(BF16)) for p in ps]
        ls_new = []
        for j in range(n_pairs):
            alpha = jnp.exp(ms[j] - ms_new[j])
            ls_new.append(alpha * ls[j] + jnp.sum(ps[j], axis=0, keepdims=True))
            acc_scr[j] = alpha * acc_scr[j] + pvs[j]
        return tuple(ms_new), tuple(ls_new)

    def past_chunk(c, carry):
        return step(pl.multiple_of(c * SEL_CHUNK, SEL_CHUNK), carry[0], carry[1], False)

    n_pairs = H_A // 2
    init = (tuple(jnp.full((1, 2 * nq), NEG_INF, F32) for _ in range(n_pairs)),
            tuple(jnp.zeros((1, 2 * nq), F32) for _ in range(n_pairs)))
    n_past = (base + nq - 1) // SEL_CHUNK
    ms, ls = lax.fori_loop(0, n_past, past_chunk, init)
    _, ls = step(pl.multiple_of(n_past * SEL_CHUNK, SEL_CHUNK), ms, ls, True)
    heads_s = []
    for j in range(n_pairs):
        o_pair = acc_scr[j] / ls[j]
        heads_s += [o_pair[:, 0:nq].T, o_pair[:, nq:2 * nq].T]
    for j, tile in enumerate(_gate_and_place(gt_ref[0], (heads_c, heads_s, heads_w))):
        o_ref[0, :, j * 128:(j + 1) * 128] = tile.astype(o_ref.dtype)


def _nsa_prompt(qpad, gates, pe, po, kvb):
    b, t, _ = qpad.shape
    nq = Q_BLOCK
    blk_of_key = jnp.arange(t, dtype=jnp.int32)[:, None] // SEL_BLOCK
    onehot = (blk_of_key == jnp.arange(N_SEL_LANES, dtype=jnp.int32)[None, :]).astype(BF16)
    v_sel_t = kvb[:, :, 128:256].transpose(0, 2, 1)
    return pl.pallas_call(
        _nsa_prompt_body,
        grid=(b, t // nq),
        in_specs=[pl.BlockSpec((1, nq, H_A * 128), lambda i, j: (i, j, 0)),
                  pl.BlockSpec((1, nq, 128), lambda i, j: (i, j, 0)),
                  pl.BlockSpec((1, N_SEL_LANES, 256), lambda i, j: (i, 0, 0)),
                  pl.BlockSpec((1, N_SEL_LANES, 256), lambda i, j: (i, 0, 0)),
                  pl.BlockSpec((1, t, 512), lambda i, j: (i, 0, 0)),
                  pl.BlockSpec((t, N_SEL_LANES), lambda i, j: (0, 0)),
                  pl.BlockSpec((1, 128, t), lambda i, j: (i, 0, 0))],
        out_specs=pl.BlockSpec((1, nq, H_A * DH_A), lambda i, j: (i, j, 0)),
        out_shape=jax.ShapeDtypeStruct((b, t, H_A * DH_A), BF16),
        scratch_shapes=[pltpu.VMEM((H_A // 2, 128, 2 * nq), F32)],
        compiler_params=_cparams("parallel", "arbitrary"),
        name="nsa_prompt",
    )(qpad, gates, pe, po, kvb, onehot, v_sel_t)


def _nsa_sample_body(pt_ref, q_ref, gt_ref, kvn_ref, w_ref, win_ref, *rest, n_pages, page, past, tq):
    del pt_ref
    cmp_pages, sel_pages = rest[:n_pages], rest[n_pages:2 * n_pages]
    o_ref, wout_ref, cmp_scr, sel_scr, win_scr = rest[2 * n_pages:]
    nq = q_ref.shape[1]
    npad = kvn_ref.shape[1]
    lp = cmp_scr.shape[0]
    wbuf = win_ref.shape[1]
    kvn = kvn_ref[0]
    for p in range(n_pages):
        cmp_scr[p * page:(p + 1) * page, :] = cmp_pages[p][0]
        sel_scr[p * page:(p + 1) * page, :] = sel_pages[p][0].astype(BF16)
    cmp_scr[past:past + npad, :] = kvn[:, 0:256]
    sel_scr[past:past + npad, :] = kvn[:, 256:512].astype(BF16)
    cmp_scr[past + npad:lp, :] = jnp.zeros((lp - past - npad, 256), F32)
    sel_scr[past + npad:lp, :] = jnp.zeros((lp - past - npad, 256), BF16)
    win_scr[0:wbuf, :] = win_ref[0]
    win_scr[wbuf:wbuf + npad, :] = kvn[:, 512:768]
    wl = win_scr.shape[0]
    win_scr[wbuf + npad:wl, :] = jnp.zeros((wl - wbuf - npad, 256), F32)
    wout_ref[0] = win_scr[pl.ds(tq, wbuf), :]

    nb = lp // SEL_BLOCK
    pe, po = _pool_blocks(cmp_scr[...].reshape(nb, SEL_BLOCK, 256), w_ref[...])
    zpad = jnp.zeros((N_SEL_LANES - nb, 256), F32)
    pe = jnp.concatenate([pe, zpad], axis=0)
    po = jnp.concatenate([po, zpad], axis=0)

    rowi = lax.broadcasted_iota(jnp.int32, (R_A * nq, 1), 0)
    qpos = past + (rowi & (nq - 1))
    nsel_keys = past + 256
    kpos_s = lax.broadcasted_iota(jnp.int32, (1, nsel_keys), 1)
    kpos_w = (past - wbuf) + lax.broadcasted_iota(jnp.int32, (1, wl), 1)
    heads_c, heads_s, heads_w = [], [], []
    qs, imps = [], []
    for g in range(G_A):
        q = jnp.concatenate([q_ref[0, :, (g * R_A + r) * 128:(g * R_A + r + 1) * 128] for r in range(R_A)], axis=0)
        q = (q * DH_A ** -0.5).astype(BF16)
        o_c, imp = _cmp_branch(q, qpos, pe, po, nq)
        heads_c += _split_heads(o_c, nq)
        qs.append(q)
        imps.append(imp)
    sel_all = _select_blocks(jnp.concatenate(imps, axis=0), qpos[0:G_A * nq], 1)
    for g in range(G_A):
        q, sel = qs[g], sel_all[g * nq:(g + 1) * nq]
        ok = jnp.logical_and(_expand_sel(sel, 0, nsel_keys) > 0.5, kpos_s <= qpos)
        o_s = _masked_attend(q, sel_scr[0:nsel_keys, 0:128], sel_scr[0:nsel_keys, 128:256], ok)
        kw = win_scr[:, 0:128].astype(BF16)
        vw = win_scr[:, 128:256].astype(BF16)
        o_w = _masked_attend(q, kw, vw, _window_ok(qpos, kpos_w))
        heads_s += _split_heads(o_s, nq)
        heads_w += _split_heads(o_w, nq)
    for j, tile in enumerate(_gate_and_place(gt_ref[0], (heads_c, heads_s, heads_w))):
        o_ref[0, :, j * 128:(j + 1) * 128] = tile


def _nsa_sample(page_table, q8, gates8, kvn16, w, win, cmp_pages, sel_pages, tq):
    b, n_pages = page_table.shape
    page = cmp_pages.shape[1]
    past = n_pages * page
    wbuf = win.shape[1]
    lp = past + 512
    nq = q8.shape[1]

    def row(shape):
        return pl.BlockSpec((1,) + shape, lambda i, pt: (i, 0, 0))

    def page_spec(p):
        return pl.BlockSpec((1, page, 256), lambda i, pt, p=p: (pt[i, p], 0, 0))

    grid_spec = pltpu.PrefetchScalarGridSpec(
        num_scalar_prefetch=1,
        grid=(b,),
        in_specs=([row((nq, H_A * 128)), row((nq, 128)), row((kvn16.shape[1], 768)),
                   pl.BlockSpec((CMP_BLOCK, 256), lambda i, pt: (0, 0)), row((wbuf, 256))]
                  + [page_spec(p) for p in range(n_pages)] * 2),
        out_specs=[row((nq, H_A * DH_A)), row((wbuf, 256))],
        scratch_shapes=[pltpu.VMEM((lp, 256), F32), pltpu.VMEM((lp, 256), BF16),
                        pltpu.VMEM((wbuf + 128, 256), F32)],
    )
    return pl.pallas_call(
        functools.partial(_nsa_sample_body, n_pages=n_pages, page=page, past=past, tq=tq),
        grid_spec=grid_spec,
        out_shape=[jax.ShapeDtypeStruct((b, nq, H_A * DH_A), F32),
                   jax.ShapeDtypeStruct((b, wbuf, 256), F32)],
        compiler_params=_cparams("arbitrary"),
        name="nsa_sample",
    )(page_table, q8, gates8, kvn16, w, win, *([cmp_pages] * n_pages), *([sel_pages] * n_pages))


def _cast_transpose_body(x_ref, o_ref):
    o_ref[...] = x_ref[...].T.astype(o_ref.dtype)


def _cast_transpose(x, rows=512):
    r, c = x.shape
    return pl.pallas_call(
        _cast_transpose_body,
        grid=(r // rows,),
        in_specs=[pl.BlockSpec((rows, c), lambda i: (i, 0))],
        out_specs=pl.BlockSpec((c, rows), lambda i: (0, i)),
        out_shape=jax.ShapeDtypeStruct((c, r), BF16),
        compiler_params=_cparams("parallel"),
        name="cast_transpose",
    )(x)


def _comb_candidates(v1, v2):
    k, half, lanes = P_TOPK, P_TOPK // 2, v1.shape[1]
    assert k == 16
    qi = lax.broadcasted_iota(jnp.int32, (half, lanes), 0).astype(F32)
    sums = [v1[0:1] + v2]
    idx = [lax.broadcasted_iota(jnp.int32, (k, lanes), 0).astype(F32)]
    for p in range(1, half):
        sums.append(v1[p:p + 1] + v2[0:half])
        idx.append(qi + float(p * k))
    sums.append(v1[half:k] + v2[0:1])
    idx.append((qi + float(half)) * float(k))
    return jnp.concatenate(sums, axis=0), jnp.concatenate(idx, axis=0)


def _peer_select(s1s, s2s):
    k, half = P_TOPK, P_TOPK // 2
    nt = len(s1s)
    poss, vals = _topk_pos_multi_fast(list(s1s) + list(s2s), 0, k)
    combs = [_comb_candidates(vals[t], vals[nt + t]) for t in range(nt)]
    poscs, vcs = _topk_pos_multi_fast([cb[0] for cb in combs], 0, k, combs[0][1])
    out = []
    for t in range(nt):
        pos1, v1, pos2, v2 = poss[t], vals[t], poss[nt + t], vals[nt + t]
        z = jnp.sum(jnp.exp(vcs[t] - vcs[t][0:1]), axis=0, keepdims=True)
        sel = jnp.where(poscs[t] < k, 1.0, 0.0)
        n1 = jnp.where(pos1 == 0.0, jnp.sum(sel[0:k], axis=0, keepdims=True), 0.0)
        for p in range(1, half):
            r0 = k + half * (p - 1)
            n1 = jnp.where(pos1 == float(p), jnp.sum(sel[r0:r0 + half], axis=0, keepdims=True), n1)
        for p in range(half, k):
            r0 = k + half * (half - 1) + (p - half)
            n1 = jnp.where(pos1 == float(p), sel[r0:r0 + 1], n1)
        f1 = jnp.where(pos1 < k, jnp.exp(s1s[t] - v1[0:1]), 0.0) / z
        out.append((n1, f1, pos2, jnp.exp(s2s[t] - v2[0:1])))
    return out


def _peer_body(x_ref, g_ref, gf_ref, wq_ref, keys_ref, u_ref, vt_ref, o_ref,
               hnt_scr, n1_scr, f1_scr, q2_scr, e2_scr, acc_scr, h_even, h_odd, *, ec, final_norm):
    c = pl.program_id(1)
    tm = x_ref.shape[0]

    @pl.when(c == 0)
    def _select():
        hn = _rmsnorm(x_ref[...], g_ref[...])
        hnt_scr[...] = hn.T.astype(BF16)
        acc_scr[...] = jnp.zeros_like(acc_scr)

        def head(h, carry):
            r0 = pl.multiple_of(h * 2 * P_DHALF, 2 * P_DHALF)
            qt = _dot(wq_ref[pl.ds(r0, 2 * P_DHALF), :], hnt_scr[...]).astype(BF16)
            lane_tiles = [slice(lt * 128, (lt + 1) * 128) for lt in range(tm // 128)]
            s1s = [_dot(keys_ref[2 * h], qt[0:P_DHALF, sl]) for sl in lane_tiles]
            s2s = [_dot(keys_ref[2 * h + 1], qt[P_DHALF:2 * P_DHALF, sl]) for sl in lane_tiles]
            for sl, (n1, f1, q2, e2) in zip(lane_tiles, _peer_select(s1s, s2s)):
                n1_scr[h, :, sl] = n1
                f1_scr[h, :, sl] = f1
                q2_scr[h, :, sl] = q2.astype(BF16)
                e2_scr[h, :, sl] = e2.astype(BF16)
            return carry

        lax.fori_loop(0, P_HEADS, head, 0)

    nc = N_EXP // ec
    assert nc % 2 == 0

    def weighted_acts(h_out):
        act = _gelu(_dot(u_ref[...], hnt_scr[...])).astype(BF16)
        for ai in range(ec // N_KEYS):
            a = c * (ec // N_KEYS) + ai
            w = None
            for h in range(P_HEADS):
                n1 = n1_scr[h, pl.ds(a, 1), :].astype(BF16)
                f1 = f1_scr[h, pl.ds(a, 1), :].astype(BF16)
                term = jnp.where(q2_scr[h] < n1, e2_scr[h] * f1, jnp.zeros((), BF16))
                w = term if w is None else w + term
            rows = slice(ai * N_KEYS, (ai + 1) * N_KEYS)
            h_out[rows, :] = w * act[rows]

    @pl.when(c == 0)
    def _first():
        weighted_acts(h_even)

    @pl.when(jnp.logical_and(c < nc, c % 2 == 1))
    def _odd():
        acc_scr[...] += _dot(vt_ref[...], h_even[...])
        weighted_acts(h_odd)

    @pl.when(jnp.logical_and(jnp.logical_and(c > 0, c < nc), c % 2 == 0))
    def _even():
        acc_scr[...] += _dot(vt_ref[...], h_odd[...])
        weighted_acts(h_even)

    @pl.when(c == nc)
    def _finish():
        y = x_ref[...] + (acc_scr[...] + _dot(vt_ref[...], h_odd[...])).T
        if final_norm:
            y = _rmsnorm(y, gf_ref[...])
        o_ref[...] = y


def _peer(x, g, gf, wq_t, keys, u, vt, final_norm, tm=512, ec=1024):
    n, d = x.shape
    nc = N_EXP // ec
    const2 = lambda i, c: (0, 0)
    return pl.pallas_call(
        functools.partial(_peer_body, ec=ec, final_norm=final_norm),
        grid=(n // tm, nc + 1),
        in_specs=[pl.BlockSpec((tm, d), lambda i, c: (i, 0)),
                  pl.BlockSpec((1, d), const2), pl.BlockSpec((1, d), const2),
                  pl.BlockSpec(wq_t.shape, const2),
                  pl.BlockSpec(keys.shape, lambda i, c: (0, 0, 0)),
                  pl.BlockSpec((ec, d), lambda i, c: (jnp.minimum(c, nc - 1), 0)),
                  pl.BlockSpec((d, ec), lambda i, c: (0, jnp.maximum(c - 1, 0)))],
        out_specs=pl.BlockSpec((tm, d), lambda i, c: (i, 0)),
        out_shape=jax.ShapeDtypeStruct((n, d), F32),
        scratch_shapes=[pltpu.VMEM((d, tm), BF16)]
        + [pltpu.VMEM((P_HEADS, N_KEYS, tm), F32)] * 2
        + [pltpu.VMEM((P_HEADS, N_KEYS, tm), BF16)] * 2
        + [pltpu.VMEM((d, tm), F32), pltpu.VMEM((ec, tm), BF16), pltpu.VMEM((ec, tm), BF16)],
        compiler_params=_cparams("parallel", "arbitrary"),
        name="peer",
    )(x, g.reshape(1, d), gf.reshape(1, d), wq_t, keys, u, vt)


def _unit_lower_inverses(ms, order):
    ps = [-m for m in ms]
    invs = list(ps)
    span = 2
    while span < order:
        ps = [_dot_x3(p, p) for p in ps]
        invs = [inv + p + _dot_x3(inv, p) for inv, p in zip(invs, ps)]
        span *= 2
    row = lax.broadcasted_iota(jnp.int32, ms[0].shape, 0)
    col = lax.broadcasted_iota(jnp.int32, ms[0].shape, 1)
    eye = jnp.where(row == col, 1.0, 0.0)
    return [inv + eye for inv in invs]


def _gdn_body(qkv_ref, z_ref, ba_ref, cb_ref, s0_ref, wc_ref, prm_ref, gn_ref, o_ref, s_ref,
              xc_scr, s_scr, *, valid):
    t = pl.program_id(1)
    c = GDN_CHUNK
    r = qkv_ref.shape[1]

    @pl.when(t == 0)
    def _():
        xc_scr[0:8, :] = cb_ref[0]
        s_scr[...] = s0_ref[0]

    xc_scr[8:8 + r, :] = qkv_ref[0]
    if r < c:
        xc_scr[8 + r:8 + c, :] = jnp.zeros((c - r, QKV_C), F32)
    wc = wc_ref[...]
    conv = wc[0:1] * xc_scr[pl.ds(5, c), :]
    for j in range(1, CONV_W):
        conv = conv + wc[j:j + 1] * xc_scr[pl.ds(5 + j, c), :]
    xc_scr[0:8, :] = xc_scr[c:c + 8, :]
    act = _silu(conv)

    row = lax.broadcasted_iota(jnp.int32, (c, 1), 0)
    live = row < valid
    ba = ba_ref[0]
    if r < c:
        ba = jnp.concatenate([ba, jnp.zeros((c - r, 128), F32)], axis=0)
    prm = prm_ref[...]
    beta_all = jnp.where(live, jax.nn.sigmoid(ba), 0.0)
    g_all = jnp.where(live, prm[0:1] * _softplus(ba + prm[1:2]), 0.0)
    gc_all = g_all
    s = 1
    while s < c:
        gc_all = gc_all + _shift_rows(gc_all, s, 0.0)
        s *= 2
    gct = gc_all.T
    ri = lax.broadcasted_iota(jnp.int32, (c, c), 0)
    ci = lax.broadcasted_iota(jnp.int32, (c, c), 1)
    tri = ri >= ci
    strict = ri > ci
    z = z_ref[0]
    if r < c:
        z = jnp.concatenate([z, jnp.zeros((c - r, H_C * DV_C), F32)], axis=0)
    per_head = []
    for h in range(H_C):
        q = act[:, h * DK_C:(h + 1) * DK_C]
        k = act[:, (H_C + h) * DK_C:(H_C + h + 1) * DK_C]
        v = act[:, (2 * H_C + h) * DK_C:(2 * H_C + h + 1) * DK_C]
        q = q * lax.rsqrt(jnp.sum(q * q, -1, keepdims=True) + 1e-6) * (DK_C ** -0.5)
        k = k * lax.rsqrt(jnp.sum(k * k, -1, keepdims=True) + 1e-6)
        beta = beta_all[:, h:h + 1]
        gcol = gc_all[:, H_C + h:H_C + h + 1]
        grow = gct[H_C + h:H_C + h + 1, :]
        glast = gc_all[c - 1:c, H_C + h:H_C + h + 1]
        lmat = jnp.where(tri, jnp.exp(jnp.where(tri, gcol - grow, 0.0)), 0.0)
        kb = k * beta
        kbf = k.astype(BF16)
        m = jnp.where(strict, _dot_nt(kb.astype(BF16), kbf) * lmat, 0.0)
        eg = jnp.exp(gcol)
        aqk = (_dot_nt(q.astype(BF16), kbf) * lmat).astype(BF16)
        per_head.append(dict(m=m, vb=(v * beta).astype(BF16), kbe=(kb * eg).astype(BF16), aqk=aqk,
                             qd=(q * eg).astype(BF16), kdt=(k * jnp.exp(glast - gcol)).T.astype(BF16),
                             decay=jnp.exp(glast)))
    tinvs = _unit_lower_inverses([ph["m"] for ph in per_head], min(valid, c))
    for h, (ph, tinv) in enumerate(zip(per_head, tinvs)):
        tinv = tinv.astype(BF16)
        u = _dot(tinv, ph["vb"])
        w = _dot(tinv, ph["kbe"])
        st = s_scr[h]
        stb = st.astype(BF16)
        v_new = u - _dot(w.astype(BF16), stb)
        o = _dot(ph["qd"], stb) + _dot(ph["aqk"], v_new.astype(BF16))
        s_scr[h] = st * ph["decay"] + _dot(ph["kdt"], v_new.astype(BF16))
        o = o * lax.rsqrt(jnp.mean(o * o, -1, keepdims=True) + EPS) * gn_ref[...]
        o = o * _silu(z[:, h * DV_C:(h + 1) * DV_C])
        o_ref[0, :, h * DV_C:(h + 1) * DV_C] = o[0:r].astype(o_ref.dtype)

    @pl.when(t == pl.num_programs(1) - 1)
    def _():
        s_ref[0] = s_scr[...]


def _gdn(qkv, z, ba, cbuf8, s0, wc, prm, gn, valid):
    b, t, _ = qkv.shape
    r = min(t, GDN_CHUNK)
    tile = lambda wd: pl.BlockSpec((1, r, wd), lambda i, j: (i, j, 0))
    return pl.pallas_call(
        functools.partial(_gdn_body, valid=valid),
        grid=(b, t // r),
        in_specs=[tile(QKV_C), tile(H_C * DV_C), tile(128),
                  pl.BlockSpec((1, 8, QKV_C), lambda i, j: (i, 0, 0)),
                  pl.BlockSpec((1, H_C, DK_C, DV_C), lambda i, j: (i, 0, 0, 0)),
                  pl.BlockSpec((CONV_W, QKV_C), lambda i, j: (0, 0)),
                  pl.BlockSpec((8, 128), lambda i, j: (0, 0)),
                  pl.BlockSpec((1, DV_C), lambda i, j: (0, 0))],
        out_specs=[tile(H_C * DV_C),
                   pl.BlockSpec((1, H_C, DK_C, DV_C), lambda i, j: (i, 0, 0, 0))],
        out_shape=[jax.ShapeDtypeStruct((b, t, H_C * DV_C), F32 if r < GDN_CHUNK else BF16),
                   jax.ShapeDtypeStruct((b, H_C, DK_C, DV_C), F32)],
        scratch_shapes=[pltpu.VMEM((GDN_CHUNK + 8, QKV_C), F32), pltpu.VMEM((H_C, DK_C, DV_C), F32)],
        compiler_params=_cparams("parallel", "arbitrary"),
        name="gdn",
    )(qkv, z, ba, cbuf8, s0, wc, prm, gn)


def _block_diag(w):
    n, d, e = w.shape
    eye = jnp.eye(n, dtype=w.dtype)
    return (w[:, :, None, :] * eye[:, None, :, None]).reshape(n * d, n * e)


def _pad_rows(x, rows, front=0):
    return jnp.pad(x, ((0, 0), (front, rows - x.shape[1] - front), (0, 0)))


def _layer_a_weights(w_in, w_pool):
    d = w_in.shape[0]
    q_a, kv_a = H_A * DH_A, 3 * 2 * G_A * DH_A
    wq = w_in[:, :q_a].reshape(d, G_A, R_A, DH_A)
    zeros = jnp.zeros_like(wq)
    wq = jnp.stack([jnp.concatenate([wq[:, 0], zeros[:, 0]], -1),
                    jnp.concatenate([zeros[:, 1], wq[:, 1]], -1)], axis=1).reshape(d, H_A * 128)
    i1 = q_a + kv_a
    i2 = i1 + 3 * H_A
    wg = jnp.pad(w_in[:, i1:i2], ((0, 0), (0, 128 - 3 * H_A)))
    w = jnp.concatenate([wq, w_in[:, q_a:i1], wg, w_in[:, i2:]], axis=1).astype(BF16)
    wp = jnp.concatenate([jnp.broadcast_to(w_pool[0][:, None], (CMP_BLOCK, 128)),
                          jnp.broadcast_to(w_pool[1][:, None], (CMP_BLOCK, 128))], axis=1)
    return w, wp


def kernel(x_prompt, x_sample, cache_cmp_kv, cache_sel_kv, cache_win_kv, state_rg_h, state_rg_conv, state_gdn_S, state_gdn_conv, page_table, norm_mix, norm_ffn, norm_final, w_in_a, w_cmp_pool, w_rg_conv, b_rg_conv, w_rg_a, b_rg_a, w_rg_x, b_rg_x, rg_lambda, w_out_a, w_in_c, w_c_conv, c_A_log, c_dt_bias, c_norm, w_out_c, peer_w_q, peer_keys, peer_u, peer_v):
    bp, tp, d = x_prompt.shape
    bs, ts, _ = x_sample.shape
    np_, ns = bp * tp, bs * ts
    xp = x_prompt.reshape(np_, d)
    xs = x_sample.reshape(ns, d)
    kv_w = 3 * 2 * G_A * DH_A

    def peer_layer(x, layer, final):
        wq_t = _cast_transpose(peer_w_q[layer])
        keys = peer_keys[layer].reshape(2 * P_HEADS, N_KEYS, P_DHALF).astype(BF16)
        u = peer_u[layer].astype(BF16)
        vt = _cast_transpose(peer_v[layer])
        return [_peer(xx, norm_ffn[layer], norm_final, wq_t, keys, u, vt, final) for xx in x]

    w0, wp = _layer_a_weights(w_in_a[0], w_cmp_pool[0])
    qo, ko, go, ro = 0, H_A * 128, H_A * 128 + kv_w, H_A * 128 + kv_w + 128
    outs = ((qo, H_A * 128, None), (ko, kv_w, None), (ko + 256, 512, None), (go, 128, "sigmoid"),
            (ro, D_RNN, None), (ro + D_RNN, D_RNN, None))
    dts = (BF16, F32, BF16, F32, F32, F32)
    qp, kvp, kvbp, gtp, xrp, ygp = _norm_proj(xp, norm_mix[0], w0, outs, dts, 512)
    qs, kvs, _, gts, xrs, ygs = _norm_proj(xs, norm_mix[0], w0, outs, dts, min(512, ns))

    kvp3 = kvp.reshape(bp, tp, kv_w)
    pe, po = _nsa_pool(kvp3, wp)
    att_p = _nsa_prompt(qp.reshape(bp, tp, H_A * 128), gtp.reshape(bp, tp, 128), pe, po,
                        kvbp.reshape(bp, tp, 512)).reshape(np_, H_A * DH_A)

    kvs3 = kvs.reshape(bs, ts, kv_w)
    n_pool, page = cache_cmp_kv.shape[1], cache_cmp_kv.shape[2]
    att_s8, win_new = _nsa_sample(
        page_table,
        _pad_rows(qs.astype(F32).reshape(bs, ts, H_A * 128), 8),
        _pad_rows(gts.reshape(bs, ts, 128), 8),
        _pad_rows(kvs3, 16), wp,
        cache_win_kv[0].reshape(bs, -1, 256),
        cache_cmp_kv[0].reshape(n_pool, page, 256),
        cache_sel_kv[0].reshape(n_pool, page, 256), ts)
    att_s = att_s8[:, :ts].reshape(ns, H_A * DH_A).astype(BF16)

    rg = (w_rg_conv[0], b_rg_conv[0].reshape(1, D_RNN), _block_diag(w_rg_a[0]).astype(BF16),
          b_rg_a[0].reshape(1, D_RNN), _block_diag(w_rg_x[0]).astype(BF16), b_rg_x[0].reshape(1, D_RNN),
          rg_lambda[0].reshape(1, D_RNN))
    xrp3 = xrp.reshape(bp, tp, D_RNN)
    rec_p, hl_p = _rglru_prompt(xrp3, ygp.reshape(bp, tp, D_RNN), *rg)
    xrs3 = xrs.reshape(bs, ts, D_RNN)
    rec_s, hl_s = _rglru_sample(xrs3.transpose(1, 0, 2), ygs.reshape(bs, ts, D_RNN).transpose(1, 0, 2),
                                state_rg_conv[0].transpose(1, 0, 2), state_rg_h[0], *rg)
    rec_s = rec_s.transpose(1, 0, 2).reshape(ns, D_RNN).astype(BF16)

    wo = w_out_a[0].astype(BF16)
    wo_att, wo_rec = wo[:H_A * DH_A], wo[H_A * DH_A:]
    xp = _out_proj([att_p, rec_p.reshape(np_, D_RNN)], [wo_att, wo_rec], xp, 512)
    xs = _out_proj([att_s, rec_s], [wo_att, wo_rec], xs, min(512, ns))
    xp, xs = peer_layer([xp, xs], 0, False)

    d_c = H_C * DV_C
    w1 = jnp.pad(w_in_c[0], ((0, 0), (0, 128 - 2 * H_C))).astype(BF16)
    outs1 = ((0, QKV_C, None), (QKV_C, d_c, None), (QKV_C + d_c, 128, None))
    dts1 = (F32, F32, F32)
    qkv_p, z_p, ba_p = _norm_proj(xp, norm_mix[1], w1, outs1, dts1, 512)
    qkv_s, z_s, ba_s = _norm_proj(xs, norm_mix[1], w1, outs1, dts1, min(512, ns))
    lane = jnp.arange(128)
    hsel = jnp.clip(lane - H_C, 0, H_C - 1)
    in_g = (lane >= H_C) & (lane < 2 * H_C)
    prm = jnp.zeros((8, 128), F32)
    prm = prm.at[0].set(jnp.where(in_g, -jnp.exp(c_A_log[0])[hsel], 0.0))
    prm = prm.at[1].set(jnp.where(in_g, c_dt_bias[0][hsel], 0.0))
    gn = c_norm[0].reshape(1, DV_C)
    qkv_p3 = qkv_p.reshape(bp, tp, QKV_C)
    o_p, gs_p = _gdn(qkv_p3, z_p.reshape(bp, tp, d_c), ba_p.reshape(bp, tp, 128),
                     jnp.zeros((bp, 8, QKV_C), F32), jnp.zeros((bp, H_C, DK_C, DV_C), F32),
                     w_c_conv[0], prm, gn, GDN_CHUNK)
    qkv_s3 = qkv_s.reshape(bs, ts, QKV_C)
    o_s, gs_s = _gdn(_pad_rows(qkv_s3, 8), _pad_rows(z_s.reshape(bs, ts, d_c), 8),
                     _pad_rows(ba_s.reshape(bs, ts, 128), 8),
                     _pad_rows(state_gdn_conv[0], 8, front=8 - (CONV_W - 1)), state_gdn_S[0],
                     w_c_conv[0], prm, gn, ts)
    wo_c = w_out_c[0].astype(BF16)
    xp = _out_proj([o_p.reshape(np_, d_c)], [wo_c], xp, 512)
    xs = _out_proj([o_s[:, :ts].reshape(ns, d_c).astype(BF16)], [wo_c], xs, min(512, ns))
    yp, ys = peer_layer([xp, xs], 1, True)

    kvp6 = kvp3.reshape(bp, tp, 3, 2, G_A, DH_A)
    kvs6 = kvs3.reshape(bs, ts, 3, 2, G_A, DH_A)
    wlen = min(WINDOW, tp)
    cw = CONV_W - 1
    return (yp.reshape(bp, tp, d), ys.reshape(bs, ts, d),
            kvp6[None, :, :, 0], kvs6[None, :, :, 0], kvp6[None, :, :, 1], kvs6[None, :, :, 1],
            kvp6[None, :, tp - wlen:, 2], win_new.reshape(bs, -1, 2, G_A, DH_A)[None],
            hl_p[None], hl_s[None], xrp3[None, :, tp - cw:], xrs3[None, :, ts - cw:],
            gs_p[None], gs_s[None], qkv_p3[None, :, tp - cw:], qkv_s3[None, :, ts - cw:])
```

```python
import functools
import math

import jax
import jax.numpy as jnp
from jax import lax
from jax.experimental import pallas as pl
from jax.experimental.pallas import tpu as pltpu

F32 = jnp.float32
BF16 = jnp.bfloat16

D_MODEL = 1024
H_A, G_A, R_A, DH_A = 8, 2, 4, 64
CMP_BLOCK, SEL_BLOCK, TOP_N, WINDOW, Q_BLOCK = 32, 64, 16, 512, 128
FORCE_BONUS = 1.0e3
NEG_INF = -1.0e30
N_SEL_LANES = 128
D_RNN, RG_C, CONV_W = 512, 8.0, 4
H_C, DK_C, DV_C = 8, 128, 128
QKV_C = 3 * H_C * DK_C
GDN_CHUNK = 128
P_HEADS, N_KEYS, P_DHALF, P_TOPK = 8, 128, 128, 16
N_EXP = N_KEYS * N_KEYS
EPS = 1e-6

VMEM_LIMIT = 56 * 1024 * 1024


def _cparams(*sem):
    return pltpu.CompilerParams(dimension_semantics=sem, vmem_limit_bytes=VMEM_LIMIT)


def _rmsnorm(x, g):
    return x * lax.rsqrt(jnp.mean(x * x, -1, keepdims=True) + EPS) * g


def _gelu(x):
    k = -2.0 * math.sqrt(2.0 / math.pi) * math.log2(math.e)
    return x / (1.0 + jnp.exp2(x * (k + (k * 0.044715) * (x * x))))


def _softplus(x):
    return jnp.maximum(x, 0.0) + jnp.log1p(jnp.exp(-jnp.abs(x)))


def _silu(x):
    return x * jax.nn.sigmoid(x)


def _neg_expm1(x):
    t = jnp.tanh(0.5 * x)
    return -2.0 * t / (1.0 - t)


def _dot(a, b):
    return jnp.dot(a, b, preferred_element_type=F32)


def _dot_nt(a, b):
    return lax.dot_general(a, b, (((1,), (1,)), ((), ())), preferred_element_type=F32)


def _dot_x3(a, b):
    a_hi = a.astype(BF16)
    b_hi = b.astype(BF16)
    a_lo = (a - a_hi.astype(F32)).astype(BF16)
    b_lo = (b - b_hi.astype(F32)).astype(BF16)
    return _dot(a_hi, b_hi) + (_dot(a_hi, b_lo) + _dot(a_lo, b_hi))


def _shift_rows(x, s, fill):
    rolled = pltpu.roll(x, s, 0)
    row = lax.broadcasted_iota(jnp.int32, x.shape, 0)
    return jnp.where(row >= s, rolled, fill)


def _topk_pos_multi(ss, axis, k, idx=None):
    shape = ss[0].shape
    if idx is None:
        idx = lax.broadcasted_iota(jnp.int32, shape, axis).astype(F32)
    vshape = tuple(k if d == axis else shape[d] for d in range(len(shape)))
    vidx = lax.broadcasted_iota(jnp.int32, vshape, axis)
    n = len(ss)
    ss = list(ss)
    poss = [jnp.full(shape, float(k), F32) for _ in range(n)]
    valss = [jnp.zeros(vshape, F32) for _ in range(n)]
    for i in range(k):
        ms = [jnp.max(s, axis=axis, keepdims=True) for s in ss]
        js = [jnp.min(jnp.where(s == m, idx, 1e9), axis=axis, keepdims=True) for s, m in zip(ss, ms)]
        ohs = [idx == j for j in js]
        poss = [jnp.where(oh, float(i), pos) for oh, pos in zip(ohs, poss)]
        ss = [jnp.where(oh, -jnp.inf, s) for oh, s in zip(ohs, ss)]
        valss = [jnp.where(vidx == i, m, vals) for m, vals in zip(ms, valss)]
    return poss, valss


def _topk_pos(s, axis, k, idx=None):
    poss, valss = _topk_pos_multi([s], axis, k, idx)
    return poss[0], valss[0]


def _topk_pos_multi_fast(ss, axis, k, idx=None):
    shape = ss[0].shape
    vshape = tuple(k if d == axis else shape[d] for d in range(len(shape)))
    vidx = lax.broadcasted_iota(jnp.int32, vshape, axis)
    n = len(ss)
    rest = list(ss)
    poss = [jnp.full(shape, float(k), F32) for _ in range(n)]
    valss = [jnp.zeros(vshape, F32) for _ in range(n)]
    for i in range(k):
        ms = [jnp.max(s, axis=axis, keepdims=True) for s in rest]
        ohs = [s == m for s, m in zip(rest, ms)]
        poss = [jnp.where(oh, float(i), pos) for oh, pos in zip(ohs, poss)]
        rest = [jnp.where(oh, -jnp.inf, s) for oh, s in zip(ohs, rest)]
        valss = [jnp.where(vidx == i, m, vals) for m, vals in zip(ms, valss)]
    removed = sum(jnp.sum(jnp.where(pos < k, 1.0, 0.0)) for pos in poss)
    expected = float(n * k * math.prod(shape) // shape[axis])
    return lax.cond(removed == expected,
                    lambda: (tuple(poss), tuple(valss)),
                    lambda: tuple(tuple(x) for x in _topk_pos_multi(ss, axis, k, idx)))


def _norm_proj_body(x_ref, g_ref, w_ref, *out_refs, outs):
    h = _rmsnorm(x_ref[...], g_ref[...]).astype(BF16)
    for o_ref, (off, width, act) in zip(out_refs, outs):
        z = _dot(h, w_ref[:, off:off + width])
        if act == "sigmoid":
            z = jax.nn.sigmoid(z)
        o_ref[...] = z.astype(o_ref.dtype)


def _norm_proj(x, g, w, outs, dtypes, tm):
    n, d = x.shape
    body = functools.partial(_norm_proj_body, outs=outs)
    return pl.pallas_call(
        body,
        grid=(n // tm,),
        in_specs=[pl.BlockSpec((tm, d), lambda i: (i, 0)),
                  pl.BlockSpec((1, d), lambda i: (0, 0)),
                  pl.BlockSpec(w.shape, lambda i: (0, 0))],
        out_specs=[pl.BlockSpec((tm, wd), lambda i: (i, 0)) for (_, wd, _) in outs],
        out_shape=[jax.ShapeDtypeStruct((n, wd), dt) for (_, wd, _), dt in zip(outs, dtypes)],
        compiler_params=_cparams("parallel"),
        name="norm_proj",
    )(x, g.reshape(1, d), w)


def _out_proj_body(*refs, n_in):
    a_refs, w_refs = refs[:n_in], refs[n_in:2 * n_in]
    x_ref, o_ref = refs[2 * n_in], refs[2 * n_in + 1]
    acc = x_ref[...]
    for a, w in zip(a_refs, w_refs):
        acc = acc + _dot(a[...], w[...])
    o_ref[...] = acc


def _out_proj(acts, ws, x, tm):
    n, d = x.shape
    n_in = len(acts)
    return pl.pallas_call(
        functools.partial(_out_proj_body, n_in=n_in),
        grid=(n // tm,),
        in_specs=([pl.BlockSpec((tm, a.shape[1]), lambda i: (i, 0)) for a in acts]
                  + [pl.BlockSpec(w.shape, lambda i: (0, 0)) for w in ws]
                  + [pl.BlockSpec((tm, d), lambda i: (i, 0))]),
        out_specs=pl.BlockSpec((tm, d), lambda i: (i, 0)),
        out_shape=jax.ShapeDtypeStruct((n, d), F32),
        compiler_params=_cparams("parallel"),
        name="out_proj",
    )(*acts, *ws, x)


def _rg_gates(xf, wa, ba, wx, bx, lam):
    xb = xf.astype(BF16)
    r = jax.nn.sigmoid(_dot(xb, wa) + ba)
    i = jax.nn.sigmoid(_dot(xb, wx) + bx)
    log_a = -RG_C * r * _softplus(-lam)
    a = jnp.exp(log_a)
    b = jnp.sqrt(_neg_expm1(2.0 * log_a)) * (i * xf)
    return a, b


def _rglru_prompt_body(xr_ref, yg_ref, wc_ref, bc_ref, wa_ref, ba_ref, wx_ref, bx_ref, lam_ref,
                       rec_ref, hl_ref, xc_scr, h_scr, *, rt):
    t = pl.program_id(1)

    @pl.when(t == 0)
    def _():
        xc_scr[0:8, :] = jnp.zeros((8, D_RNN), F32)
        h_scr[...] = jnp.zeros_like(h_scr)

    xc_scr[8:8 + rt, :] = xr_ref[0]
    wc = wc_ref[...]
    conv = wc[0:1] * xc_scr[pl.ds(5, rt), :]
    for j in range(1, CONV_W):
        conv = conv + wc[j:j + 1] * xc_scr[pl.ds(5 + j, rt), :]
    xc_scr[0:8, :] = xc_scr[rt:rt + 8, :]
    xf = conv + bc_ref[...]
    a, b = _rg_gates(xf, wa_ref[...], ba_ref[...], wx_ref[...], bx_ref[...], lam_ref[...])
    s = 1
    while s < rt:
        b = a * _shift_rows(b, s, 0.0) + b
        a = a * _shift_rows(a, s, 1.0)
        s *= 2
    h = a * h_scr[0:1, :] + b
    h_last = h[rt - 1:rt, :]
    h_scr[0:1, :] = h_last
    hl_ref[0] = h_last
    rec_ref[0] = (h * _gelu(yg_ref[0])).astype(rec_ref.dtype)


def _rglru_prompt(xr, yg, wc, bc, wa, ba, wx, bx, lam, rt=256):
    b, t, d = xr.shape
    vec = lambda: pl.BlockSpec((1, d), lambda i, j: (0, 0))
    mat = lambda: pl.BlockSpec((d, d), lambda i, j: (0, 0))
    rec, hl = pl.pallas_call(
        functools.partial(_rglru_prompt_body, rt=rt),
        grid=(b, t // rt),
        in_specs=[pl.BlockSpec((1, rt, d), lambda i, j: (i, j, 0)),
                  pl.BlockSpec((1, rt, d), lambda i, j: (i, j, 0)),
                  pl.BlockSpec((CONV_W, d), lambda i, j: (0, 0)),
                  vec(), mat(), vec(), mat(), vec(), vec()],
        out_specs=[pl.BlockSpec((1, rt, d), lambda i, j: (i, j, 0)),
                   pl.BlockSpec((1, 1, d), lambda i, j: (i, 0, 0))],
        out_shape=[jax.ShapeDtypeStruct((b, t, d), BF16),
                   jax.ShapeDtypeStruct((b, 1, d), F32)],
        scratch_shapes=[pltpu.VMEM((rt + 8, d), F32), pltpu.VMEM((8, d), F32)],
        compiler_params=_cparams("arbitrary", "arbitrary"),
        name="rglru_prompt",
    )(xr, yg, wc, bc, wa, ba, wx, bx, lam)
    return rec, hl[:, 0]


def _rglru_sample_body(xr_ref, yg_ref, buf_ref, h0_ref, wc_ref, bc_ref, wa_ref, ba_ref, wx_ref,
                       bx_ref, lam_ref, rec_ref, hl_ref, *, t_len):
    xs = [buf_ref[j] for j in range(CONV_W - 1)] + [xr_ref[j] for j in range(t_len)]
    wc = wc_ref[...]
    h = h0_ref[...]
    for t in range(t_len):
        conv = wc[0:1] * xs[t]
        for j in range(1, CONV_W):
            conv = conv + wc[j:j + 1] * xs[t + j]
        xf = conv + bc_ref[...]
        a, b = _rg_gates(xf, wa_ref[...], ba_ref[...], wx_ref[...], bx_ref[...], lam_ref[...])
        h = a * h + b
        rec_ref[t] = h * _gelu(yg_ref[t])
    hl_ref[...] = h


def _rglru_sample(xr, yg, buf, h0, wc, bc, wa, ba, wx, bx, lam):
    t_len, b, d = xr.shape
    return pl.pallas_call(
        functools.partial(_rglru_sample_body, t_len=t_len),
        out_shape=[jax.ShapeDtypeStruct((t_len, b, d), F32),
                   jax.ShapeDtypeStruct((b, d), F32)],
        compiler_params=pltpu.CompilerParams(vmem_limit_bytes=VMEM_LIMIT),
        name="rglru_sample",
    )(xr, yg, buf, h0, wc, bc, wa, ba, wx, bx, lam)


def _pool_blocks(x3, w):
    first = jnp.sum(x3[:, :CMP_BLOCK, :] * w[None], axis=1)
    second = jnp.sum(x3[:, CMP_BLOCK:, :] * w[None], axis=1)
    return first, second


def _nsa_pool_body(x_ref, w_ref, pe_ref, po_ref):
    x = x_ref[0]
    nb = x.shape[0] // SEL_BLOCK
    pe, po = _pool_blocks(x.reshape(nb, SEL_BLOCK, x.shape[1]), w_ref[...])
    pe_ref[0] = pe
    po_ref[0] = po


def _nsa_pool(kv, w, rows=1024):
    b, t, _ = kv.shape
    nb = rows // SEL_BLOCK
    return pl.pallas_call(
        _nsa_pool_body,
        grid=(b, t // rows),
        in_specs=[pl.BlockSpec((1, rows, 256), lambda i, j: (i, j, 0)),
                  pl.BlockSpec((CMP_BLOCK, 256), lambda i, j: (0, 0))],
        out_specs=[pl.BlockSpec((1, nb, 256), lambda i, j: (i, j, 0))] * 2,
        out_shape=[jax.ShapeDtypeStruct((b, t // SEL_BLOCK, 256), F32)] * 2,
        compiler_params=_cparams("parallel", "parallel"),
        name="nsa_pool",
    )(kv, w)


def _cmp_branch(q, qpos, pe, po, nq):
    jl = lax.broadcasted_iota(jnp.int32, (1, N_SEL_LANES), 1)
    vis_e = (SEL_BLOCK * jl + (CMP_BLOCK - 1)) <= qpos
    vis_o = (SEL_BLOCK * jl + (SEL_BLOCK - 1)) <= qpos
    s_e = jnp.where(vis_e, _dot_nt(q, pe[:, :128].astype(BF16)), NEG_INF)
    s_o = jnp.where(vis_o, _dot_nt(q, po[:, :128].astype(BF16)), NEG_INF)
    m = jnp.maximum(jnp.max(s_e, -1, keepdims=True), jnp.max(s_o, -1, keepdims=True))
    p_e = jnp.where(vis_e, jnp.exp(s_e - m), 0.0)
    p_o = jnp.where(vis_o, jnp.exp(s_o - m), 0.0)
    den = jnp.sum(p_e, -1, keepdims=True) + jnp.sum(p_o, -1, keepdims=True)
    inv = jnp.where(den > 0.0, 1.0 / jnp.maximum(den, 1e-30), 0.0)
    p_e = p_e * inv
    p_o = p_o * inv
    o = _dot(p_e.astype(BF16), pe[:, 128:].astype(BF16)) + _dot(p_o.astype(BF16), po[:, 128:].astype(BF16))
    p = p_e + p_o
    imp = p[0:nq]
    for r in range(1, R_A):
        imp = imp + p[r * nq:(r + 1) * nq]
    return o, imp


def _select_blocks(imp, qpos, blk_axis):
    shape = (N_SEL_LANES, 1) if blk_axis == 0 else (1, N_SEL_LANES)
    blk = lax.broadcasted_iota(jnp.int32, shape, blk_axis)
    valid = blk * SEL_BLOCK <= qpos
    forced = jnp.logical_or(blk == 0, blk == (qpos >> 6))
    score = jnp.where(valid, imp + jnp.where(forced, FORCE_BONUS, 0.0), -jnp.inf)
    pos, _ = _topk_pos(score, blk_axis, TOP_N)
    return jnp.where(valid, jnp.where(pos < TOP_N, 1.0, 0.0), 0.0)


def _expand_sel(sel, first_blk, nkeys):
    bi = lax.broadcasted_iota(jnp.int32, (N_SEL_LANES, nkeys), 0)
    ki = lax.broadcasted_iota(jnp.int32, (N_SEL_LANES, nkeys), 1)
    e = jnp.where(bi - first_blk == (ki >> 6), 1.0, 0.0).astype(BF16)
    x = _dot(sel.astype(BF16), e)
    return jnp.concatenate([x] * R_A, axis=0)


def _masked_attend(q, k, v, ok):
    s = jnp.where(ok, _dot_nt(q, k), NEG_INF)
    m = jnp.max(s, -1, keepdims=True)
    p = jnp.where(ok, jnp.exp(s - m), 0.0)
    l = jnp.sum(p, -1, keepdims=True)
    return _dot(p.astype(BF16), v) / l


def _window_ok(qpos, kpos):
    d = qpos - kpos
    return jnp.logical_and(d >= 0, d <= WINDOW)


def _gate_and_place(gt, branches):
    heads = []
    for h in range(H_A):
        o = None
        for br in range(3):
            term = gt[:, 3 * h + br:3 * h + br + 1] * branches[br][h]
            o = term if o is None else o + term
        if (h % 2) != h // R_A:
            o = pltpu.roll(o, DH_A, 1)
        heads.append(o)
    lane = lax.broadcasted_iota(jnp.int32, (1, 128), 1)
    return [jnp.where(lane < DH_A, heads[2 * j], heads[2 * j + 1]) for j in range(H_A // 2)]


def _split_heads(o, nq):
    return [o[r * nq:(r + 1) * nq] for r in range(R_A)]


SEL_CHUNK = 512


def _nsa_prompt_body(q_ref, gt_ref, pe_ref, po_ref, kv_ref, oh_ref, vt_ref, o_ref, acc_scr):
    i = pl.program_id(1)
    nq = Q_BLOCK
    base = i * nq
    pe, po = pe_ref[0], po_ref[0]
    rowi = lax.broadcasted_iota(jnp.int32, (R_A * nq, 1), 0)
    qpos = base + (rowi & (nq - 1))
    qpos_row = base + lax.broadcasted_iota(jnp.int32, (1, nq), 1)
    heads_c, heads_w, qats = [], [], []
    for g in range(G_A):
        q = jnp.concatenate([q_ref[0, :, (g * R_A + r) * 128:(g * R_A + r + 1) * 128] for r in range(R_A)], axis=0)
        q = q * jnp.asarray(DH_A ** -0.5, BF16)
        o_c, imp = _cmp_branch(q, qpos, pe, po, nq)
        heads_c += _split_heads(o_c, nq)
        sel_t = _select_blocks(imp.T, qpos_row, 0)
        selterm_t = jnp.where(sel_t > 0.5, 0.0, NEG_INF).astype(BF16)
        qh_t = [jnp.concatenate([qh.astype(F32).T.astype(BF16), selterm_t], axis=0) for qh in _split_heads(q, nq)]
        qats += [jnp.concatenate(qh_t[2 * j:2 * j + 2], axis=1) for j in range(R_A // 2)]

        wlen = WINDOW + nq
        ws = pl.multiple_of(jnp.maximum(i - WINDOW // nq, 0) * nq, nq)
        kw = kv_ref[0, pl.ds(ws, wlen), 256:384]
        vw = kv_ref[0, pl.ds(ws, wlen), 384:512]
        kpos = ws + lax.broadcasted_iota(jnp.int32, (1, wlen), 1)
        heads_w += _split_heads(_masked_attend(q, kw, vw, _window_ok(qpos, kpos)), nq)

    acc_scr[...] = jnp.zeros_like(acc_scr)

    def step(k0, ms, ls, diagonal):
        ka = jnp.concatenate([kv_ref[0, pl.ds(k0, SEL_CHUNK), 0:128], oh_ref[pl.ds(k0, SEL_CHUNK), :]], axis=1)
        vt = vt_ref[0, :, pl.ds(k0, SEL_CHUNK)]
        if diagonal:
            qpos_pair = base + (lax.broadcasted_iota(jnp.int32, (1, 2 * nq), 1) & (nq - 1))
            visible = k0 + lax.broadcasted_iota(jnp.int32, (SEL_CHUNK, 1), 0) <= qpos_pair
        ss = [_dot(ka, qat) for qat in qats]
        if diagonal:
            ss = [jnp.where(visible, s, NEG_INF) for s in ss]
        ms_new = [jnp.maximum(m, jnp.max(s, axis=0, keepdims=True)) for m, s in zip(ms, ss)]
        ps = [jnp.exp(s - m) for s, m in zip(ss, ms_new)]
        pvs = [_dot(vt, p.astype(BF16)) for p in ps]
        ls_new = []
        for j in range(n_pairs):
            alpha = jnp.exp(ms[j] - ms_new[j])
            ls_new.append(alpha * ls[j] + jnp.sum(ps[j], axis=0, keepdims=True))
            acc_scr[j] = alpha * acc_scr[j] + pvs[j]
        return tuple(ms_new), tuple(ls_new)

    def past_chunk(c, carry):
        return step(pl.multiple_of(c * SEL_CHUNK, SEL_CHUNK), carry[0], carry[1], False)

    n_pairs = H_A // 2
    init = (tuple(jnp.full((1, 2 * nq), NEG_INF, F32) for _ in range(n_pairs)),
            tuple(jnp.zeros((1, 2 * nq), F32) for _ in range(n_pairs)))
    n_past = (base + nq - 1) // SEL_CHUNK
    ms, ls = lax.fori_loop(0, n_past, past_chunk, init)
    _, ls = step(pl.multiple_of(n_past * SEL_CHUNK, SEL_CHUNK), ms, ls, True)
    heads_s = []
    for j in range(n_pairs):
        o_pair = acc_scr[j] / ls[j]
        heads_s += [o_pair[:, 0:nq].T, o_pair[:, nq:2 * nq].T]
    for j, tile in enumerate(_gate_and_place(gt_ref[0], (heads_c, heads_s, heads_w))):
        o_ref[0, :, j * 128:(j + 1) * 128] = tile.astype(o_ref.dtype)


def _nsa_prompt(qpad, gates, pe, po, kvb):
    b, t, _ = qpad.shape
    nq = Q_BLOCK
    blk_of_key = jnp.arange(t, dtype=jnp.int32)[:, None] // SEL_BLOCK
    onehot = (blk_of_key == jnp.arange(N_SEL_LANES, dtype=jnp.int32)[None, :]).astype(BF16)
    v_sel_t = kvb[:, :, 128:256].transpose(0, 2, 1)
    return pl.pallas_call(
        _nsa_prompt_body,
        grid=(b, t // nq),
        in_specs=[pl.BlockSpec((1, nq, H_A * 128), lambda i, j: (i, j, 0)),
                  pl.BlockSpec((1, nq, 128), lambda i, j: (i, j, 0)),
                  pl.BlockSpec((1, N_SEL_LANES, 256), lambda i, j: (i, 0, 0)),
                  pl.BlockSpec((1, N_SEL_LANES, 256), lambda i, j: (i, 0, 0)),
                  pl.BlockSpec((1, t, 512), lambda i, j: (i, 0, 0)),
                  pl.BlockSpec((t, N_SEL_LANES), lambda i, j: (0, 0)),
                  pl.BlockSpec((1, 128, t), lambda i, j: (i, 0, 0))],
        out_specs=pl.BlockSpec((1, nq, H_A * DH_A), lambda i, j: (i, j, 0)),
        out_shape=jax.ShapeDtypeStruct((b, t, H_A * DH_A), BF16),
        scratch_shapes=[pltpu.VMEM((H_A // 2, 128, 2 * nq), F32)],
        compiler_params=_cparams("parallel", "arbitrary"),
        name="nsa_prompt",
    )(qpad, gates, pe, po, kvb, onehot, v_sel_t)


def _nsa_sample_body(pt_ref, q_ref, gt_ref, kvn_ref, w_ref, win_ref, *rest, n_pages, page, past, tq):
    del pt_ref
    cmp_pages, sel_pages = rest[:n_pages], rest[n_pages:2 * n_pages]
    o_ref, wout_ref, cmp_scr, sel_scr, win_scr = rest[2 * n_pages:]
    nq = q_ref.shape[1]
    npad = kvn_ref.shape[1]
    lp = cmp_scr.shape[0]
    wbuf = win_ref.shape[1]
    kvn = kvn_ref[0]
    for p in range(n_pages):
        cmp_scr[p * page:(p + 1) * page, :] = cmp_pages[p][0]
        sel_scr[p * page:(p + 1) * page, :] = sel_pages[p][0].astype(BF16)
    cmp_scr[past:past + npad, :] = kvn[:, 0:256]
    sel_scr[past:past + npad, :] = kvn[:, 256:512].astype(BF16)
    cmp_scr[past + npad:lp, :] = jnp.zeros((lp - past - npad, 256), F32)
    sel_scr[past + npad:lp, :] = jnp.zeros((lp - past - npad, 256), BF16)
    win_scr[0:wbuf, :] = win_ref[0]
    win_scr[wbuf:wbuf + npad, :] = kvn[:, 512:768]
    wl = win_scr.shape[0]
    win_scr[wbuf + npad:wl, :] = jnp.zeros((wl - wbuf - npad, 256), F32)
    wout_ref[0] = win_scr[pl.ds(tq, wbuf), :]

    nb = lp // SEL_BLOCK
    pe, po = _pool_blocks(cmp_scr[...].reshape(nb, SEL_BLOCK, 256), w_ref[...])
    zpad = jnp.zeros((N_SEL_LANES - nb, 256), F32)
    pe = jnp.concatenate([pe, zpad], axis=0)
    po = jnp.concatenate([po, zpad], axis=0)

    rowi = lax.broadcasted_iota(jnp.int32, (R_A * nq, 1), 0)
    qpos = past + (rowi & (nq - 1))
    nsel_keys = past + 256
    kpos_s = lax.broadcasted_iota(jnp.int32, (1, nsel_keys), 1)
    kpos_w = (past - wbuf) + lax.broadcasted_iota(jnp.int32, (1, wl), 1)
    heads_c, heads_s, heads_w = [], [], []
    qs, imps = [], []
    for g in range(G_A):
        q = jnp.concatenate([q_ref[0, :, (g * R_A + r) * 128:(g * R_A + r + 1) * 128] for r in range(R_A)], axis=0)
        q = (q * DH_A ** -0.5).astype(BF16)
        o_c, imp = _cmp_branch(q, qpos, pe, po, nq)
        heads_c += _split_heads(o_c, nq)
        qs.append(q)
        imps.append(imp)
    n_sel_q = G_A * nq
    imp_pad = jnp.concatenate(imps + [jnp.zeros((N_SEL_LANES - n_sel_q, N_SEL_LANES), F32)], axis=0)
    qpos_lane = past + (lax.broadcasted_iota(jnp.int32, (1, N_SEL_LANES), 1) & (nq - 1))
    sel_all = _select_blocks(imp_pad.T, qpos_lane, 0).T[0:n_sel_q]
    for g in range(G_A):
        q, sel = qs[g], sel_all[g * nq:(g + 1) * nq]
        ok = jnp.logical_and(_expand_sel(sel, 0, nsel_keys) > 0.5, kpos_s <= qpos)
        o_s = _masked_attend(q, sel_scr[0:nsel_keys, 0:128], sel_scr[0:nsel_keys, 128:256], ok)
        kw = win_scr[:, 0:128].astype(BF16)
        vw = win_scr[:, 128:256].astype(BF16)
        o_w = _masked_attend(q, kw, vw, _window_ok(qpos, kpos_w))
        heads_s += _split_heads(o_s, nq)
        heads_w += _split_heads(o_w, nq)
    for j, tile in enumerate(_gate_and_place(gt_ref[0], (heads_c, heads_s, heads_w))):
        o_ref[0, :, j * 128:(j + 1) * 128] = tile


def _nsa_sample(page_table, q8, gates8, kvn16, w, win, cmp_pages, sel_pages, tq):
    b, n_pages = page_table.shape
    page = cmp_pages.shape[1]
    past = n_pages * page
    wbuf = win.shape[1]
    lp = past + 512
    nq = q8.shape[1]

    def row(shape):
        return pl.BlockSpec((1,) + shape, lambda i, pt: (i, 0, 0))

    def page_spec(p):
        return pl.BlockSpec((1, page, 256), lambda i, pt, p=p: (pt[i, p], 0, 0))

    grid_spec = pltpu.PrefetchScalarGridSpec(
        num_scalar_prefetch=1,
        grid=(b,),
        in_specs=([row((nq, H_A * 128)), row((nq, 128)), row((kvn16.shape[1], 768)),
                   pl.BlockSpec((CMP_BLOCK, 256), lambda i, pt: (0, 0)), row((wbuf, 256))]
                  + [page_spec(p) for p in range(n_pages)] * 2),
        out_specs=[row((nq, H_A * DH_A)), row((wbuf, 256))],
        scratch_shapes=[pltpu.VMEM((lp, 256), F32), pltpu.VMEM((lp, 256), BF16),
                        pltpu.VMEM((wbuf + 128, 256), F32)],
    )
    return pl.pallas_call(
        functools.partial(_nsa_sample_body, n_pages=n_pages, page=page, past=past, tq=tq),
        grid_spec=grid_spec,
        out_shape=[jax.ShapeDtypeStruct((b, nq, H_A * DH_A), F32),
                   jax.ShapeDtypeStruct((b, wbuf, 256), F32)],
        compiler_params=_cparams("arbitrary"),
        name="nsa_sample",
    )(page_table, q8, gates8, kvn16, w, win, *([cmp_pages] * n_pages), *([sel_pages] * n_pages))


def _cast_transpose_body(x_ref, o_ref):
    o_ref[...] = x_ref[...].T.astype(o_ref.dtype)


def _cast_transpose(x, rows=512):
    r, c = x.shape
    return pl.pallas_call(
        _cast_transpose_body,
        grid=(r // rows,),
        in_specs=[pl.BlockSpec((rows, c), lambda i: (i, 0))],
        out_specs=pl.BlockSpec((c, rows), lambda i: (0, i)),
        out_shape=jax.ShapeDtypeStruct((c, r), BF16),
        compiler_params=_cparams("parallel"),
        name="cast_transpose",
    )(x)


def _comb_candidates(v1, v2):
    k, half, lanes = P_TOPK, P_TOPK // 2, v1.shape[1]
    assert k == 16
    qi = lax.broadcasted_iota(jnp.int32, (half, lanes), 0).astype(F32)
    sums = [v1[0:1] + v2]
    idx = [lax.broadcasted_iota(jnp.int32, (k, lanes), 0).astype(F32)]
    for p in range(1, half):
        sums.append(v1[p:p + 1] + v2[0:half])
        idx.append(qi + float(p * k))
    sums.append(v1[half:k] + v2[0:1])
    idx.append((qi + float(half)) * float(k))
    return jnp.concatenate(sums, axis=0), jnp.concatenate(idx, axis=0)


def _peer_select(s1s, s2s):
    k, half = P_TOPK, P_TOPK // 2
    nt = len(s1s)
    poss, vals = _topk_pos_multi_fast(list(s1s) + list(s2s), 0, k)
    combs = [_comb_candidates(vals[t], vals[nt + t]) for t in range(nt)]
    poscs, vcs = _topk_pos_multi_fast([cb[0] for cb in combs], 0, k, combs[0][1])
    out = []
    for t in range(nt):
        pos1, v1, pos2, v2 = poss[t], vals[t], poss[nt + t], vals[nt + t]
        z = jnp.sum(jnp.exp(vcs[t] - vcs[t][0:1]), axis=0, keepdims=True)
        sel = jnp.where(poscs[t] < k, 1.0, 0.0)
        n1 = jnp.where(pos1 == 0.0, jnp.sum(sel[0:k], axis=0, keepdims=True), 0.0)
        for p in range(1, half):
            r0 = k + half * (p - 1)
            n1 = jnp.where(pos1 == float(p), jnp.sum(sel[r0:r0 + half], axis=0, keepdims=True), n1)
        for p in range(half, k):
            r0 = k + half * (half - 1) + (p - half)
            n1 = jnp.where(pos1 == float(p), sel[r0:r0 + 1], n1)
        f1 = jnp.where(pos1 < k, jnp.exp(s1s[t] - v1[0:1]), 0.0) / z
        out.append((n1, f1, pos2, jnp.exp(s2s[t] - v2[0:1])))
    return out


def _peer_body(x_ref, g_ref, gf_ref, wq_ref, keys_ref, u_ref, vt_ref, o_ref,
               hnt_scr, n1_scr, f1_scr, q2_scr, e2_scr, acc_scr, h_even, h_odd, *, ec, final_norm):
    c = pl.program_id(1)
    tm = x_ref.shape[0]

    @pl.when(c == 0)
    def _select():
        hn = _rmsnorm(x_ref[...], g_ref[...])
        hnt_scr[...] = hn.T.astype(BF16)
        acc_scr[...] = jnp.zeros_like(acc_scr)

        def head(h, carry):
            r0 = pl.multiple_of(h * 2 * P_DHALF, 2 * P_DHALF)
            qt = _dot(wq_ref[pl.ds(r0, 2 * P_DHALF), :], hnt_scr[...]).astype(BF16)
            lane_tiles = [slice(lt * 128, (lt + 1) * 128) for lt in range(tm // 128)]
            s1s = [_dot(keys_ref[2 * h], qt[0:P_DHALF, sl]) for sl in lane_tiles]
            s2s = [_dot(keys_ref[2 * h + 1], qt[P_DHALF:2 * P_DHALF, sl]) for sl in lane_tiles]
            for sl, (n1, f1, q2, e2) in zip(lane_tiles, _peer_select(s1s, s2s)):
                n1_scr[h, :, sl] = n1
                f1_scr[h, :, sl] = f1
                q2_scr[h, :, sl] = q2.astype(BF16)
                e2_scr[h, :, sl] = e2.astype(BF16)
            return carry

        lax.fori_loop(0, P_HEADS, head, 0)

    nc = N_EXP // ec
    assert nc % 2 == 0

    def weighted_acts(h_out):
        act = _gelu(_dot(u_ref[...], hnt_scr[...]).astype(BF16))
        for ai in range(ec // N_KEYS):
            a = c * (ec // N_KEYS) + ai
            w = None
            for h in range(P_HEADS):
                n1 = n1_scr[h, pl.ds(a, 1), :].astype(BF16)
                f1 = f1_scr[h, pl.ds(a, 1), :].astype(BF16)
                term = jnp.where(q2_scr[h] < n1, e2_scr[h] * f1, jnp.zeros((), BF16))
                w = term if w is None else w + term
            rows = slice(ai * N_KEYS, (ai + 1) * N_KEYS)
            h_out[rows, :] = w * act[rows]

    @pl.when(c == 0)
    def _first():
        weighted_acts(h_even)

    @pl.when(jnp.logical_and(c < nc, c % 2 == 1))
    def _odd():
        acc_scr[...] += _dot(vt_ref[...], h_even[...])
        weighted_acts(h_odd)

    @pl.when(jnp.logical_and(jnp.logical_and(c > 0, c < nc), c % 2 == 0))
    def _even():
        acc_scr[...] += _dot(vt_ref[...], h_odd[...])
        weighted_acts(h_even)

    @pl.when(c == nc)
    def _finish():
        y = x_ref[...] + (acc_scr[...] + _dot(vt_ref[...], h_odd[...])).T
        if final_norm:
            y = _rmsnorm(y, gf_ref[...])
        o_ref[...] = y


def _peer(x, g, gf, wq_t, keys, u, vt, final_norm, tm=512, ec=1024):
    n, d = x.shape
    nc = N_EXP // ec
    const2 = lambda i, c: (0, 0)
    return pl.pallas_call(
        functools.partial(_peer_body, ec=ec, final_norm=final_norm),
        grid=(n // tm, nc + 1),
        in_specs=[pl.BlockSpec((tm, d), lambda i, c: (i, 0)),
                  pl.BlockSpec((1, d), const2), pl.BlockSpec((1, d), const2),
                  pl.BlockSpec(wq_t.shape, const2),
                  pl.BlockSpec(keys.shape, lambda i, c: (0, 0, 0)),
                  pl.BlockSpec((ec, d), lambda i, c: (jnp.minimum(c, nc - 1), 0)),
                  pl.BlockSpec((d, ec), lambda i, c: (0, jnp.maximum(c - 1, 0)))],
        out_specs=pl.BlockSpec((tm, d), lambda i, c: (i, 0)),
        out_shape=jax.ShapeDtypeStruct((n, d), F32),
        scratch_shapes=[pltpu.VMEM((d, tm), BF16)]
        + [pltpu.VMEM((P_HEADS, N_KEYS, tm), F32)] * 2
        + [pltpu.VMEM((P_HEADS, N_KEYS, tm), BF16)] * 2
        + [pltpu.VMEM((d, tm), F32), pltpu.VMEM((ec, tm), BF16), pltpu.VMEM((ec, tm), BF16)],
        compiler_params=_cparams("parallel", "arbitrary"),
        name="peer",
    )(x, g.reshape(1, d), gf.reshape(1, d), wq_t, keys, u, vt)


def _unit_lower_inverses(ms, order):
    ps = [-m for m in ms]
    invs = list(ps)
    span = 2
    while span < order:
        ps = [_dot_x3(p, p) for p in ps]
        invs = [inv + p + _dot_x3(inv, p) for inv, p in zip(invs, ps)]
        span *= 2
    row = lax.broadcasted_iota(jnp.int32, ms[0].shape, 0)
    col = lax.broadcasted_iota(jnp.int32, ms[0].shape, 1)
    eye = jnp.where(row == col, 1.0, 0.0)
    return [inv + eye for inv in invs]


def _gdn_body(qkv_ref, z_ref, ba_ref, cb_ref, s0_ref, wc_ref, prm_ref, gn_ref, o_ref, s_ref,
              xc_scr, s_scr, *, valid):
    t = pl.program_id(1)
    c = GDN_CHUNK
    r = qkv_ref.shape[1]

    @pl.when(t == 0)
    def _():
        xc_scr[0:8, :] = cb_ref[0]
        s_scr[...] = s0_ref[0]

    xc_scr[8:8 + r, :] = qkv_ref[0]
    if r < c:
        xc_scr[8 + r:8 + c, :] = jnp.zeros((c - r, QKV_C), F32)
    wc = wc_ref[...]
    conv = wc[0:1] * xc_scr[pl.ds(5, c), :]
    for j in range(1, CONV_W):
        conv = conv + wc[j:j + 1] * xc_scr[pl.ds(5 + j, c), :]
    xc_scr[0:8, :] = xc_scr[c:c + 8, :]
    act = _silu(conv)

    row = lax.broadcasted_iota(jnp.int32, (c, 1), 0)
    live = row < valid
    ba = ba_ref[0]
    if r < c:
        ba = jnp.concatenate([ba, jnp.zeros((c - r, 128), F32)], axis=0)
    prm = prm_ref[...]
    beta_all = jnp.where(live, jax.nn.sigmoid(ba), 0.0)
    g_all = jnp.where(live, prm[0:1] * _softplus(ba + prm[1:2]), 0.0)
    gc_all = g_all
    s = 1
    while s < c:
        gc_all = gc_all + _shift_rows(gc_all, s, 0.0)
        s *= 2
    gct = gc_all.T
    ri = lax.broadcasted_iota(jnp.int32, (c, c), 0)
    ci = lax.broadcasted_iota(jnp.int32, (c, c), 1)
    tri = ri >= ci
    strict = ri > ci
    z = z_ref[0]
    if r < c:
        z = jnp.concatenate([z, jnp.zeros((c - r, H_C * DV_C), F32)], axis=0)
    per_head = []
    for h in range(H_C):
        q = act[:, h * DK_C:(h + 1) * DK_C]
        k = act[:, (H_C + h) * DK_C:(H_C + h + 1) * DK_C]
        v = act[:, (2 * H_C + h) * DK_C:(2 * H_C + h + 1) * DK_C]
        q = q * lax.rsqrt(jnp.sum(q * q, -1, keepdims=True) + 1e-6) * (DK_C ** -0.5)
        k = k * lax.rsqrt(jnp.sum(k * k, -1, keepdims=True) + 1e-6)
        beta = beta_all[:, h:h + 1]
        gcol = gc_all[:, H_C + h:H_C + h + 1]
        grow = gct[H_C + h:H_C + h + 1, :]
        glast = gc_all[c - 1:c, H_C + h:H_C + h + 1]
        lmat = jnp.where(tri, jnp.exp(jnp.where(tri, gcol - grow, 0.0)), 0.0)
        kb = k * beta
        kbf = k.astype(BF16)
        m = jnp.where(strict, _dot_nt(kb.astype(BF16), kbf) * lmat, 0.0)
        eg = jnp.exp(gcol)
        aqk = (_dot_nt(q.astype(BF16), kbf) * lmat).astype(BF16)
        per_head.append(dict(m=m, vb=(v * beta).astype(BF16), kbe=(kb * eg).astype(BF16), aqk=aqk,
                             qd=(q * eg).astype(BF16), kdt=(k * jnp.exp(glast - gcol)).T.astype(BF16),
                             decay=jnp.exp(glast)))
    tinvs = _unit_lower_inverses([ph["m"] for ph in per_head], min(valid, c))
    for h, (ph, tinv) in enumerate(zip(per_head, tinvs)):
        tinv = tinv.astype(BF16)
        u = _dot(tinv, ph["vb"])
        w = _dot(tinv, ph["kbe"])
        st = s_scr[h]
        stb = st.astype(BF16)
        v_new = u - _dot(w.astype(BF16), stb)
        o = _dot(ph["qd"], stb) + _dot(ph["aqk"], v_new.astype(BF16))
        s_scr[h] = st * ph["decay"] + _dot(ph["kdt"], v_new.astype(BF16))
        o = o * lax.rsqrt(jnp.mean(o * o, -1, keepdims=True) + EPS) * gn_ref[...]
        o = o * _silu(z[:, h * DV_C:(h + 1) * DV_C])
        o_ref[0, :, h * DV_C:(h + 1) * DV_C] = o[0:r].astype(o_ref.dtype)

    @pl.when(t == pl.num_programs(1) - 1)
    def _():
        s_ref[0] = s_scr[...]


def _gdn(qkv, z, ba, cbuf8, s0, wc, prm, gn, valid):
    b, t, _ = qkv.shape
    r = min(t, GDN_CHUNK)
    tile = lambda wd: pl.BlockSpec((1, r, wd), lambda i, j: (i, j, 0))
    return pl.pallas_call(
        functools.partial(_gdn_body, valid=valid),
        grid=(b, t // r),
        in_specs=[tile(QKV_C), tile(H_C * DV_C), tile(128),
                  pl.BlockSpec((1, 8, QKV_C), lambda i, j: (i, 0, 0)),
                  pl.BlockSpec((1, H_C, DK_C, DV_C), lambda i, j: (i, 0, 0, 0)),
                  pl.BlockSpec((CONV_W, QKV_C), lambda i, j: (0, 0)),
                  pl.BlockSpec((8, 128), lambda i, j: (0, 0)),
                  pl.BlockSpec((1, DV_C), lambda i, j: (0, 0))],
        out_specs=[tile(H_C * DV_C),
                   pl.BlockSpec((1, H_C, DK_C, DV_C), lambda i, j: (i, 0, 0, 0))],
        out_shape=[jax.ShapeDtypeStruct((b, t, H_C * DV_C), F32 if r < GDN_CHUNK else BF16),
                   jax.ShapeDtypeStruct((b, H_C, DK_C, DV_C), F32)],
        scratch_shapes=[pltpu.VMEM((GDN_CHUNK + 8, QKV_C), F32), pltpu.VMEM((H_C, DK_C, DV_C), F32)],
        compiler_params=_cparams("parallel", "arbitrary"),
        name="gdn",
    )(qkv, z, ba, cbuf8, s0, wc, prm, gn)


def _block_diag(w):
    n, d, e = w.shape
    eye = jnp.eye(n, dtype=w.dtype)
    return (w[:, :, None, :] * eye[:, None, :, None]).reshape(n * d, n * e)


def _pad_rows(x, rows, front=0):
    return jnp.pad(x, ((0, 0), (front, rows - x.shape[1] - front), (0, 0)))


def _layer_a_weights(w_in, w_pool):
    d = w_in.shape[0]
    q_a, kv_a = H_A * DH_A, 3 * 2 * G_A * DH_A
    wq = w_in[:, :q_a].reshape(d, G_A, R_A, DH_A)
    zeros = jnp.zeros_like(wq)
    wq = jnp.stack([jnp.concatenate([wq[:, 0], zeros[:, 0]], -1),
                    jnp.concatenate([zeros[:, 1], wq[:, 1]], -1)], axis=1).reshape(d, H_A * 128)
    i1 = q_a + kv_a
    i2 = i1 + 3 * H_A
    wg = jnp.pad(w_in[:, i1:i2], ((0, 0), (0, 128 - 3 * H_A)))
    w = jnp.concatenate([wq, w_in[:, q_a:i1], wg, w_in[:, i2:]], axis=1).astype(BF16)
    wp = jnp.concatenate([jnp.broadcast_to(w_pool[0][:, None], (CMP_BLOCK, 128)),
                          jnp.broadcast_to(w_pool[1][:, None], (CMP_BLOCK, 128))], axis=1)
    return w, wp


def kernel(x_prompt, x_sample, cache_cmp_kv, cache_sel_kv, cache_win_kv, state_rg_h, state_rg_conv, state_gdn_S, state_gdn_conv, page_table, norm_mix, norm_ffn, norm_final, w_in_a, w_cmp_pool, w_rg_conv, b_rg_conv, w_rg_a, b_rg_a, w_rg_x, b_rg_x, rg_lambda, w_out_a, w_in_c, w_c_conv, c_A_log, c_dt_bias, c_norm, w_out_c, peer_w_q, peer_keys, peer_u, peer_v):
    bp, tp, d = x_prompt.shape
    bs, ts, _ = x_sample.shape
    np_, ns = bp * tp, bs * ts
    xp = x_prompt.reshape(np_, d)
    xs = x_sample.reshape(ns, d)
    kv_w = 3 * 2 * G_A * DH_A

    def peer_layer(x, layer, final):
        wq_t = _cast_transpose(peer_w_q[layer])
        keys = peer_keys[layer].reshape(2 * P_HEADS, N_KEYS, P_DHALF).astype(BF16)
        u = peer_u[layer].astype(BF16)
        vt = _cast_transpose(peer_v[layer])
        return [_peer(xx, norm_ffn[layer], norm_final, wq_t, keys, u, vt, final) for xx in x]

    w0, wp = _layer_a_weights(w_in_a[0], w_cmp_pool[0])
    qo, ko, go, ro = 0, H_A * 128, H_A * 128 + kv_w, H_A * 128 + kv_w + 128
    outs = ((qo, H_A * 128, None), (ko, kv_w, None), (ko + 256, 512, None), (go, 128, "sigmoid"),
            (ro, D_RNN, None), (ro + D_RNN, D_RNN, None))
    dts = (BF16, F32, BF16, F32, F32, F32)
    qp, kvp, kvbp, gtp, xrp, ygp = _norm_proj(xp, norm_mix[0], w0, outs, dts, 512)
    qs, kvs, _, gts, xrs, ygs = _norm_proj(xs, norm_mix[0], w0, outs, dts, min(512, ns))

    kvp3 = kvp.reshape(bp, tp, kv_w)
    pe, po = _nsa_pool(kvp3, wp)
    att_p = _nsa_prompt(qp.reshape(bp, tp, H_A * 128), gtp.reshape(bp, tp, 128), pe, po,
                        kvbp.reshape(bp, tp, 512)).reshape(np_, H_A * DH_A)

    kvs3 = kvs.reshape(bs, ts, kv_w)
    n_pool, page = cache_cmp_kv.shape[1], cache_cmp_kv.shape[2]
    att_s8, win_new = _nsa_sample(
        page_table,
        _pad_rows(qs.astype(F32).reshape(bs, ts, H_A * 128), 8),
        _pad_rows(gts.reshape(bs, ts, 128), 8),
        _pad_rows(kvs3, 16), wp,
        cache_win_kv[0].reshape(bs, -1, 256),
        cache_cmp_kv[0].reshape(n_pool, page, 256),
        cache_sel_kv[0].reshape(n_pool, page, 256), ts)
    att_s = att_s8[:, :ts].reshape(ns, H_A * DH_A).astype(BF16)

    rg = (w_rg_conv[0], b_rg_conv[0].reshape(1, D_RNN), _block_diag(w_rg_a[0]).astype(BF16),
          b_rg_a[0].reshape(1, D_RNN), _block_diag(w_rg_x[0]).astype(BF16), b_rg_x[0].reshape(1, D_RNN),
          rg_lambda[0].reshape(1, D_RNN))
    xrp3 = xrp.reshape(bp, tp, D_RNN)
    rec_p, hl_p = _rglru_prompt(xrp3, ygp.reshape(bp, tp, D_RNN), *rg)
    xrs3 = xrs.reshape(bs, ts, D_RNN)
    rec_s, hl_s = _rglru_sample(xrs3.transpose(1, 0, 2), ygs.reshape(bs, ts, D_RNN).transpose(1, 0, 2),
                                state_rg_conv[0].transpose(1, 0, 2), state_rg_h[0], *rg)
    rec_s = rec_s.transpose(1, 0, 2).reshape(ns, D_RNN).astype(BF16)

    wo = w_out_a[0].astype(BF16)
    wo_att, wo_rec = wo[:H_A * DH_A], wo[H_A * DH_A:]
    xp = _out_proj([att_p, rec_p.reshape(np_, D_RNN)], [wo_att, wo_rec], xp, 512)
    xs = _out_proj([att_s, rec_s], [wo_att, wo_rec], xs, min(512, ns))
    xp, xs = peer_layer([xp, xs], 0, False)

    d_c = H_C * DV_C
    w1 = jnp.pad(w_in_c[0], ((0, 0), (0, 128 - 2 * H_C))).astype(BF16)
    outs1 = ((0, QKV_C, None), (QKV_C, d_c, None), (QKV_C + d_c, 128, None))
    dts1 = (F32, F32, F32)
    qkv_p, z_p, ba_p = _norm_proj(xp, norm_mix[1], w1, outs1, dts1, 512)
    qkv_s, z_s, ba_s = _norm_proj(xs, norm_mix[1], w1, outs1, dts1, min(512, ns))
    lane = jnp.arange(128)
    hsel = jnp.clip(lane - H_C, 0, H_C - 1)
    in_g = (lane >= H_C) & (lane < 2 * H_C)
    prm = jnp.zeros((8, 128), F32)
    prm = prm.at[0].set(jnp.where(in_g, -jnp.exp(c_A_log[0])[hsel], 0.0))
    prm = prm.at[1].set(jnp.where(in_g, c_dt_bias[0][hsel], 0.0))
    gn = c_norm[0].reshape(1, DV_C)
    qkv_p3 = qkv_p.reshape(bp, tp, QKV_C)
    o_p, gs_p = _gdn(qkv_p3, z_p.reshape(bp, tp, d_c), ba_p.reshape(bp, tp, 128),
                     jnp.zeros((bp, 8, QKV_C), F32), jnp.zeros((bp, H_C, DK_C, DV_C), F32),
                     w_c_conv[0], prm, gn, GDN_CHUNK)
    qkv_s3 = qkv_s.reshape(bs, ts, QKV_C)
    o_s, gs_s = _gdn(_pad_rows(qkv_s3, 8), _pad_rows(z_s.reshape(bs, ts, d_c), 8),
                     _pad_rows(ba_s.reshape(bs, ts, 128), 8),
                     _pad_rows(state_gdn_conv[0], 8, front=8 - (CONV_W - 1)), state_gdn_S[0],
                     w_c_conv[0], prm, gn, ts)
    wo_c = w_out_c[0].astype(BF16)
    xp = _out_proj([o_p.reshape(np_, d_c)], [wo_c], xp, 512)
    xs = _out_proj([o_s[:, :ts].reshape(ns, d_c).astype(BF16)], [wo_c], xs, min(512, ns))
    yp, ys = peer_layer([xp, xs], 1, True)

    kvp6 = kvp3.reshape(bp, tp, 3, 2, G_A, DH_A)
    kvs6 = kvs3.reshape(bs, ts, 3, 2, G_A, DH_A)
    wlen = min(WINDOW, tp)
    cw = CONV_W - 1
    return (yp.reshape(bp, tp, d), ys.reshape(bs, ts, d),
            kvp6[None, :, :, 0], kvs6[None, :, :, 0], kvp6[None, :, :, 1], kvs6[None, :, :, 1],
            kvp6[None, :, tp - wlen:, 2], win_new.reshape(bs, -1, 2, G_A, DH_A)[None],
            hl_p[None], hl_s[None], xrp3[None, :, tp - cw:], xrs3[None, :, ts - cw:],
            gs_p[None], gs_s[None], qkv_p3[None, :, tp - cw:], qkv_s3[None, :, ts - cw:])
```

```python
import functools
import math

import jax
import jax.numpy as jnp
from jax import lax
from jax.experimental import pallas as pl
from jax.experimental.pallas import tpu as pltpu

F32 = jnp.float32
BF16 = jnp.bfloat16

D_MODEL = 1024
H_A, G_A, R_A, DH_A = 8, 2, 4, 64
CMP_BLOCK, SEL_BLOCK, TOP_N, WINDOW, Q_BLOCK = 32, 64, 16, 512, 128
FORCE_BONUS = 1.0e3
NEG_INF = -1.0e30
N_SEL_LANES = 128
D_RNN, RG_C, CONV_W = 512, 8.0, 4
H_C, DK_C, DV_C = 8, 128, 128
QKV_C = 3 * H_C * DK_C
GDN_CHUNK = 128
P_HEADS, N_KEYS, P_DHALF, P_TOPK = 8, 128, 128, 16
N_EXP = N_KEYS * N_KEYS
EPS = 1e-6

VMEM_LIMIT = 56 * 1024 * 1024
PROJ_ROWS = 512
RG_ROWS = 256
POOL_ROWS = 1024
PEER_TOKENS = 512
PEER_EXPERTS = 1024


def _cparams(*sem):
    return pltpu.CompilerParams(dimension_semantics=sem, vmem_limit_bytes=VMEM_LIMIT)


def _rmsnorm(x, g):
    return x * lax.rsqrt(jnp.mean(x * x, -1, keepdims=True) + EPS) * g


def _gelu(x):
    k = -2.0 * math.sqrt(2.0 / math.pi) * math.log2(math.e)
    return x / (1.0 + jnp.exp2(x * (k + (k * 0.044715) * (x * x))))


def _softplus(x):
    return jnp.maximum(x, 0.0) + jnp.log1p(jnp.exp(-jnp.abs(x)))


def _silu(x):
    return x * jax.nn.sigmoid(x)


def _neg_expm1(x):
    t = jnp.tanh(0.5 * x)
    return -2.0 * t / (1.0 - t)


def _dot(a, b):
    return jnp.dot(a, b, preferred_element_type=F32)


def _dot_nt(a, b):
    return lax.dot_general(a, b, (((1,), (1,)), ((), ())), preferred_element_type=F32)


def _dot_x3(a, b):
    a_hi = a.astype(BF16)
    b_hi = b.astype(BF16)
    a_lo = (a - a_hi.astype(F32)).astype(BF16)
    b_lo = (b - b_hi.astype(F32)).astype(BF16)
    return _dot(a_hi, b_hi) + (_dot(a_hi, b_lo) + _dot(a_lo, b_hi))


def _shift_rows(x, s, fill):
    rolled = pltpu.roll(x, s, 0)
    row = lax.broadcasted_iota(jnp.int32, x.shape, 0)
    return jnp.where(row >= s, rolled, fill)


def _topk_pos_multi(ss, axis, k, idx=None):
    shape = ss[0].shape
    if idx is None:
        idx = lax.broadcasted_iota(jnp.int32, shape, axis).astype(F32)
    vshape = tuple(k if d == axis else shape[d] for d in range(len(shape)))
    vidx = lax.broadcasted_iota(jnp.int32, vshape, axis)
    n = len(ss)
    ss = list(ss)
    poss = [jnp.full(shape, float(k), F32) for _ in range(n)]
    valss = [jnp.zeros(vshape, F32) for _ in range(n)]
    for i in range(k):
        ms = [jnp.max(s, axis=axis, keepdims=True) for s in ss]
        js = [jnp.min(jnp.where(s == m, idx, 1e9), axis=axis, keepdims=True) for s, m in zip(ss, ms)]
        ohs = [idx == j for j in js]
        poss = [jnp.where(oh, float(i), pos) for oh, pos in zip(ohs, poss)]
        ss = [jnp.where(oh, -jnp.inf, s) for oh, s in zip(ohs, ss)]
        valss = [jnp.where(vidx == i, m, vals) for m, vals in zip(ms, valss)]
    return poss, valss


def _topk_pos_multi_fast(ss, axis, k, idx=None):
    shape = ss[0].shape
    vshape = tuple(k if d == axis else shape[d] for d in range(len(shape)))
    vidx = lax.broadcasted_iota(jnp.int32, vshape, axis)
    n = len(ss)
    rest = list(ss)
    poss = [jnp.full(shape, float(k), F32) for _ in range(n)]
    valss = [jnp.zeros(vshape, F32) for _ in range(n)]
    for i in range(k):
        ms = [jnp.max(s, axis=axis, keepdims=True) for s in rest]
        ohs = [s == m for s, m in zip(rest, ms)]
        poss = [jnp.where(oh, float(i), pos) for oh, pos in zip(ohs, poss)]
        rest = [jnp.where(oh, -jnp.inf, s) for oh, s in zip(ohs, rest)]
        valss = [jnp.where(vidx == i, m, vals) for m, vals in zip(ms, valss)]
    removed = jnp.sum(sum(jnp.where(pos < k, 1.0, 0.0) for pos in poss))
    expected = float(n * k * math.prod(shape) // shape[axis])
    return lax.cond(removed == expected,
                    lambda: (tuple(poss), tuple(valss)),
                    lambda: tuple(tuple(x) for x in _topk_pos_multi(ss, axis, k, idx)))


def _norm_proj_body(x_ref, g_ref, w_ref, *out_refs, outs):
    h = _rmsnorm(x_ref[...], g_ref[...]).astype(BF16)
    for o_ref, (off, width, act) in zip(out_refs, outs):
        z = _dot(h, w_ref[:, off:off + width])
        if act == "sigmoid":
            z = jax.nn.sigmoid(z)
        o_ref[...] = z.astype(o_ref.dtype)


def _norm_proj(x, g, w, outs, dtypes):
    n, d = x.shape
    tm = min(PROJ_ROWS, n)
    body = functools.partial(_norm_proj_body, outs=outs)
    return pl.pallas_call(
        body,
        grid=(n // tm,),
        in_specs=[pl.BlockSpec((tm, d), lambda i: (i, 0)),
                  pl.BlockSpec((1, d), lambda i: (0, 0)),
                  pl.BlockSpec(w.shape, lambda i: (0, 0))],
        out_specs=[pl.BlockSpec((tm, wd), lambda i: (i, 0)) for (_, wd, _) in outs],
        out_shape=[jax.ShapeDtypeStruct((n, wd), dt) for (_, wd, _), dt in zip(outs, dtypes)],
        compiler_params=_cparams("parallel"),
        name="norm_proj",
    )(x, g.reshape(1, d), w)


def _out_proj_body(*refs, n_in):
    a_refs, w_refs = refs[:n_in], refs[n_in:2 * n_in]
    x_ref, o_ref = refs[2 * n_in], refs[2 * n_in + 1]
    acc = x_ref[...]
    for a, w in zip(a_refs, w_refs):
        acc = acc + _dot(a[...], w[...])
    o_ref[...] = acc


def _out_proj(acts, ws, x):
    n, d = x.shape
    tm = min(PROJ_ROWS, n)
    n_in = len(acts)
    return pl.pallas_call(
        functools.partial(_out_proj_body, n_in=n_in),
        grid=(n // tm,),
        in_specs=([pl.BlockSpec((tm, a.shape[1]), lambda i: (i, 0)) for a in acts]
                  + [pl.BlockSpec(w.shape, lambda i: (0, 0)) for w in ws]
                  + [pl.BlockSpec((tm, d), lambda i: (i, 0))]),
        out_specs=pl.BlockSpec((tm, d), lambda i: (i, 0)),
        out_shape=jax.ShapeDtypeStruct((n, d), F32),
        compiler_params=_cparams("parallel"),
        name="out_proj",
    )(*acts, *ws, x)


def _rg_gates(xf, wa, ba, wx, bx, lam):
    xb = xf.astype(BF16)
    r = jax.nn.sigmoid(_dot(xb, wa) + ba)
    i = jax.nn.sigmoid(_dot(xb, wx) + bx)
    log_a = -RG_C * r * _softplus(-lam)
    a = jnp.exp(log_a)
    b = jnp.sqrt(_neg_expm1(2.0 * log_a)) * (i * xf)
    return a, b


def _rglru_prompt_body(xr_ref, yg_ref, wc_ref, bc_ref, wa_ref, ba_ref, wx_ref, bx_ref, lam_ref,
                       rec_ref, hl_ref, xc_scr, h_scr, *, rt):
    t = pl.program_id(1)

    @pl.when(t == 0)
    def _():
        xc_scr[0:8, :] = jnp.zeros((8, D_RNN), F32)
        h_scr[...] = jnp.zeros_like(h_scr)

    xc_scr[8:8 + rt, :] = xr_ref[0]
    wc = wc_ref[...]
    conv = wc[0:1] * xc_scr[pl.ds(5, rt), :]
    for j in range(1, CONV_W):
        conv = conv + wc[j:j + 1] * xc_scr[pl.ds(5 + j, rt), :]
    xc_scr[0:8, :] = xc_scr[rt:rt + 8, :]
    xf = conv + bc_ref[...]
    a, b = _rg_gates(xf, wa_ref[...], ba_ref[...], wx_ref[...], bx_ref[...], lam_ref[...])
    s = 1
    while s < rt:
        b = a * _shift_rows(b, s, 0.0) + b
        a = a * _shift_rows(a, s, 1.0)
        s *= 2
    h = a * h_scr[0:1, :] + b
    h_last = h[rt - 1:rt, :]
    h_scr[0:1, :] = h_last
    hl_ref[0] = h_last
    rec_ref[0] = (h * _gelu(yg_ref[0])).astype(rec_ref.dtype)


def _rglru_prompt(xr, yg, wc, bc, wa, ba, wx, bx, lam, rt=RG_ROWS):
    b, t, d = xr.shape
    vec = lambda: pl.BlockSpec((1, d), lambda i, j: (0, 0))
    mat = lambda: pl.BlockSpec((d, d), lambda i, j: (0, 0))
    rec, hl = pl.pallas_call(
        functools.partial(_rglru_prompt_body, rt=rt),
        grid=(b, t // rt),
        in_specs=[pl.BlockSpec((1, rt, d), lambda i, j: (i, j, 0)),
                  pl.BlockSpec((1, rt, d), lambda i, j: (i, j, 0)),
                  pl.BlockSpec((CONV_W, d), lambda i, j: (0, 0)),
                  vec(), mat(), vec(), mat(), vec(), vec()],
        out_specs=[pl.BlockSpec((1, rt, d), lambda i, j: (i, j, 0)),
                   pl.BlockSpec((1, 1, d), lambda i, j: (i, 0, 0))],
        out_shape=[jax.ShapeDtypeStruct((b, t, d), BF16),
                   jax.ShapeDtypeStruct((b, 1, d), F32)],
        scratch_shapes=[pltpu.VMEM((rt + 8, d), F32), pltpu.VMEM((8, d), F32)],
        compiler_params=_cparams("arbitrary", "arbitrary"),
        name="rglru_prompt",
    )(xr, yg, wc, bc, wa, ba, wx, bx, lam)
    return rec, hl[:, 0]


def _rglru_sample_body(xr_ref, yg_ref, buf_ref, h0_ref, wc_ref, bc_ref, wa_ref, ba_ref, wx_ref,
                       bx_ref, lam_ref, rec_ref, hl_ref, *, t_len):
    xs = [buf_ref[j] for j in range(CONV_W - 1)] + [xr_ref[j] for j in range(t_len)]
    wc = wc_ref[...]
    h = h0_ref[...]
    for t in range(t_len):
        conv = wc[0:1] * xs[t]
        for j in range(1, CONV_W):
            conv = conv + wc[j:j + 1] * xs[t + j]
        xf = conv + bc_ref[...]
        a, b = _rg_gates(xf, wa_ref[...], ba_ref[...], wx_ref[...], bx_ref[...], lam_ref[...])
        h = a * h + b
        rec_ref[t] = h * _gelu(yg_ref[t])
    hl_ref[...] = h


def _rglru_sample(xr, yg, buf, h0, wc, bc, wa, ba, wx, bx, lam):
    t_len, b, d = xr.shape
    return pl.pallas_call(
        functools.partial(_rglru_sample_body, t_len=t_len),
        out_shape=[jax.ShapeDtypeStruct((t_len, b, d), F32),
                   jax.ShapeDtypeStruct((b, d), F32)],
        compiler_params=pltpu.CompilerParams(vmem_limit_bytes=VMEM_LIMIT),
        name="rglru_sample",
    )(xr, yg, buf, h0, wc, bc, wa, ba, wx, bx, lam)


def _pool_blocks(x3, w):
    first = jnp.sum(x3[:, :CMP_BLOCK, :] * w[None], axis=1)
    second = jnp.sum(x3[:, CMP_BLOCK:, :] * w[None], axis=1)
    return first, second


def _nsa_pool_body(x_ref, w_ref, pe_ref, po_ref):
    x = x_ref[0]
    nb = x.shape[0] // SEL_BLOCK
    pe, po = _pool_blocks(x.reshape(nb, SEL_BLOCK, x.shape[1]), w_ref[...])
    pe_ref[0] = pe
    po_ref[0] = po


def _nsa_pool(kv, w, rows=POOL_ROWS):
    b, t, _ = kv.shape
    nb = rows // SEL_BLOCK
    return pl.pallas_call(
        _nsa_pool_body,
        grid=(b, t // rows),
        in_specs=[pl.BlockSpec((1, rows, 256), lambda i, j: (i, j, 0)),
                  pl.BlockSpec((CMP_BLOCK, 256), lambda i, j: (0, 0))],
        out_specs=[pl.BlockSpec((1, nb, 256), lambda i, j: (i, j, 0))] * 2,
        out_shape=[jax.ShapeDtypeStruct((b, t // SEL_BLOCK, 256), F32)] * 2,
        compiler_params=_cparams("parallel", "parallel"),
        name="nsa_pool",
    )(kv, w)


def _cmp_branch(q, qpos, pe, po, nq):
    jl = lax.broadcasted_iota(jnp.int32, (1, N_SEL_LANES), 1)
    vis_e = (SEL_BLOCK * jl + (CMP_BLOCK - 1)) <= qpos
    vis_o = (SEL_BLOCK * jl + (SEL_BLOCK - 1)) <= qpos
    s_e = jnp.where(vis_e, _dot_nt(q, pe[:, :128].astype(BF16)), NEG_INF)
    s_o = jnp.where(vis_o, _dot_nt(q, po[:, :128].astype(BF16)), NEG_INF)
    m = jnp.maximum(jnp.max(s_e, -1, keepdims=True), jnp.max(s_o, -1, keepdims=True))
    p_e = jnp.where(vis_e, jnp.exp(s_e - m), 0.0)
    p_o = jnp.where(vis_o, jnp.exp(s_o - m), 0.0)
    den = jnp.sum(p_e, -1, keepdims=True) + jnp.sum(p_o, -1, keepdims=True)
    inv = jnp.where(den > 0.0, 1.0 / jnp.maximum(den, 1e-30), 0.0)
    p_e = p_e * inv
    p_o = p_o * inv
    o = _dot(p_e.astype(BF16), pe[:, 128:].astype(BF16)) + _dot(p_o.astype(BF16), po[:, 128:].astype(BF16))
    p = p_e + p_o
    imp = p[0:nq]
    for r in range(1, R_A):
        imp = imp + p[r * nq:(r + 1) * nq]
    return o, imp


def _select_blocks(imps, qpos, blk_axis):
    shape = (N_SEL_LANES, 1) if blk_axis == 0 else (1, N_SEL_LANES)
    blk = lax.broadcasted_iota(jnp.int32, shape, blk_axis)
    valid = blk * SEL_BLOCK <= qpos
    bonus = jnp.where(jnp.logical_or(blk == 0, blk == (qpos >> 6)), FORCE_BONUS, 0.0)
    scores = [jnp.where(valid, imp + bonus, -jnp.inf) for imp in imps]
    poss, _ = _topk_pos_multi(scores, blk_axis, TOP_N)
    return [jnp.where(valid, jnp.where(pos < TOP_N, 1.0, 0.0), 0.0) for pos in poss]


def _expand_sel(sel, first_blk, nkeys):
    bi = lax.broadcasted_iota(jnp.int32, (N_SEL_LANES, nkeys), 0)
    ki = lax.broadcasted_iota(jnp.int32, (N_SEL_LANES, nkeys), 1)
    e = jnp.where(bi - first_blk == (ki >> 6), 1.0, 0.0).astype(BF16)
    x = _dot(sel.astype(BF16), e)
    return jnp.concatenate([x] * R_A, axis=0)


def _masked_attend(q, k, v, ok):
    s = jnp.where(ok, _dot_nt(q, k), NEG_INF)
    m = jnp.max(s, -1, keepdims=True)
    p = jnp.where(ok, jnp.exp(s - m), 0.0)
    l = jnp.sum(p, -1, keepdims=True)
    return _dot(p.astype(BF16), v) / l


def _window_ok(qpos, kpos):
    d = qpos - kpos
    return jnp.logical_and(d >= 0, d <= WINDOW)


def _gate_and_place(gt, branches):
    heads = []
    for h in range(H_A):
        o = None
        for br in range(3):
            term = gt[:, 3 * h + br:3 * h + br + 1] * branches[br][h]
            o = term if o is None else o + term
        if (h % 2) != h // R_A:
            o = pltpu.roll(o, DH_A, 1)
        heads.append(o)
    lane = lax.broadcasted_iota(jnp.int32, (1, 128), 1)
    return [jnp.where(lane < DH_A, heads[2 * j], heads[2 * j + 1]) for j in range(H_A // 2)]


def _split_heads(o, nq):
    return [o[r * nq:(r + 1) * nq] for r in range(R_A)]


SEL_CHUNK = 512


def _nsa_prompt_body(q_ref, gt_ref, pe_ref, po_ref, kv_ref, oh_ref, vt_ref, o_ref, acc_scr):
    i = pl.program_id(1)
    nq = Q_BLOCK
    base = i * nq
    pe, po = pe_ref[0], po_ref[0]
    rowi = lax.broadcasted_iota(jnp.int32, (R_A * nq, 1), 0)
    qpos = base + (rowi & (nq - 1))
    qpos_row = base + lax.broadcasted_iota(jnp.int32, (1, nq), 1)
    heads_c, heads_w, qats, qs, imps_t = [], [], [], [], []
    for g in range(G_A):
        q = jnp.concatenate([q_ref[0, :, (g * R_A + r) * 128:(g * R_A + r + 1) * 128] for r in range(R_A)], axis=0)
        q = q * jnp.asarray(DH_A ** -0.5, BF16)
        o_c, imp = _cmp_branch(q, qpos, pe, po, nq)
        heads_c += _split_heads(o_c, nq)
        qs.append(q)
        imps_t.append(imp.T)
    sels_t = _select_blocks(imps_t, qpos_row, 0)
    for g in range(G_A):
        q = qs[g]
        selterm_t = jnp.where(sels_t[g] > 0.5, 0.0, NEG_INF).astype(BF16)
        qh_t = [jnp.concatenate([qh.astype(F32).T.astype(BF16), selterm_t], axis=0) for qh in _split_heads(q, nq)]
        qats += [jnp.concatenate(qh_t[2 * j:2 * j + 2], axis=1) for j in range(R_A // 2)]

        wlen = WINDOW + nq
        ws = pl.multiple_of(jnp.maximum(i - WINDOW // nq, 0) * nq, nq)
        kw = kv_ref[0, pl.ds(ws, wlen), 256:384]
        vw = kv_ref[0, pl.ds(ws, wlen), 384:512]
        kpos = ws + lax.broadcasted_iota(jnp.int32, (1, wlen), 1)
        heads_w += _split_heads(_masked_attend(q, kw, vw, _window_ok(qpos, kpos)), nq)

    acc_scr[...] = jnp.zeros_like(acc_scr)

    def step(k0, ms, ls, diagonal):
        ka = jnp.concatenate([kv_ref[0, pl.ds(k0, SEL_CHUNK), 0:128], oh_ref[pl.ds(k0, SEL_CHUNK), :]], axis=1)
        vt = vt_ref[0, :, pl.ds(k0, SEL_CHUNK)]
        if diagonal:
            qpos_pair = base + (lax.broadcasted_iota(jnp.int32, (1, 2 * nq), 1) & (nq - 1))
            visible = k0 + lax.broadcasted_iota(jnp.int32, (SEL_CHUNK, 1), 0) <= qpos_pair
        ss = [_dot(ka, qat) for qat in qats]
        if diagonal:
            ss = [jnp.where(visible, s, NEG_INF) for s in ss]
        ms_new = [jnp.maximum(m, jnp.max(s, axis=0, keepdims=True)) for m, s in zip(ms, ss)]
        ps = [jnp.exp(s - m) for s, m in zip(ss, ms_new)]
        pvs = [_dot(vt, p.astype(BF16)) for p in ps]
        ls_new = []
        for j in range(n_pairs):
            alpha = jnp.exp(ms[j] - ms_new[j])
            ls_new.append(alpha * ls[j] + jnp.sum(ps[j], axis=0, keepdims=True))
            acc_scr[j] = alpha * acc_scr[j] + pvs[j]
        return tuple(ms_new), tuple(ls_new)

    def past_chunk(c, carry):
        return step(pl.multiple_of(c * SEL_CHUNK, SEL_CHUNK), carry[0], carry[1], False)

    n_pairs = H_A // 2
    init = (tuple(jnp.full((1, 2 * nq), NEG_INF, F32) for _ in range(n_pairs)),
            tuple(jnp.zeros((1, 2 * nq), F32) for _ in range(n_pairs)))
    n_past = (base + nq - 1) // SEL_CHUNK
    ms, ls = lax.fori_loop(0, n_past, past_chunk, init)
    _, ls = step(pl.multiple_of(n_past * SEL_CHUNK, SEL_CHUNK), ms, ls, True)
    heads_s = []
    for j in range(n_pairs):
        o_pair = acc_scr[j] / ls[j]
        heads_s += [o_pair[:, 0:nq].T, o_pair[:, nq:2 * nq].T]
    for j, tile in enumerate(_gate_and_place(gt_ref[0], (heads_c, heads_s, heads_w))):
        o_ref[0, :, j * 128:(j + 1) * 128] = tile.astype(o_ref.dtype)


def _nsa_prompt(qpad, gates, pe, po, kvb):
    b, t, _ = qpad.shape
    nq = Q_BLOCK
    blk_of_key = jnp.arange(t, dtype=jnp.int32)[:, None] // SEL_BLOCK
    onehot = (blk_of_key == jnp.arange(N_SEL_LANES, dtype=jnp.int32)[None, :]).astype(BF16)
    v_sel_t = kvb[:, :, 128:256].transpose(0, 2, 1)
    return pl.pallas_call(
        _nsa_prompt_body,
        grid=(b, t // nq),
        in_specs=[pl.BlockSpec((1, nq, H_A * 128), lambda i, j: (i, j, 0)),
                  pl.BlockSpec((1, nq, 128), lambda i, j: (i, j, 0)),
                  pl.BlockSpec((1, N_SEL_LANES, 256), lambda i, j: (i, 0, 0)),
                  pl.BlockSpec((1, N_SEL_LANES, 256), lambda i, j: (i, 0, 0)),
                  pl.BlockSpec((1, t, 512), lambda i, j: (i, 0, 0)),
                  pl.BlockSpec((t, N_SEL_LANES), lambda i, j: (0, 0)),
                  pl.BlockSpec((1, 128, t), lambda i, j: (i, 0, 0))],
        out_specs=pl.BlockSpec((1, nq, H_A * DH_A), lambda i, j: (i, j, 0)),
        out_shape=jax.ShapeDtypeStruct((b, t, H_A * DH_A), BF16),
        scratch_shapes=[pltpu.VMEM((H_A // 2, 128, 2 * nq), F32)],
        compiler_params=_cparams("parallel", "arbitrary"),
        name="nsa_prompt",
    )(qpad, gates, pe, po, kvb, onehot, v_sel_t)


def _nsa_sample_body(pt_ref, q_ref, gt_ref, kvn_ref, w_ref, win_ref, *rest, n_pages, page, past, tq):
    del pt_ref
    cmp_pages, sel_pages = rest[:n_pages], rest[n_pages:2 * n_pages]
    o_ref, wout_ref, cmp_scr, sel_scr, win_scr = rest[2 * n_pages:]
    nq = q_ref.shape[1]
    npad = kvn_ref.shape[1]
    lp = cmp_scr.shape[0]
    wbuf = win_ref.shape[1]
    kvn = kvn_ref[0]
    for p in range(n_pages):
        cmp_scr[p * page:(p + 1) * page, :] = cmp_pages[p][0]
        sel_scr[p * page:(p + 1) * page, :] = sel_pages[p][0].astype(BF16)
    cmp_scr[past:past + npad, :] = kvn[:, 0:256]
    sel_scr[past:past + npad, :] = kvn[:, 256:512].astype(BF16)
    cmp_scr[past + npad:lp, :] = jnp.zeros((lp - past - npad, 256), F32)
    sel_scr[past + npad:lp, :] = jnp.zeros((lp - past - npad, 256), BF16)
    win_scr[0:wbuf, :] = win_ref[0]
    win_scr[wbuf:wbuf + npad, :] = kvn[:, 512:768]
    wl = win_scr.shape[0]
    win_scr[wbuf + npad:wl, :] = jnp.zeros((wl - wbuf - npad, 256), F32)
    wout_ref[0] = win_scr[pl.ds(tq, wbuf), :]

    nb = lp // SEL_BLOCK
    pe, po = _pool_blocks(cmp_scr[...].reshape(nb, SEL_BLOCK, 256), w_ref[...])
    zpad = jnp.zeros((N_SEL_LANES - nb, 256), F32)
    pe = jnp.concatenate([pe, zpad], axis=0)
    po = jnp.concatenate([po, zpad], axis=0)

    rowi = lax.broadcasted_iota(jnp.int32, (R_A * nq, 1), 0)
    qpos = past + (rowi & (nq - 1))
    nsel_keys = past + 256
    kpos_s = lax.broadcasted_iota(jnp.int32, (1, nsel_keys), 1)
    kpos_w = (past - wbuf) + lax.broadcasted_iota(jnp.int32, (1, wl), 1)
    heads_c, heads_s, heads_w = [], [], []
    qs, imps = [], []
    for g in range(G_A):
        q = jnp.concatenate([q_ref[0, :, (g * R_A + r) * 128:(g * R_A + r + 1) * 128] for r in range(R_A)], axis=0)
        q = (q * DH_A ** -0.5).astype(BF16)
        o_c, imp = _cmp_branch(q, qpos, pe, po, nq)
        heads_c += _split_heads(o_c, nq)
        qs.append(q)
        imps.append(imp)
    n_sel_q = G_A * nq
    imp_pad = jnp.concatenate(imps + [jnp.zeros((N_SEL_LANES - n_sel_q, N_SEL_LANES), F32)], axis=0)
    qpos_lane = past + (lax.broadcasted_iota(jnp.int32, (1, N_SEL_LANES), 1) & (nq - 1))
    sel_all = _select_blocks([imp_pad.T], qpos_lane, 0)[0].T[0:n_sel_q]
    for g in range(G_A):
        q, sel = qs[g], sel_all[g * nq:(g + 1) * nq]
        ok = jnp.logical_and(_expand_sel(sel, 0, nsel_keys) > 0.5, kpos_s <= qpos)
        o_s = _masked_attend(q, sel_scr[0:nsel_keys, 0:128], sel_scr[0:nsel_keys, 128:256], ok)
        kw = win_scr[:, 0:128].astype(BF16)
        vw = win_scr[:, 128:256].astype(BF16)
        o_w = _masked_attend(q, kw, vw, _window_ok(qpos, kpos_w))
        heads_s += _split_heads(o_s, nq)
        heads_w += _split_heads(o_w, nq)
    for j, tile in enumerate(_gate_and_place(gt_ref[0], (heads_c, heads_s, heads_w))):
        o_ref[0, :, j * 128:(j + 1) * 128] = tile


def _nsa_sample(page_table, q8, gates8, kvn16, w, win, cmp_pages, sel_pages, tq):
    b, n_pages = page_table.shape
    page = cmp_pages.shape[1]
    past = n_pages * page
    wbuf = win.shape[1]
    lp = past + 512
    nq = q8.shape[1]

    def row(shape):
        return pl.BlockSpec((1,) + shape, lambda i, pt: (i, 0, 0))

    def page_spec(p):
        return pl.BlockSpec((1, page, 256), lambda i, pt, p=p: (pt[i, p], 0, 0))

    grid_spec = pltpu.PrefetchScalarGridSpec(
        num_scalar_prefetch=1,
        grid=(b,),
        in_specs=([row((nq, H_A * 128)), row((nq, 128)), row((kvn16.shape[1], 768)),
                   pl.BlockSpec((CMP_BLOCK, 256), lambda i, pt: (0, 0)), row((wbuf, 256))]
                  + [page_spec(p) for p in range(n_pages)] * 2),
        out_specs=[row((nq, H_A * DH_A)), row((wbuf, 256))],
        scratch_shapes=[pltpu.VMEM((lp, 256), F32), pltpu.VMEM((lp, 256), BF16),
                        pltpu.VMEM((wbuf + 128, 256), F32)],
    )
    return pl.pallas_call(
        functools.partial(_nsa_sample_body, n_pages=n_pages, page=page, past=past, tq=tq),
        grid_spec=grid_spec,
        out_shape=[jax.ShapeDtypeStruct((b, nq, H_A * DH_A), F32),
                   jax.ShapeDtypeStruct((b, wbuf, 256), F32)],
        compiler_params=_cparams("arbitrary"),
        name="nsa_sample",
    )(page_table, q8, gates8, kvn16, w, win, *([cmp_pages] * n_pages), *([sel_pages] * n_pages))


def _cast_transpose_body(x_ref, o_ref):
    o_ref[...] = x_ref[...].T.astype(o_ref.dtype)


def _cast_transpose(x, rows=PROJ_ROWS):
    r, c = x.shape
    return pl.pallas_call(
        _cast_transpose_body,
        grid=(r // rows,),
        in_specs=[pl.BlockSpec((rows, c), lambda i: (i, 0))],
        out_specs=pl.BlockSpec((c, rows), lambda i: (0, i)),
        out_shape=jax.ShapeDtypeStruct((c, r), BF16),
        compiler_params=_cparams("parallel"),
        name="cast_transpose",
    )(x)


def _comb_candidates(v1, v2):
    k, half, lanes = P_TOPK, P_TOPK // 2, v1.shape[1]
    assert k == 16
    qi = lax.broadcasted_iota(jnp.int32, (half, lanes), 0).astype(F32)
    sums = [v1[0:1] + v2]
    idx = [lax.broadcasted_iota(jnp.int32, (k, lanes), 0).astype(F32)]
    for p in range(1, half):
        sums.append(v1[p:p + 1] + v2[0:half])
        idx.append(qi + float(p * k))
    sums.append(v1[half:k] + v2[0:1])
    idx.append((qi + float(half)) * float(k))
    return jnp.concatenate(sums, axis=0), jnp.concatenate(idx, axis=0)


def _peer_select(s1s, s2s):
    k, half = P_TOPK, P_TOPK // 2
    nt = len(s1s)
    poss, vals = _topk_pos_multi_fast(list(s1s) + list(s2s), 0, k)
    combs = [_comb_candidates(vals[t], vals[nt + t]) for t in range(nt)]
    poscs, vcs = _topk_pos_multi_fast([cb[0] for cb in combs], 0, k, combs[0][1])
    out = []
    for t in range(nt):
        pos1, v1, pos2, v2 = poss[t], vals[t], poss[nt + t], vals[nt + t]
        z = jnp.sum(jnp.exp(vcs[t] - vcs[t][0:1]), axis=0, keepdims=True)
        sel = jnp.where(poscs[t] < k, 1.0, 0.0)
        n1 = jnp.where(pos1 == 0.0, jnp.sum(sel[0:k], axis=0, keepdims=True), 0.0)
        for p in range(1, half):
            r0 = k + half * (p - 1)
            n1 = jnp.where(pos1 == float(p), jnp.sum(sel[r0:r0 + half], axis=0, keepdims=True), n1)
        for p in range(half, k):
            r0 = k + half * (half - 1) + (p - half)
            n1 = jnp.where(pos1 == float(p), sel[r0:r0 + 1], n1)
        f1 = jnp.where(pos1 < k, jnp.exp(s1s[t] - v1[0:1]), 0.0) / z
        out.append((n1, f1, pos2, jnp.exp(s2s[t] - v2[0:1])))
    return out


def _peer_body(x_ref, g_ref, gf_ref, wq_ref, keys_ref, u_ref, vt_ref, o_ref,
               hnt_scr, n1_scr, f1_scr, q2_scr, e2_scr, acc_scr, h_even, h_odd, *, ec, final_norm):
    c = pl.program_id(1)
    tm = x_ref.shape[0]

    @pl.when(c == 0)
    def _select():
        hn = _rmsnorm(x_ref[...], g_ref[...])
        hnt_scr[...] = hn.T.astype(BF16)
        acc_scr[...] = jnp.zeros_like(acc_scr)

        def head(h, carry):
            r0 = pl.multiple_of(h * 2 * P_DHALF, 2 * P_DHALF)
            qt = _dot(wq_ref[pl.ds(r0, 2 * P_DHALF), :], hnt_scr[...]).astype(BF16)
            lane_tiles = [slice(lt * 128, (lt + 1) * 128) for lt in range(tm // 128)]
            s1s = [_dot(keys_ref[2 * h], qt[0:P_DHALF, sl]) for sl in lane_tiles]
            s2s = [_dot(keys_ref[2 * h + 1], qt[P_DHALF:2 * P_DHALF, sl]) for sl in lane_tiles]
            for sl, (n1, f1, q2, e2) in zip(lane_tiles, _peer_select(s1s, s2s)):
                n1_scr[h, :, sl] = n1
                f1_scr[h, :, sl] = f1
                q2_scr[h, :, sl] = q2.astype(BF16)
                e2_scr[h, :, sl] = e2.astype(BF16)
            return carry

        lax.fori_loop(0, P_HEADS, head, 0)

    nc = N_EXP // ec
    assert nc % 2 == 0

    def weighted_acts(h_out):
        act = _gelu(_dot(u_ref[...], hnt_scr[...]).astype(BF16))
        for ai in range(ec // N_KEYS):
            a = c * (ec // N_KEYS) + ai
            w = None
            for h in range(P_HEADS):
                n1 = n1_scr[h, pl.ds(a, 1), :].astype(BF16)
                f1 = f1_scr[h, pl.ds(a, 1), :].astype(BF16)
                term = jnp.where(q2_scr[h] < n1, e2_scr[h] * f1, jnp.zeros((), BF16))
                w = term if w is None else w + term
            rows = slice(ai * N_KEYS, (ai + 1) * N_KEYS)
            h_out[rows, :] = w * act[rows]

    @pl.when(c == 0)
    def _first():
        weighted_acts(h_even)

    @pl.when(jnp.logical_and(c < nc, c % 2 == 1))
    def _odd():
        acc_scr[...] += _dot(vt_ref[...], h_even[...])
        weighted_acts(h_odd)

    @pl.when(jnp.logical_and(jnp.logical_and(c > 0, c < nc), c % 2 == 0))
    def _even():
        acc_scr[...] += _dot(vt_ref[...], h_odd[...])
        weighted_acts(h_even)

    @pl.when(c == nc)
    def _finish():
        y = x_ref[...] + (acc_scr[...] + _dot(vt_ref[...], h_odd[...])).T
        if final_norm:
            y = _rmsnorm(y, gf_ref[...])
        o_ref[...] = y


def _peer(x, g, gf, wq_t, keys, u, vt, final_norm, tm=PEER_TOKENS, ec=PEER_EXPERTS):
    n, d = x.shape
    nc = N_EXP // ec
    const2 = lambda i, c: (0, 0)
    return pl.pallas_call(
        functools.partial(_peer_body, ec=ec, final_norm=final_norm),
        grid=(n // tm, nc + 1),
        in_specs=[pl.BlockSpec((tm, d), lambda i, c: (i, 0)),
                  pl.BlockSpec((1, d), const2), pl.BlockSpec((1, d), const2),
                  pl.BlockSpec(wq_t.shape, const2),
                  pl.BlockSpec(keys.shape, lambda i, c: (0, 0, 0)),
                  pl.BlockSpec((ec, d), lambda i, c: (jnp.minimum(c, nc - 1), 0)),
                  pl.BlockSpec((d, ec), lambda i, c: (0, jnp.maximum(c - 1, 0)))],
        out_specs=pl.BlockSpec((tm, d), lambda i, c: (i, 0)),
        out_shape=jax.ShapeDtypeStruct((n, d), F32),
        scratch_shapes=[pltpu.VMEM((d, tm), BF16)]
        + [pltpu.VMEM((P_HEADS, N_KEYS, tm), F32)] * 2
        + [pltpu.VMEM((P_HEADS, N_KEYS, tm), BF16)] * 2
        + [pltpu.VMEM((d, tm), F32), pltpu.VMEM((ec, tm), BF16), pltpu.VMEM((ec, tm), BF16)],
        compiler_params=_cparams("parallel", "arbitrary"),
        name="peer",
    )(x, g.reshape(1, d), gf.reshape(1, d), wq_t, keys, u, vt)


def _unit_lower_inverses(ms, order):
    ps = [-m for m in ms]
    invs = list(ps)
    span = 2
    while span < order:
        ps = [_dot_x3(p, p) for p in ps]
        invs = [inv + p + _dot_x3(inv, p) for inv, p in zip(invs, ps)]
        span *= 2
    row = lax.broadcasted_iota(jnp.int32, ms[0].shape, 0)
    col = lax.broadcasted_iota(jnp.int32, ms[0].shape, 1)
    eye = jnp.where(row == col, 1.0, 0.0)
    return [inv + eye for inv in invs]


def _gdn_body(qkv_ref, z_ref, ba_ref, cb_ref, s0_ref, wc_ref, prm_ref, gn_ref, o_ref, s_ref,
              xc_scr, s_scr, *, valid):
    t = pl.program_id(1)
    c = GDN_CHUNK
    r = qkv_ref.shape[1]

    @pl.when(t == 0)
    def _():
        xc_scr[0:8, :] = cb_ref[0]
        s_scr[...] = s0_ref[0]

    xc_scr[8:8 + r, :] = qkv_ref[0]
    if r < c:
        xc_scr[8 + r:8 + c, :] = jnp.zeros((c - r, QKV_C), F32)
    wc = wc_ref[...]
    rc = c if r == c else r + 8
    conv = wc[0:1] * xc_scr[pl.ds(5, rc), :]
    for j in range(1, CONV_W):
        conv = conv + wc[j:j + 1] * xc_scr[pl.ds(5 + j, rc), :]
    xc_scr[0:8, :] = xc_scr[c:c + 8, :]
    act = _silu(conv)
    if rc < c:
        act = jnp.concatenate([act, jnp.zeros((c - rc, QKV_C), F32)], axis=0)

    row = lax.broadcasted_iota(jnp.int32, (c, 1), 0)
    live = row < valid
    ba = ba_ref[0]
    if r < c:
        ba = jnp.concatenate([ba, jnp.zeros((c - r, 128), F32)], axis=0)
    prm = prm_ref[...]
    beta_all = jnp.where(live, jax.nn.sigmoid(ba), 0.0)
    g_all = jnp.where(live, prm[0:1] * _softplus(ba + prm[1:2]), 0.0)
    gc_all = g_all
    s = 1
    while s < c:
        gc_all = gc_all + _shift_rows(gc_all, s, 0.0)
        s *= 2
    gct = gc_all.T
    ri = lax.broadcasted_iota(jnp.int32, (c, c), 0)
    ci = lax.broadcasted_iota(jnp.int32, (c, c), 1)
    tri = ri >= ci
    strict = ri > ci
    z = z_ref[0]
    if r < c:
        z = jnp.concatenate([z, jnp.zeros((c - r, H_C * DV_C), F32)], axis=0)
    per_head = []
    for h in range(H_C):
        q = act[:, h * DK_C:(h + 1) * DK_C]
        k = act[:, (H_C + h) * DK_C:(H_C + h + 1) * DK_C]
        v = act[:, (2 * H_C + h) * DK_C:(2 * H_C + h + 1) * DK_C]
        q = q * lax.rsqrt(jnp.sum(q * q, -1, keepdims=True) + 1e-6) * (DK_C ** -0.5)
        k = k * lax.rsqrt(jnp.sum(k * k, -1, keepdims=True) + 1e-6)
        beta = beta_all[:, h:h + 1]
        gcol = gc_all[:, H_C + h:H_C + h + 1]
        grow = gct[H_C + h:H_C + h + 1, :]
        glast = gc_all[c - 1:c, H_C + h:H_C + h + 1]
        lmat = jnp.where(tri, jnp.exp(jnp.where(tri, gcol - grow, 0.0)), 0.0)
        kb = k * beta
        kbf = k.astype(BF16)
        m = jnp.where(strict, _dot_nt(kb.astype(BF16), kbf) * lmat, 0.0)
        eg = jnp.exp(gcol)
        aqk = (_dot_nt(q.astype(BF16), kbf) * lmat).astype(BF16)
        per_head.append(dict(m=m, vb=(v * beta).astype(BF16), kbe=(kb * eg).astype(BF16), aqk=aqk,
                             qd=(q * eg).astype(BF16), kdt=(k * jnp.exp(glast - gcol)).T.astype(BF16),
                             decay=jnp.exp(glast)))
    tinvs = _unit_lower_inverses([ph["m"] for ph in per_head], min(valid, c))
    for h, (ph, tinv) in enumerate(zip(per_head, tinvs)):
        tinv = tinv.astype(BF16)
        u = _dot(tinv, ph["vb"])
        w = _dot(tinv, ph["kbe"])
        st = s_scr[h]
        stb = st.astype(BF16)
        v_new = u - _dot(w.astype(BF16), stb)
        o = _dot(ph["qd"], stb) + _dot(ph["aqk"], v_new.astype(BF16))
        s_scr[h] = st * ph["decay"] + _dot(ph["kdt"], v_new.astype(BF16))
        o = o * lax.rsqrt(jnp.mean(o * o, -1, keepdims=True) + EPS) * gn_ref[...]
        o = o * _silu(z[:, h * DV_C:(h + 1) * DV_C])
        o_ref[0, :, h * DV_C:(h + 1) * DV_C] = o[0:r].astype(o_ref.dtype)

    @pl.when(t == pl.num_programs(1) - 1)
    def _():
        s_ref[0] = s_scr[...]


def _gdn(qkv, z, ba, cbuf8, s0, wc, prm, gn, valid):
    b, t, _ = qkv.shape
    r = min(t, GDN_CHUNK)
    tile = lambda wd: pl.BlockSpec((1, r, wd), lambda i, j: (i, j, 0))
    return pl.pallas_call(
        functools.partial(_gdn_body, valid=valid),
        grid=(b, t // r),
        in_specs=[tile(QKV_C), tile(H_C * DV_C), tile(128),
                  pl.BlockSpec((1, 8, QKV_C), lambda i, j: (i, 0, 0)),
                  pl.BlockSpec((1, H_C, DK_C, DV_C), lambda i, j: (i, 0, 0, 0)),
                  pl.BlockSpec((CONV_W, QKV_C), lambda i, j: (0, 0)),
                  pl.BlockSpec((8, 128), lambda i, j: (0, 0)),
                  pl.BlockSpec((1, DV_C), lambda i, j: (0, 0))],
        out_specs=[tile(H_C * DV_C),
                   pl.BlockSpec((1, H_C, DK_C, DV_C), lambda i, j: (i, 0, 0, 0))],
        out_shape=[jax.ShapeDtypeStruct((b, t, H_C * DV_C), F32 if r < GDN_CHUNK else BF16),
                   jax.ShapeDtypeStruct((b, H_C, DK_C, DV_C), F32)],
        scratch_shapes=[pltpu.VMEM((GDN_CHUNK + 8, QKV_C), F32), pltpu.VMEM((H_C, DK_C, DV_C), F32)],
        compiler_params=_cparams("parallel", "arbitrary"),
        name="gdn",
    )(qkv, z, ba, cbuf8, s0, wc, prm, gn)


def _block_diag(w):
    n, d, e = w.shape
    eye = jnp.eye(n, dtype=w.dtype)
    return (w[:, :, None, :] * eye[:, None, :, None]).reshape(n * d, n * e)


def _pad_rows(x, rows, front=0):
    return jnp.pad(x, ((0, 0), (front, rows - x.shape[1] - front), (0, 0)))


def _layer_a_weights(w_in, w_pool):
    d = w_in.shape[0]
    q_a, kv_a = H_A * DH_A, 3 * 2 * G_A * DH_A
    wq = w_in[:, :q_a].reshape(d, G_A, R_A, DH_A)
    zeros = jnp.zeros_like(wq)
    wq = jnp.stack([jnp.concatenate([wq[:, 0], zeros[:, 0]], -1),
                    jnp.concatenate([zeros[:, 1], wq[:, 1]], -1)], axis=1).reshape(d, H_A * 128)
    i1 = q_a + kv_a
    i2 = i1 + 3 * H_A
    wg = jnp.pad(w_in[:, i1:i2], ((0, 0), (0, 128 - 3 * H_A)))
    w = jnp.concatenate([wq, w_in[:, q_a:i1], wg, w_in[:, i2:]], axis=1).astype(BF16)
    wp = jnp.concatenate([jnp.broadcast_to(w_pool[0][:, None], (CMP_BLOCK, 128)),
                          jnp.broadcast_to(w_pool[1][:, None], (CMP_BLOCK, 128))], axis=1)
    return w, wp


def kernel(x_prompt, x_sample, cache_cmp_kv, cache_sel_kv, cache_win_kv, state_rg_h, state_rg_conv, state_gdn_S, state_gdn_conv, page_table, norm_mix, norm_ffn, norm_final, w_in_a, w_cmp_pool, w_rg_conv, b_rg_conv, w_rg_a, b_rg_a, w_rg_x, b_rg_x, rg_lambda, w_out_a, w_in_c, w_c_conv, c_A_log, c_dt_bias, c_norm, w_out_c, peer_w_q, peer_keys, peer_u, peer_v):
    bp, tp, d = x_prompt.shape
    bs, ts, _ = x_sample.shape
    np_, ns = bp * tp, bs * ts
    xp = x_prompt.reshape(np_, d)
    xs = x_sample.reshape(ns, d)
    kv_w = 3 * 2 * G_A * DH_A

    def peer_layer(x, layer, final):
        wq_t = _cast_transpose(peer_w_q[layer])
        keys = peer_keys[layer].reshape(2 * P_HEADS, N_KEYS, P_DHALF).astype(BF16)
        u = peer_u[layer].astype(BF16)
        vt = _cast_transpose(peer_v[layer])
        return [_peer(xx, norm_ffn[layer], norm_final, wq_t, keys, u, vt, final) for xx in x]

    w0, wp = _layer_a_weights(w_in_a[0], w_cmp_pool[0])
    qo, ko, go, ro = 0, H_A * 128, H_A * 128 + kv_w, H_A * 128 + kv_w + 128
    outs = ((qo, H_A * 128, None), (ko, kv_w, None), (ko + 256, 512, None), (go, 128, "sigmoid"),
            (ro, D_RNN, None), (ro + D_RNN, D_RNN, None))
    dts = (BF16, F32, BF16, F32, F32, F32)
    qp, kvp, kvbp, gtp, xrp, ygp = _norm_proj(xp, norm_mix[0], w0, outs, dts)
    qs, kvs, _, gts, xrs, ygs = _norm_proj(xs, norm_mix[0], w0, outs, dts)

    kvp3 = kvp.reshape(bp, tp, kv_w)
    pe, po = _nsa_pool(kvp3, wp)
    att_p = _nsa_prompt(qp.reshape(bp, tp, H_A * 128), gtp.reshape(bp, tp, 128), pe, po,
                        kvbp.reshape(bp, tp, 512)).reshape(np_, H_A * DH_A)

    kvs3 = kvs.reshape(bs, ts, kv_w)
    n_pool, page = cache_cmp_kv.shape[1], cache_cmp_kv.shape[2]
    att_s8, win_new = _nsa_sample(
        page_table,
        _pad_rows(qs.astype(F32).reshape(bs, ts, H_A * 128), 8),
        _pad_rows(gts.reshape(bs, ts, 128), 8),
        _pad_rows(kvs3, 16), wp,
        cache_win_kv[0].reshape(bs, -1, 256),
        cache_cmp_kv[0].reshape(n_pool, page, 256),
        cache_sel_kv[0].reshape(n_pool, page, 256), ts)
    att_s = att_s8[:, :ts].reshape(ns, H_A * DH_A).astype(BF16)

    rg = (w_rg_conv[0], b_rg_conv[0].reshape(1, D_RNN), _block_diag(w_rg_a[0]).astype(BF16),
          b_rg_a[0].reshape(1, D_RNN), _block_diag(w_rg_x[0]).astype(BF16), b_rg_x[0].reshape(1, D_RNN),
          rg_lambda[0].reshape(1, D_RNN))
    xrp3 = xrp.reshape(bp, tp, D_RNN)
    rec_p, hl_p = _rglru_prompt(xrp3, ygp.reshape(bp, tp, D_RNN), *rg)
    xrs3 = xrs.reshape(bs, ts, D_RNN)
    rec_s, hl_s = _rglru_sample(xrs3.transpose(1, 0, 2), ygs.reshape(bs, ts, D_RNN).transpose(1, 0, 2),
                                state_rg_conv[0].transpose(1, 0, 2), state_rg_h[0], *rg)
    rec_s = rec_s.transpose(1, 0, 2).reshape(ns, D_RNN).astype(BF16)

    wo = w_out_a[0].astype(BF16)
    wo_att, wo_rec = wo[:H_A * DH_A], wo[H_A * DH_A:]
    xp = _out_proj([att_p, rec_p.reshape(np_, D_RNN)], [wo_att, wo_rec], xp)
    xs = _out_proj([att_s, rec_s], [wo_att, wo_rec], xs)
    xp, xs = peer_layer([xp, xs], 0, False)

    d_c = H_C * DV_C
    w1 = jnp.pad(w_in_c[0], ((0, 0), (0, 128 - 2 * H_C))).astype(BF16)
    outs1 = ((0, QKV_C, None), (QKV_C, d_c, None), (QKV_C + d_c, 128, None))
    dts1 = (F32, F32, F32)
    qkv_p, z_p, ba_p = _norm_proj(xp, norm_mix[1], w1, outs1, dts1)
    qkv_s, z_s, ba_s = _norm_proj(xs, norm_mix[1], w1, outs1, dts1)
    lane = jnp.arange(128)
    hsel = jnp.clip(lane - H_C, 0, H_C - 1)
    in_g = (lane >= H_C) & (lane < 2 * H_C)
    prm = jnp.zeros((8, 128), F32)
    prm = prm.at[0].set(jnp.where(in_g, -jnp.exp(c_A_log[0])[hsel], 0.0))
    prm = prm.at[1].set(jnp.where(in_g, c_dt_bias[0][hsel], 0.0))
    gn = c_norm[0].reshape(1, DV_C)
    qkv_p3 = qkv_p.reshape(bp, tp, QKV_C)
    o_p, gs_p = _gdn(qkv_p3, z_p.reshape(bp, tp, d_c), ba_p.reshape(bp, tp, 128),
                     jnp.zeros((bp, 8, QKV_C), F32), jnp.zeros((bp, H_C, DK_C, DV_C), F32),
                     w_c_conv[0], prm, gn, GDN_CHUNK)
    qkv_s3 = qkv_s.reshape(bs, ts, QKV_C)
    o_s, gs_s = _gdn(_pad_rows(qkv_s3, 8), _pad_rows(z_s.reshape(bs, ts, d_c), 8),
                     _pad_rows(ba_s.reshape(bs, ts, 128), 8),
                     _pad_rows(state_gdn_conv[0], 8, front=8 - (CONV_W - 1)), state_gdn_S[0],
                     w_c_conv[0], prm, gn, ts)
    wo_c = w_out_c[0].astype(BF16)
    xp = _out_proj([o_p.reshape(np_, d_c)], [wo_c], xp)
    xs = _out_proj([o_s[:, :ts].reshape(ns, d_c).astype(BF16)], [wo_c], xs)
    yp, ys = peer_layer([xp, xs], 1, True)

    kvp6 = kvp3.reshape(bp, tp, 3, 2, G_A, DH_A)
    kvs6 = kvs3.reshape(bs, ts, 3, 2, G_A, DH_A)
    wlen = min(WINDOW, tp)
    cw = CONV_W - 1
    return (yp.reshape(bp, tp, d), ys.reshape(bs, ts, d),
            kvp6[None, :, :, 0], kvs6[None, :, :, 0], kvp6[None, :, :, 1], kvs6[None, :, :, 1],
            kvp6[None, :, tp - wlen:, 2], win_new.reshape(bs, -1, 2, G_A, DH_A)[None],
            hl_p[None], hl_s[None], xrp3[None, :, tp - cw:], xrs3[None, :, ts - cw:],
            gs_p[None], gs_s[None], qkv_p3[None, :, tp - cw:], qkv_s3[None, :, ts - cw:])
```

```python
import functools
import math

import jax
import jax.numpy as jnp
from jax import lax
from jax.experimental import pallas as pl
from jax.experimental.pallas import tpu as pltpu

F32 = jnp.float32
BF16 = jnp.bfloat16

D_MODEL = 1024
H_A, G_A, R_A, DH_A = 8, 2, 4, 64
CMP_BLOCK, SEL_BLOCK, TOP_N, WINDOW, Q_BLOCK = 32, 64, 16, 512, 128
FORCE_BONUS = 1.0e3
NEG_INF = -1.0e30
N_SEL_LANES = 128
D_RNN, RG_C, CONV_W = 512, 8.0, 4
H_C, DK_C, DV_C = 8, 128, 128
QKV_C = 3 * H_C * DK_C
GDN_CHUNK = 128
P_HEADS, N_KEYS, P_DHALF, P_TOPK = 8, 128, 128, 16
N_EXP = N_KEYS * N_KEYS
EPS = 1e-6

VMEM_LIMIT = 56 * 1024 * 1024
PROJ_ROWS = 512
RG_ROWS = 256
POOL_ROWS = 1024
PEER_TOKENS = 512
PEER_EXPERTS = 1024


def _cparams(*sem):
    return pltpu.CompilerParams(dimension_semantics=sem, vmem_limit_bytes=VMEM_LIMIT)


def _rmsnorm(x, g):
    return x * lax.rsqrt(jnp.mean(x * x, -1, keepdims=True) + EPS) * g


def _gelu(x):
    k = -2.0 * math.sqrt(2.0 / math.pi) * math.log2(math.e)
    return x / (1.0 + jnp.exp2(x * (k + (k * 0.044715) * (x * x))))


def _softplus(x):
    return jnp.maximum(x, 0.0) + jnp.log1p(jnp.exp(-jnp.abs(x)))


def _silu(x):
    return x * jax.nn.sigmoid(x)


def _neg_expm1(x):
    t = jnp.tanh(0.5 * x)
    return -2.0 * t / (1.0 - t)


def _dot(a, b):
    return jnp.dot(a, b, preferred_element_type=F32)


def _dot_nt(a, b):
    return lax.dot_general(a, b, (((1,), (1,)), ((), ())), preferred_element_type=F32)


def _dot_x3(a, b):
    a_hi = a.astype(BF16)
    b_hi = b.astype(BF16)
    a_lo = (a - a_hi.astype(F32)).astype(BF16)
    b_lo = (b - b_hi.astype(F32)).astype(BF16)
    return _dot(a_hi, b_hi) + (_dot(a_hi, b_lo) + _dot(a_lo, b_hi))


def _shift_rows(x, s, fill):
    rolled = pltpu.roll(x, s, 0)
    row = lax.broadcasted_iota(jnp.int32, x.shape, 0)
    return jnp.where(row >= s, rolled, fill)


def _topk_pos_multi(ss, axis, k, idx=None):
    shape = ss[0].shape
    if idx is None:
        idx = lax.broadcasted_iota(jnp.int32, shape, axis).astype(F32)
    vshape = tuple(k if d == axis else shape[d] for d in range(len(shape)))
    vidx = lax.broadcasted_iota(jnp.int32, vshape, axis)
    n = len(ss)
    ss = list(ss)
    poss = [jnp.full(shape, float(k), F32) for _ in range(n)]
    valss = [jnp.zeros(vshape, F32) for _ in range(n)]
    for i in range(k):
        ms = [jnp.max(s, axis=axis, keepdims=True) for s in ss]
        js = [jnp.min(jnp.where(s == m, idx, 1e9), axis=axis, keepdims=True) for s, m in zip(ss, ms)]
        ohs = [idx == j for j in js]
        poss = [jnp.where(oh, float(i), pos) for oh, pos in zip(ohs, poss)]
        ss = [jnp.where(oh, -jnp.inf, s) for oh, s in zip(ohs, ss)]
        valss = [jnp.where(vidx == i, m, vals) for m, vals in zip(ms, valss)]
    return poss, valss


def _topk_pos_multi_untied(ss, axis, k):
    shape = ss[0].shape
    vshape = tuple(k if d == axis else shape[d] for d in range(len(shape)))
    vidx = lax.broadcasted_iota(jnp.int32, vshape, axis)
    n = len(ss)
    rest = list(ss)
    poss = [jnp.full(shape, float(k), F32) for _ in range(n)]
    valss = [jnp.zeros(vshape, F32) for _ in range(n)]
    for i in range(k):
        ms = [jnp.max(s, axis=axis, keepdims=True) for s in rest]
        ohs = [s == m for s, m in zip(rest, ms)]
        poss = [jnp.where(oh, float(i), pos) for oh, pos in zip(ohs, poss)]
        rest = [jnp.where(oh, -jnp.inf, s) for oh, s in zip(ohs, rest)]
        valss = [jnp.where(vidx == i, m, vals) for m, vals in zip(ms, valss)]
    removed = jnp.sum(sum(jnp.where(pos < k, 1.0, 0.0) for pos in poss))
    expected = float(n * k * math.prod(shape) // shape[axis])
    return poss, valss, removed - expected


def _norm_proj_body(x_ref, g_ref, w_ref, *out_refs, outs):
    h = _rmsnorm(x_ref[...], g_ref[...]).astype(BF16)
    for o_ref, (off, width, act) in zip(out_refs, outs):
        z = _dot(h, w_ref[:, off:off + width])
        if act == "sigmoid":
            z = jax.nn.sigmoid(z)
        o_ref[...] = z.astype(o_ref.dtype)


def _norm_proj(x, g, w, outs, dtypes):
    n, d = x.shape
    tm = min(PROJ_ROWS, n)
    body = functools.partial(_norm_proj_body, outs=outs)
    return pl.pallas_call(
        body,
        grid=(n // tm,),
        in_specs=[pl.BlockSpec((tm, d), lambda i: (i, 0)),
                  pl.BlockSpec((1, d), lambda i: (0, 0)),
                  pl.BlockSpec(w.shape, lambda i: (0, 0))],
        out_specs=[pl.BlockSpec((tm, wd), lambda i: (i, 0)) for (_, wd, _) in outs],
        out_shape=[jax.ShapeDtypeStruct((n, wd), dt) for (_, wd, _), dt in zip(outs, dtypes)],
        compiler_params=_cparams("parallel"),
        name="norm_proj",
    )(x, g.reshape(1, d), w)


def _out_proj_body(*refs, n_in):
    a_refs, w_refs = refs[:n_in], refs[n_in:2 * n_in]
    x_ref, o_ref = refs[2 * n_in], refs[2 * n_in + 1]
    acc = x_ref[...]
    for a, w in zip(a_refs, w_refs):
        acc = acc + _dot(a[...], w[...])
    o_ref[...] = acc


def _out_proj(acts, ws, x):
    n, d = x.shape
    tm = min(PROJ_ROWS, n)
    n_in = len(acts)
    return pl.pallas_call(
        functools.partial(_out_proj_body, n_in=n_in),
        grid=(n // tm,),
        in_specs=([pl.BlockSpec((tm, a.shape[1]), lambda i: (i, 0)) for a in acts]
                  + [pl.BlockSpec(w.shape, lambda i: (0, 0)) for w in ws]
                  + [pl.BlockSpec((tm, d), lambda i: (i, 0))]),
        out_specs=pl.BlockSpec((tm, d), lambda i: (i, 0)),
        out_shape=jax.ShapeDtypeStruct((n, d), F32),
        compiler_params=_cparams("parallel"),
        name="out_proj",
    )(*acts, *ws, x)


def _rg_gates(xf, wa, ba, wx, bx, lam):
    xb = xf.astype(BF16)
    r = jax.nn.sigmoid(_dot(xb, wa) + ba)
    i = jax.nn.sigmoid(_dot(xb, wx) + bx)
    log_a = -RG_C * r * _softplus(-lam)
    a = jnp.exp(log_a)
    b = jnp.sqrt(_neg_expm1(2.0 * log_a)) * (i * xf)
    return a, b


def _rglru_prompt_body(xr_ref, yg_ref, wc_ref, bc_ref, wa_ref, ba_ref, wx_ref, bx_ref, lam_ref,
                       rec_ref, hl_ref, xc_scr, h_scr, *, rt):
    t = pl.program_id(1)

    @pl.when(t == 0)
    def _():
        xc_scr[0:8, :] = jnp.zeros((8, D_RNN), F32)
        h_scr[...] = jnp.zeros_like(h_scr)

    xc_scr[8:8 + rt, :] = xr_ref[0]
    wc = wc_ref[...]
    conv = wc[0:1] * xc_scr[pl.ds(5, rt), :]
    for j in range(1, CONV_W):
        conv = conv + wc[j:j + 1] * xc_scr[pl.ds(5 + j, rt), :]
    xc_scr[0:8, :] = xc_scr[rt:rt + 8, :]
    xf = conv + bc_ref[...]
    a, b = _rg_gates(xf, wa_ref[...], ba_ref[...], wx_ref[...], bx_ref[...], lam_ref[...])
    s = 1
    while s < rt:
        b = a * _shift_rows(b, s, 0.0) + b
        a = a * _shift_rows(a, s, 1.0)
        s *= 2
    h = a * h_scr[0:1, :] + b
    h_last = h[rt - 1:rt, :]
    h_scr[0:1, :] = h_last
    hl_ref[0] = h_last
    rec_ref[0] = (h * _gelu(yg_ref[0])).astype(rec_ref.dtype)


def _rglru_prompt(xr, yg, wc, bc, wa, ba, wx, bx, lam, rt=RG_ROWS):
    b, t, d = xr.shape
    vec = lambda: pl.BlockSpec((1, d), lambda i, j: (0, 0))
    mat = lambda: pl.BlockSpec((d, d), lambda i, j: (0, 0))
    rec, hl = pl.pallas_call(
        functools.partial(_rglru_prompt_body, rt=rt),
        grid=(b, t // rt),
        in_specs=[pl.BlockSpec((1, rt, d), lambda i, j: (i, j, 0)),
                  pl.BlockSpec((1, rt, d), lambda i, j: (i, j, 0)),
                  pl.BlockSpec((CONV_W, d), lambda i, j: (0, 0)),
                  vec(), mat(), vec(), mat(), vec(), vec()],
        out_specs=[pl.BlockSpec((1, rt, d), lambda i, j: (i, j, 0)),
                   pl.BlockSpec((1, 1, d), lambda i, j: (i, 0, 0))],
        out_shape=[jax.ShapeDtypeStruct((b, t, d), BF16),
                   jax.ShapeDtypeStruct((b, 1, d), F32)],
        scratch_shapes=[pltpu.VMEM((rt + 8, d), F32), pltpu.VMEM((8, d), F32)],
        compiler_params=_cparams("arbitrary", "arbitrary"),
        name="rglru_prompt",
    )(xr, yg, wc, bc, wa, ba, wx, bx, lam)
    return rec, hl[:, 0]


def _rglru_sample_body(xr_ref, yg_ref, buf_ref, h0_ref, wc_ref, bc_ref, wa_ref, ba_ref, wx_ref,
                       bx_ref, lam_ref, rec_ref, hl_ref, *, t_len):
    xs = [buf_ref[j] for j in range(CONV_W - 1)] + [xr_ref[j] for j in range(t_len)]
    wc = wc_ref[...]
    h = h0_ref[...]
    for t in range(t_len):
        conv = wc[0:1] * xs[t]
        for j in range(1, CONV_W):
            conv = conv + wc[j:j + 1] * xs[t + j]
        xf = conv + bc_ref[...]
        a, b = _rg_gates(xf, wa_ref[...], ba_ref[...], wx_ref[...], bx_ref[...], lam_ref[...])
        h = a * h + b
        rec_ref[t] = h * _gelu(yg_ref[t])
    hl_ref[...] = h


def _rglru_sample(xr, yg, buf, h0, wc, bc, wa, ba, wx, bx, lam):
    t_len, b, d = xr.shape
    return pl.pallas_call(
        functools.partial(_rglru_sample_body, t_len=t_len),
        out_shape=[jax.ShapeDtypeStruct((t_len, b, d), F32),
                   jax.ShapeDtypeStruct((b, d), F32)],
        compiler_params=pltpu.CompilerParams(vmem_limit_bytes=VMEM_LIMIT),
        name="rglru_sample",
    )(xr, yg, buf, h0, wc, bc, wa, ba, wx, bx, lam)


def _pool_blocks(x3, w):
    first = jnp.sum(x3[:, :CMP_BLOCK, :] * w[None], axis=1)
    second = jnp.sum(x3[:, CMP_BLOCK:, :] * w[None], axis=1)
    return first, second


def _nsa_pool_body(x_ref, w_ref, pe_ref, po_ref):
    x = x_ref[0]
    nb = x.shape[0] // SEL_BLOCK
    pe, po = _pool_blocks(x.reshape(nb, SEL_BLOCK, x.shape[1]), w_ref[...])
    pe_ref[0] = pe
    po_ref[0] = po


def _nsa_pool(kv, w, rows=POOL_ROWS):
    b, t, _ = kv.shape
    nb = rows // SEL_BLOCK
    return pl.pallas_call(
        _nsa_pool_body,
        grid=(b, t // rows),
        in_specs=[pl.BlockSpec((1, rows, 256), lambda i, j: (i, j, 0)),
                  pl.BlockSpec((CMP_BLOCK, 256), lambda i, j: (0, 0))],
        out_specs=[pl.BlockSpec((1, nb, 256), lambda i, j: (i, j, 0))] * 2,
        out_shape=[jax.ShapeDtypeStruct((b, t // SEL_BLOCK, 256), F32)] * 2,
        compiler_params=_cparams("parallel", "parallel"),
        name="nsa_pool",
    )(kv, w)


def _cmp_branch(q, qpos, pe, po, nq):
    jl = lax.broadcasted_iota(jnp.int32, (1, N_SEL_LANES), 1)
    vis_e = (SEL_BLOCK * jl + (CMP_BLOCK - 1)) <= qpos
    vis_o = (SEL_BLOCK * jl + (SEL_BLOCK - 1)) <= qpos
    s_e = jnp.where(vis_e, _dot_nt(q, pe[:, :128].astype(BF16)), NEG_INF)
    s_o = jnp.where(vis_o, _dot_nt(q, po[:, :128].astype(BF16)), NEG_INF)
    m = jnp.maximum(jnp.max(s_e, -1, keepdims=True), jnp.max(s_o, -1, keepdims=True))
    p_e = jnp.where(vis_e, jnp.exp(s_e - m), 0.0)
    p_o = jnp.where(vis_o, jnp.exp(s_o - m), 0.0)
    den = jnp.sum(p_e, -1, keepdims=True) + jnp.sum(p_o, -1, keepdims=True)
    inv = jnp.where(den > 0.0, 1.0 / jnp.maximum(den, 1e-30), 0.0)
    p_e = p_e * inv
    p_o = p_o * inv
    o = _dot(p_e.astype(BF16), pe[:, 128:].astype(BF16)) + _dot(p_o.astype(BF16), po[:, 128:].astype(BF16))
    p = p_e + p_o
    imp = p[0:nq]
    for r in range(1, R_A):
        imp = imp + p[r * nq:(r + 1) * nq]
    return o, imp


def _select_blocks(imps, qpos, blk_axis):
    shape = (N_SEL_LANES, 1) if blk_axis == 0 else (1, N_SEL_LANES)
    blk = lax.broadcasted_iota(jnp.int32, shape, blk_axis)
    valid = blk * SEL_BLOCK <= qpos
    bonus = jnp.where(jnp.logical_or(blk == 0, blk == (qpos >> 6)), FORCE_BONUS, 0.0)
    scores = [jnp.where(valid, imp + bonus, -jnp.inf) for imp in imps]
    poss, _ = _topk_pos_multi(scores, blk_axis, TOP_N)
    return [jnp.where(valid, jnp.where(pos < TOP_N, 1.0, 0.0), 0.0) for pos in poss]


def _expand_sel(sel, first_blk, nkeys):
    bi = lax.broadcasted_iota(jnp.int32, (N_SEL_LANES, nkeys), 0)
    ki = lax.broadcasted_iota(jnp.int32, (N_SEL_LANES, nkeys), 1)
    e = jnp.where(bi - first_blk == (ki >> 6), 1.0, 0.0).astype(BF16)
    x = _dot(sel.astype(BF16), e)
    return jnp.concatenate([x] * R_A, axis=0)


def _masked_attend(q, k, v, ok):
    s = jnp.where(ok, _dot_nt(q, k), NEG_INF)
    m = jnp.max(s, -1, keepdims=True)
    p = jnp.where(ok, jnp.exp(s - m), 0.0)
    l = jnp.sum(p, -1, keepdims=True)
    return _dot(p.astype(BF16), v) / l


def _window_ok(qpos, kpos):
    d = qpos - kpos
    return jnp.logical_and(d >= 0, d <= WINDOW)


def _gate_and_place(gt, branches):
    heads = []
    for h in range(H_A):
        o = None
        for br in range(3):
            term = gt[:, 3 * h + br:3 * h + br + 1] * branches[br][h]
            o = term if o is None else o + term
        if (h % 2) != h // R_A:
            o = pltpu.roll(o, DH_A, 1)
        heads.append(o)
    lane = lax.broadcasted_iota(jnp.int32, (1, 128), 1)
    return [jnp.where(lane < DH_A, heads[2 * j], heads[2 * j + 1]) for j in range(H_A // 2)]


def _split_heads(o, nq):
    return [o[r * nq:(r + 1) * nq] for r in range(R_A)]


SEL_CHUNK = 512


def _nsa_prompt_body(q_ref, gt_ref, pe_ref, po_ref, kv_ref, oh_ref, vt_ref, o_ref, acc_scr):
    i = pl.program_id(1)
    nq = Q_BLOCK
    base = i * nq
    pe, po = pe_ref[0], po_ref[0]
    rowi = lax.broadcasted_iota(jnp.int32, (R_A * nq, 1), 0)
    qpos = base + (rowi & (nq - 1))
    qpos_row = base + lax.broadcasted_iota(jnp.int32, (1, nq), 1)
    heads_c, heads_w, qats, qs, imps_t = [], [], [], [], []
    for g in range(G_A):
        q = jnp.concatenate([q_ref[0, :, (g * R_A + r) * 128:(g * R_A + r + 1) * 128] for r in range(R_A)], axis=0)
        q = q * jnp.asarray(DH_A ** -0.5, BF16)
        o_c, imp = _cmp_branch(q, qpos, pe, po, nq)
        heads_c += _split_heads(o_c, nq)
        qs.append(q)
        imps_t.append(imp.T)
    sels_t = _select_blocks(imps_t, qpos_row, 0)
    for g in range(G_A):
        q = qs[g]
        selterm_t = jnp.where(sels_t[g] > 0.5, 0.0, NEG_INF).astype(BF16)
        qh_t = [jnp.concatenate([qh.astype(F32).T.astype(BF16), selterm_t], axis=0) for qh in _split_heads(q, nq)]
        qats += [jnp.concatenate(qh_t[2 * j:2 * j + 2], axis=1) for j in range(R_A // 2)]

        wlen = WINDOW + nq
        ws = pl.multiple_of(jnp.maximum(i - WINDOW // nq, 0) * nq, nq)
        kw = kv_ref[0, pl.ds(ws, wlen), 256:384]
        vw = kv_ref[0, pl.ds(ws, wlen), 384:512]
        kpos = ws + lax.broadcasted_iota(jnp.int32, (1, wlen), 1)
        heads_w += _split_heads(_masked_attend(q, kw, vw, _window_ok(qpos, kpos)), nq)

    acc_scr[...] = jnp.zeros_like(acc_scr)

    def step(k0, ms, ls, diagonal):
        ka = jnp.concatenate([kv_ref[0, pl.ds(k0, SEL_CHUNK), 0:128], oh_ref[pl.ds(k0, SEL_CHUNK), :]], axis=1)
        vt = vt_ref[0, :, pl.ds(k0, SEL_CHUNK)]
        if diagonal:
            qpos_pair = base + (lax.broadcasted_iota(jnp.int32, (1, 2 * nq), 1) & (nq - 1))
            visible = k0 + lax.broadcasted_iota(jnp.int32, (SEL_CHUNK, 1), 0) <= qpos_pair
        ss = [_dot(ka, qat) for qat in qats]
        if diagonal:
            ss = [jnp.where(visible, s, NEG_INF) for s in ss]
        ms_new = [jnp.maximum(m, jnp.max(s, axis=0, keepdims=True)) for m, s in zip(ms, ss)]
        ps = [jnp.exp(s - m) for s, m in zip(ss, ms_new)]
        pvs = [_dot(vt, p.astype(BF16)) for p in ps]
        ls_new = []
        for j in range(n_pairs):
            alpha = jnp.exp(ms[j] - ms_new[j])
            ls_new.append(alpha * ls[j] + jnp.sum(ps[j], axis=0, keepdims=True))
            acc_scr[j] = alpha * acc_scr[j] + pvs[j]
        return tuple(ms_new), tuple(ls_new)

    def past_chunk(c, carry):
        return step(pl.multiple_of(c * SEL_CHUNK, SEL_CHUNK), carry[0], carry[1], False)

    n_pairs = H_A // 2
    init = (tuple(jnp.full((1, 2 * nq), NEG_INF, F32) for _ in range(n_pairs)),
            tuple(jnp.zeros((1, 2 * nq), F32) for _ in range(n_pairs)))
    n_past = (base + nq - 1) // SEL_CHUNK
    ms, ls = lax.fori_loop(0, n_past, past_chunk, init)
    _, ls = step(pl.multiple_of(n_past * SEL_CHUNK, SEL_CHUNK), ms, ls, True)
    heads_s = []
    for j in range(n_pairs):
        o_pair = acc_scr[j] / ls[j]
        heads_s += [o_pair[:, 0:nq].T, o_pair[:, nq:2 * nq].T]
    for j, tile in enumerate(_gate_and_place(gt_ref[0], (heads_c, heads_s, heads_w))):
        o_ref[0, :, j * 128:(j + 1) * 128] = tile.astype(o_ref.dtype)


def _nsa_prompt(qpad, gates, pe, po, kvb):
    b, t, _ = qpad.shape
    nq = Q_BLOCK
    blk_of_key = jnp.arange(t, dtype=jnp.int32)[:, None] // SEL_BLOCK
    onehot = (blk_of_key == jnp.arange(N_SEL_LANES, dtype=jnp.int32)[None, :]).astype(BF16)
    v_sel_t = kvb[:, :, 128:256].transpose(0, 2, 1)
    return pl.pallas_call(
        _nsa_prompt_body,
        grid=(b, t // nq),
        in_specs=[pl.BlockSpec((1, nq, H_A * 128), lambda i, j: (i, j, 0)),
                  pl.BlockSpec((1, nq, 128), lambda i, j: (i, j, 0)),
                  pl.BlockSpec((1, N_SEL_LANES, 256), lambda i, j: (i, 0, 0)),
                  pl.BlockSpec((1, N_SEL_LANES, 256), lambda i, j: (i, 0, 0)),
                  pl.BlockSpec((1, t, 512), lambda i, j: (i, 0, 0)),
                  pl.BlockSpec((t, N_SEL_LANES), lambda i, j: (0, 0)),
                  pl.BlockSpec((1, 128, t), lambda i, j: (i, 0, 0))],
        out_specs=pl.BlockSpec((1, nq, H_A * DH_A), lambda i, j: (i, j, 0)),
        out_shape=jax.ShapeDtypeStruct((b, t, H_A * DH_A), BF16),
        scratch_shapes=[pltpu.VMEM((H_A // 2, 128, 2 * nq), F32)],
        compiler_params=_cparams("parallel", "arbitrary"),
        name="nsa_prompt",
    )(qpad, gates, pe, po, kvb, onehot, v_sel_t)


def _nsa_sample_body(pt_ref, q_ref, gt_ref, kvn_ref, w_ref, win_ref, *rest, n_pages, page, past, tq):
    del pt_ref
    cmp_pages, sel_pages = rest[:n_pages], rest[n_pages:2 * n_pages]
    o_ref, wout_ref, cmp_scr, sel_scr, win_scr = rest[2 * n_pages:]
    nq = q_ref.shape[1]
    npad = kvn_ref.shape[1]
    lp = cmp_scr.shape[0]
    wbuf = win_ref.shape[1]
    kvn = kvn_ref[0]
    for p in range(n_pages):
        cmp_scr[p * page:(p + 1) * page, :] = cmp_pages[p][0]
        sel_scr[p * page:(p + 1) * page, :] = sel_pages[p][0].astype(BF16)
    cmp_scr[past:past + npad, :] = kvn[:, 0:256]
    sel_scr[past:past + npad, :] = kvn[:, 256:512].astype(BF16)
    cmp_scr[past + npad:lp, :] = jnp.zeros((lp - past - npad, 256), F32)
    sel_scr[past + npad:lp, :] = jnp.zeros((lp - past - npad, 256), BF16)
    win_scr[0:wbuf, :] = win_ref[0]
    win_scr[wbuf:wbuf + npad, :] = kvn[:, 512:768]
    wl = win_scr.shape[0]
    win_scr[wbuf + npad:wl, :] = jnp.zeros((wl - wbuf - npad, 256), F32)
    wout_ref[0] = win_scr[pl.ds(tq, wbuf), :]

    nb = lp // SEL_BLOCK
    pe, po = _pool_blocks(cmp_scr[...].reshape(nb, SEL_BLOCK, 256), w_ref[...])
    zpad = jnp.zeros((N_SEL_LANES - nb, 256), F32)
    pe = jnp.concatenate([pe, zpad], axis=0)
    po = jnp.concatenate([po, zpad], axis=0)

    rowi = lax.broadcasted_iota(jnp.int32, (R_A * nq, 1), 0)
    qpos = past + (rowi & (nq - 1))
    nsel_keys = past + 256
    kpos_s = lax.broadcasted_iota(jnp.int32, (1, nsel_keys), 1)
    kpos_w = (past - wbuf) + lax.broadcasted_iota(jnp.int32, (1, wl), 1)
    heads_c, heads_s, heads_w = [], [], []
    qs, imps = [], []
    for g in range(G_A):
        q = jnp.concatenate([q_ref[0, :, (g * R_A + r) * 128:(g * R_A + r + 1) * 128] for r in range(R_A)], axis=0)
        q = (q * DH_A ** -0.5).astype(BF16)
        o_c, imp = _cmp_branch(q, qpos, pe, po, nq)
        heads_c += _split_heads(o_c, nq)
        qs.append(q)
        imps.append(imp)
    n_sel_q = G_A * nq
    imp_pad = jnp.concatenate(imps + [jnp.zeros((N_SEL_LANES - n_sel_q, N_SEL_LANES), F32)], axis=0)
    qpos_lane = past + (lax.broadcasted_iota(jnp.int32, (1, N_SEL_LANES), 1) & (nq - 1))
    sel_all = _select_blocks([imp_pad.T], qpos_lane, 0)[0].T[0:n_sel_q]
    for g in range(G_A):
        q, sel = qs[g], sel_all[g * nq:(g + 1) * nq]
        ok = jnp.logical_and(_expand_sel(sel, 0, nsel_keys) > 0.5, kpos_s <= qpos)
        o_s = _masked_attend(q, sel_scr[0:nsel_keys, 0:128], sel_scr[0:nsel_keys, 128:256], ok)
        kw = win_scr[:, 0:128].astype(BF16)
        vw = win_scr[:, 128:256].astype(BF16)
        o_w = _masked_attend(q, kw, vw, _window_ok(qpos, kpos_w))
        heads_s += _split_heads(o_s, nq)
        heads_w += _split_heads(o_w, nq)
    for j, tile in enumerate(_gate_and_place(gt_ref[0], (heads_c, heads_s, heads_w))):
        o_ref[0, :, j * 128:(j + 1) * 128] = tile


def _nsa_sample(page_table, q8, gates8, kvn16, w, win, cmp_pages, sel_pages, tq):
    b, n_pages = page_table.shape
    page = cmp_pages.shape[1]
    past = n_pages * page
    wbuf = win.shape[1]
    lp = past + 512
    nq = q8.shape[1]

    def row(shape):
        return pl.BlockSpec((1,) + shape, lambda i, pt: (i, 0, 0))

    def page_spec(p):
        return pl.BlockSpec((1, page, 256), lambda i, pt, p=p: (pt[i, p], 0, 0))

    grid_spec = pltpu.PrefetchScalarGridSpec(
        num_scalar_prefetch=1,
        grid=(b,),
        in_specs=([row((nq, H_A * 128)), row((nq, 128)), row((kvn16.shape[1], 768)),
                   pl.BlockSpec((CMP_BLOCK, 256), lambda i, pt: (0, 0)), row((wbuf, 256))]
                  + [page_spec(p) for p in range(n_pages)] * 2),
        out_specs=[row((nq, H_A * DH_A)), row((wbuf, 256))],
        scratch_shapes=[pltpu.VMEM((lp, 256), F32), pltpu.VMEM((lp, 256), BF16),
                        pltpu.VMEM((wbuf + 128, 256), F32)],
    )
    return pl.pallas_call(
        functools.partial(_nsa_sample_body, n_pages=n_pages, page=page, past=past, tq=tq),
        grid_spec=grid_spec,
        out_shape=[jax.ShapeDtypeStruct((b, nq, H_A * DH_A), F32),
                   jax.ShapeDtypeStruct((b, wbuf, 256), F32)],
        compiler_params=_cparams("arbitrary"),
        name="nsa_sample",
    )(page_table, q8, gates8, kvn16, w, win, *([cmp_pages] * n_pages), *([sel_pages] * n_pages))


def _cast_transpose_body(x_ref, o_ref):
    o_ref[...] = x_ref[...].T.astype(o_ref.dtype)


def _cast_transpose(x, rows=PROJ_ROWS):
    r, c = x.shape
    return pl.pallas_call(
        _cast_transpose_body,
        grid=(r // rows,),
        in_specs=[pl.BlockSpec((rows, c), lambda i: (i, 0))],
        out_specs=pl.BlockSpec((c, rows), lambda i: (0, i)),
        out_shape=jax.ShapeDtypeStruct((c, r), BF16),
        compiler_params=_cparams("parallel"),
        name="cast_transpose",
    )(x)


def _comb_candidates(v1, v2):
    k, half, lanes = P_TOPK, P_TOPK // 2, v1.shape[1]
    assert k == 16
    qi = lax.broadcasted_iota(jnp.int32, (half, lanes), 0).astype(F32)
    sums = [v1[0:1] + v2]
    idx = [lax.broadcasted_iota(jnp.int32, (k, lanes), 0).astype(F32)]
    for p in range(1, half):
        sums.append(v1[p:p + 1] + v2[0:half])
        idx.append(qi + float(p * k))
    sums.append(v1[half:k] + v2[0:1])
    idx.append((qi + float(half)) * float(k))
    return jnp.concatenate(sums, axis=0), jnp.concatenate(idx, axis=0)


def _peer_select(s1s, s2s):
    k, half = P_TOPK, P_TOPK // 2
    nt = len(s1s)
    scores = list(s1s) + list(s2s)

    def both_stages(topk):
        poss, vals, extra = topk(scores, None)
        combs = [_comb_candidates(vals[t], vals[nt + t]) for t in range(nt)]
        poscs, vcs, extra_c = topk([cb[0] for cb in combs], combs[0][1])
        return (tuple(poss), tuple(vals), tuple(poscs), tuple(vcs)), extra + extra_c

    untied, extra = both_stages(lambda ss, idx: _topk_pos_multi_untied(ss, 0, k))
    poss, vals, poscs, vcs = lax.cond(
        extra == 0.0, lambda: untied,
        lambda: both_stages(lambda ss, idx: _topk_pos_multi(ss, 0, k, idx) + (0.0,))[0])
    out = []
    for t in range(nt):
        pos1, v1, pos2, v2 = poss[t], vals[t], poss[nt + t], vals[nt + t]
        z = jnp.sum(jnp.exp(vcs[t] - vcs[t][0:1]), axis=0, keepdims=True)
        sel = jnp.where(poscs[t] < k, 1.0, 0.0)
        n1 = jnp.where(pos1 == 0.0, jnp.sum(sel[0:k], axis=0, keepdims=True), 0.0)
        for p in range(1, half):
            r0 = k + half * (p - 1)
            n1 = jnp.where(pos1 == float(p), jnp.sum(sel[r0:r0 + half], axis=0, keepdims=True), n1)
        for p in range(half, k):
            r0 = k + half * (half - 1) + (p - half)
            n1 = jnp.where(pos1 == float(p), sel[r0:r0 + 1], n1)
        f1 = jnp.where(pos1 < k, jnp.exp(s1s[t] - v1[0:1]), 0.0) / z
        out.append((n1, f1, pos2, jnp.exp(s2s[t] - v2[0:1])))
    return out


def _peer_body(x_ref, g_ref, gf_ref, wq_ref, keys_ref, u_ref, vt_ref, o_ref,
               hnt_scr, n1_scr, f1_scr, q2_scr, e2_scr, acc_scr, h_even, h_odd, *, ec, final_norm):
    c = pl.program_id(1)
    tm = x_ref.shape[0]

    @pl.when(c == 0)
    def _select():
        hn = _rmsnorm(x_ref[...], g_ref[...])
        hnt_scr[...] = hn.T.astype(BF16)
        acc_scr[...] = jnp.zeros_like(acc_scr)

        def head(h, carry):
            r0 = pl.multiple_of(h * 2 * P_DHALF, 2 * P_DHALF)
            qt = _dot(wq_ref[pl.ds(r0, 2 * P_DHALF), :], hnt_scr[...]).astype(BF16)
            lane_tiles = [slice(lt * 128, (lt + 1) * 128) for lt in range(tm // 128)]
            s1s = [_dot(keys_ref[2 * h], qt[0:P_DHALF, sl]) for sl in lane_tiles]
            s2s = [_dot(keys_ref[2 * h + 1], qt[P_DHALF:2 * P_DHALF, sl]) for sl in lane_tiles]
            for sl, (n1, f1, q2, e2) in zip(lane_tiles, _peer_select(s1s, s2s)):
                n1_scr[h, :, sl] = n1
                f1_scr[h, :, sl] = f1
                q2_scr[h, :, sl] = q2.astype(BF16)
                e2_scr[h, :, sl] = e2.astype(BF16)
            return carry

        lax.fori_loop(0, P_HEADS, head, 0)

    nc = N_EXP // ec
    assert nc % 2 == 0

    def weighted_acts(h_out):
        act = _gelu(_dot(u_ref[...], hnt_scr[...]).astype(BF16))
        for ai in range(ec // N_KEYS):
            a = c * (ec // N_KEYS) + ai
            w = None
            for h in range(P_HEADS):
                n1 = n1_scr[h, pl.ds(a, 1), :].astype(BF16)
                f1 = f1_scr[h, pl.ds(a, 1), :].astype(BF16)
                term = jnp.where(q2_scr[h] < n1, e2_scr[h] * f1, jnp.zeros((), BF16))
                w = term if w is None else w + term
            rows = slice(ai * N_KEYS, (ai + 1) * N_KEYS)
            h_out[rows, :] = w * act[rows]

    @pl.when(c == 0)
    def _first():
        weighted_acts(h_even)

    @pl.when(jnp.logical_and(c < nc, c % 2 == 1))
    def _odd():
        acc_scr[...] += _dot(vt_ref[...], h_even[...])
        weighted_acts(h_odd)

    @pl.when(jnp.logical_and(jnp.logical_and(c > 0, c < nc), c % 2 == 0))
    def _even():
        acc_scr[...] += _dot(vt_ref[...], h_odd[...])
        weighted_acts(h_even)

    @pl.when(c == nc)
    def _finish():
        y = x_ref[...] + (acc_scr[...] + _dot(vt_ref[...], h_odd[...])).T
        if final_norm:
            y = _rmsnorm(y, gf_ref[...])
        o_ref[...] = y


def _peer(x, g, gf, wq_t, keys, u, vt, final_norm, tm=PEER_TOKENS, ec=PEER_EXPERTS):
    n, d = x.shape
    nc = N_EXP // ec
    const2 = lambda i, c: (0, 0)
    return pl.pallas_call(
        functools.partial(_peer_body, ec=ec, final_norm=final_norm),
        grid=(n // tm, nc + 1),
        in_specs=[pl.BlockSpec((tm, d), lambda i, c: (i, 0)),
                  pl.BlockSpec((1, d), const2), pl.BlockSpec((1, d), const2),
                  pl.BlockSpec(wq_t.shape, const2),
                  pl.BlockSpec(keys.shape, lambda i, c: (0, 0, 0)),
                  pl.BlockSpec((ec, d), lambda i, c: (jnp.minimum(c, nc - 1), 0)),
                  pl.BlockSpec((d, ec), lambda i, c: (0, jnp.maximum(c - 1, 0)))],
        out_specs=pl.BlockSpec((tm, d), lambda i, c: (i, 0)),
        out_shape=jax.ShapeDtypeStruct((n, d), F32),
        scratch_shapes=[pltpu.VMEM((d, tm), BF16)]
        + [pltpu.VMEM((P_HEADS, N_KEYS, tm), F32)] * 2
        + [pltpu.VMEM((P_HEADS, N_KEYS, tm), BF16)] * 2
        + [pltpu.VMEM((d, tm), F32), pltpu.VMEM((ec, tm), BF16), pltpu.VMEM((ec, tm), BF16)],
        compiler_params=_cparams("parallel", "arbitrary"),
        name="peer",
    )(x, g.reshape(1, d), gf.reshape(1, d), wq_t, keys, u, vt)


def _unit_lower_inverses(ms, order):
    ps = [-m for m in ms]
    invs = list(ps)
    dot = _dot_x3 if order > 8 else (lambda a, b: _dot(a.astype(BF16), b.astype(BF16)))
    span = 2
    while span < order:
        ps = [dot(p, p) for p in ps]
        invs = [inv + p + dot(inv, p) for inv, p in zip(invs, ps)]
        span *= 2
    row = lax.broadcasted_iota(jnp.int32, ms[0].shape, 0)
    col = lax.broadcasted_iota(jnp.int32, ms[0].shape, 1)
    eye = jnp.where(row == col, 1.0, 0.0)
    return [inv + eye for inv in invs]


def _gdn_body(qkv_ref, z_ref, ba_ref, cb_ref, s0_ref, wc_ref, prm_ref, gn_ref, o_ref, s_ref,
              xc_scr, s_scr, *, valid):
    t = pl.program_id(1)
    c = GDN_CHUNK
    r = qkv_ref.shape[1]

    @pl.when(t == 0)
    def _():
        xc_scr[0:8, :] = cb_ref[0]
        s_scr[...] = s0_ref[0]

    xc_scr[8:8 + r, :] = qkv_ref[0]
    if r < c:
        xc_scr[8 + r:8 + c, :] = jnp.zeros((c - r, QKV_C), F32)
    wc = wc_ref[...]
    rc = c if r == c else r + 8
    conv = wc[0:1] * xc_scr[pl.ds(5, rc), :]
    for j in range(1, CONV_W):
        conv = conv + wc[j:j + 1] * xc_scr[pl.ds(5 + j, rc), :]
    xc_scr[0:8, :] = xc_scr[c:c + 8, :]
    act = _silu(conv)
    if rc < c:
        act = jnp.concatenate([act, jnp.zeros((c - rc, QKV_C), F32)], axis=0)

    row = lax.broadcasted_iota(jnp.int32, (c, 1), 0)
    live = row < valid
    ba = ba_ref[0]
    if r < c:
        ba = jnp.concatenate([ba, jnp.zeros((c - r, 128), F32)], axis=0)
    prm = prm_ref[...]
    beta_all = jnp.where(live, jax.nn.sigmoid(ba), 0.0)
    g_all = jnp.where(live, prm[0:1] * _softplus(ba + prm[1:2]), 0.0)
    gc_all = g_all
    s = 1
    while s < c:
        gc_all = gc_all + _shift_rows(gc_all, s, 0.0)
        s *= 2
    gct = gc_all.T
    ri = lax.broadcasted_iota(jnp.int32, (c, c), 0)
    ci = lax.broadcasted_iota(jnp.int32, (c, c), 1)
    tri = ri >= ci
    strict = ri > ci
    z = z_ref[0]
    if r < c:
        z = jnp.concatenate([z, jnp.zeros((c - r, H_C * DV_C), F32)], axis=0)
    per_head = []
    for h in range(H_C):
        q = act[:, h * DK_C:(h + 1) * DK_C]
        k = act[:, (H_C + h) * DK_C:(H_C + h + 1) * DK_C]
        v = act[:, (2 * H_C + h) * DK_C:(2 * H_C + h + 1) * DK_C]
        q = q * lax.rsqrt(jnp.sum(q * q, -1, keepdims=True) + 1e-6) * (DK_C ** -0.5)
        k = k * lax.rsqrt(jnp.sum(k * k, -1, keepdims=True) + 1e-6)
        beta = beta_all[:, h:h + 1]
        gcol = gc_all[:, H_C + h:H_C + h + 1]
        grow = gct[H_C + h:H_C + h + 1, :]
        glast = gc_all[c - 1:c, H_C + h:H_C + h + 1]
        lmat = jnp.where(tri, jnp.exp(jnp.where(tri, gcol - grow, 0.0)), 0.0)
        kb = k * beta
        kbf = k.astype(BF16)
        m = jnp.where(strict, _dot_nt(kb.astype(BF16), kbf) * lmat, 0.0)
        eg = jnp.exp(gcol)
        aqk = (_dot_nt(q.astype(BF16), kbf) * lmat).astype(BF16)
        per_head.append(dict(m=m, vb=(v * beta).astype(BF16), kbe=(kb * eg).astype(BF16), aqk=aqk,
                             qd=(q * eg).astype(BF16), kdt=(k * jnp.exp(glast - gcol)).T.astype(BF16),
                             decay=jnp.exp(glast)))
    tinvs = _unit_lower_inverses([ph["m"] for ph in per_head], min(valid, c))
    for h, (ph, tinv) in enumerate(zip(per_head, tinvs)):
        tinv = tinv.astype(BF16)
        u = _dot(tinv, ph["vb"])
        w = _dot(tinv, ph["kbe"])
        st = s_scr[h]
        stb = st.astype(BF16)
        v_new = u - _dot(w.astype(BF16), stb)
        o = _dot(ph["qd"], stb) + _dot(ph["aqk"], v_new.astype(BF16))
        s_scr[h] = st * ph["decay"] + _dot(ph["kdt"], v_new.astype(BF16))
        o = o * lax.rsqrt(jnp.mean(o * o, -1, keepdims=True) + EPS) * gn_ref[...]
        o = o * _silu(z[:, h * DV_C:(h + 1) * DV_C])
        o_ref[0, :, h * DV_C:(h + 1) * DV_C] = o[0:r].astype(o_ref.dtype)

    @pl.when(t == pl.num_programs(1) - 1)
    def _():
        s_ref[0] = s_scr[...]


def _gdn(qkv, z, ba, cbuf8, s0, wc, prm, gn, valid):
    b, t, _ = qkv.shape
    r = min(t, GDN_CHUNK)
    tile = lambda wd: pl.BlockSpec((1, r, wd), lambda i, j: (i, j, 0))
    return pl.pallas_call(
        functools.partial(_gdn_body, valid=valid),
        grid=(b, t // r),
        in_specs=[tile(QKV_C), tile(H_C * DV_C), tile(128),
                  pl.BlockSpec((1, 8, QKV_C), lambda i, j: (i, 0, 0)),
                  pl.BlockSpec((1, H_C, DK_C, DV_C), lambda i, j: (i, 0, 0, 0)),
                  pl.BlockSpec((CONV_W, QKV_C), lambda i, j: (0, 0)),
                  pl.BlockSpec((8, 128), lambda i, j: (0, 0)),
                  pl.BlockSpec((1, DV_C), lambda i, j: (0, 0))],
        out_specs=[tile(H_C * DV_C),
                   pl.BlockSpec((1, H_C, DK_C, DV_C), lambda i, j: (i, 0, 0, 0))],
        out_shape=[jax.ShapeDtypeStruct((b, t, H_C * DV_C), F32 if r < GDN_CHUNK else BF16),
                   jax.ShapeDtypeStruct((b, H_C, DK_C, DV_C), F32)],
        scratch_shapes=[pltpu.VMEM((GDN_CHUNK + 8, QKV_C), F32), pltpu.VMEM((H_C, DK_C, DV_C), F32)],
        compiler_params=_cparams("parallel", "arbitrary"),
        name="gdn",
    )(qkv, z, ba, cbuf8, s0, wc, prm, gn)


def _block_diag(w):
    n, d, e = w.shape
    eye = jnp.eye(n, dtype=w.dtype)
    return (w[:, :, None, :] * eye[:, None, :, None]).reshape(n * d, n * e)


def _pad_rows(x, rows, front=0):
    return jnp.pad(x, ((0, 0), (front, rows - x.shape[1] - front), (0, 0)))


def _layer_a_weights(w_in, w_pool):
    d = w_in.shape[0]
    q_a, kv_a = H_A * DH_A, 3 * 2 * G_A * DH_A
    wq = w_in[:, :q_a].reshape(d, G_A, R_A, DH_A)
    zeros = jnp.zeros_like(wq)
    wq = jnp.stack([jnp.concatenate([wq[:, 0], zeros[:, 0]], -1),
                    jnp.concatenate([zeros[:, 1], wq[:, 1]], -1)], axis=1).reshape(d, H_A * 128)
    i1 = q_a + kv_a
    i2 = i1 + 3 * H_A
    wg = jnp.pad(w_in[:, i1:i2], ((0, 0), (0, 128 - 3 * H_A)))
    w = jnp.concatenate([wq, w_in[:, q_a:i1], wg, w_in[:, i2:]], axis=1).astype(BF16)
    wp = jnp.concatenate([jnp.broadcast_to(w_pool[0][:, None], (CMP_BLOCK, 128)),
                          jnp.broadcast_to(w_pool[1][:, None], (CMP_BLOCK, 128))], axis=1)
    return w, wp


def kernel(x_prompt, x_sample, cache_cmp_kv, cache_sel_kv, cache_win_kv, state_rg_h, state_rg_conv, state_gdn_S, state_gdn_conv, page_table, norm_mix, norm_ffn, norm_final, w_in_a, w_cmp_pool, w_rg_conv, b_rg_conv, w_rg_a, b_rg_a, w_rg_x, b_rg_x, rg_lambda, w_out_a, w_in_c, w_c_conv, c_A_log, c_dt_bias, c_norm, w_out_c, peer_w_q, peer_keys, peer_u, peer_v):
    bp, tp, d = x_prompt.shape
    bs, ts, _ = x_sample.shape
    np_, ns = bp * tp, bs * ts
    xp = x_prompt.reshape(np_, d)
    xs = x_sample.reshape(ns, d)
    kv_w = 3 * 2 * G_A * DH_A

    def peer_layer(x, layer, final):
        wq_t = _cast_transpose(peer_w_q[layer])
        keys = peer_keys[layer].reshape(2 * P_HEADS, N_KEYS, P_DHALF).astype(BF16)
        u = peer_u[layer].astype(BF16)
        vt = _cast_transpose(peer_v[layer])
        return [_peer(xx, norm_ffn[layer], norm_final, wq_t, keys, u, vt, final) for xx in x]

    w0, wp = _layer_a_weights(w_in_a[0], w_cmp_pool[0])
    qo, ko, go, ro = 0, H_A * 128, H_A * 128 + kv_w, H_A * 128 + kv_w + 128
    outs = ((qo, H_A * 128, None), (ko, kv_w, None), (ko + 256, 512, None), (go, 128, "sigmoid"),
            (ro, D_RNN, None), (ro + D_RNN, D_RNN, None))
    dts = (BF16, F32, BF16, F32, F32, F32)
    qp, kvp, kvbp, gtp, xrp, ygp = _norm_proj(xp, norm_mix[0], w0, outs, dts)
    qs, kvs, _, gts, xrs, ygs = _norm_proj(xs, norm_mix[0], w0, outs, dts)

    kvp3 = kvp.reshape(bp, tp, kv_w)
    pe, po = _nsa_pool(kvp3, wp)
    att_p = _nsa_prompt(qp.reshape(bp, tp, H_A * 128), gtp.reshape(bp, tp, 128), pe, po,
                        kvbp.reshape(bp, tp, 512)).reshape(np_, H_A * DH_A)

    kvs3 = kvs.reshape(bs, ts, kv_w)
    n_pool, page = cache_cmp_kv.shape[1], cache_cmp_kv.shape[2]
    att_s8, win_new = _nsa_sample(
        page_table,
        _pad_rows(qs.astype(F32).reshape(bs, ts, H_A * 128), 8),
        _pad_rows(gts.reshape(bs, ts, 128), 8),
        _pad_rows(kvs3, 16), wp,
        cache_win_kv[0].reshape(bs, -1, 256),
        cache_cmp_kv[0].reshape(n_pool, page, 256),
        cache_sel_kv[0].reshape(n_pool, page, 256), ts)
    att_s = att_s8[:, :ts].reshape(ns, H_A * DH_A).astype(BF16)

    rg = (w_rg_conv[0], b_rg_conv[0].reshape(1, D_RNN), _block_diag(w_rg_a[0]).astype(BF16),
          b_rg_a[0].reshape(1, D_RNN), _block_diag(w_rg_x[0]).astype(BF16), b_rg_x[0].reshape(1, D_RNN),
          rg_lambda[0].reshape(1, D_RNN))
    xrp3 = xrp.reshape(bp, tp, D_RNN)
    rec_p, hl_p = _rglru_prompt(xrp3, ygp.reshape(bp, tp, D_RNN), *rg)
    xrs3 = xrs.reshape(bs, ts, D_RNN)
    rec_s, hl_s = _rglru_sample(xrs3.transpose(1, 0, 2), ygs.reshape(bs, ts, D_RNN).transpose(1, 0, 2),
                                state_rg_conv[0].transpose(1, 0, 2), state_rg_h[0], *rg)
    rec_s = rec_s.transpose(1, 0, 2).reshape(ns, D_RNN).astype(BF16)

    wo = w_out_a[0].astype(BF16)
    wo_att, wo_rec = wo[:H_A * DH_A], wo[H_A * DH_A:]
    xp = _out_proj([att_p, rec_p.reshape(np_, D_RNN)], [wo_att, wo_rec], xp)
    xs = _out_proj([att_s, rec_s], [wo_att, wo_rec], xs)
    xp, xs = peer_layer([xp, xs], 0, False)

    d_c = H_C * DV_C
    w1 = jnp.pad(w_in_c[0], ((0, 0), (0, 128 - 2 * H_C))).astype(BF16)
    outs1 = ((0, QKV_C, None), (QKV_C, d_c, None), (QKV_C + d_c, 128, None))
    dts1 = (F32, F32, F32)
    qkv_p, z_p, ba_p = _norm_proj(xp, norm_mix[1], w1, outs1, dts1)
    qkv_s, z_s, ba_s = _norm_proj(xs, norm_mix[1], w1, outs1, dts1)
    lane = jnp.arange(128)
    hsel = jnp.clip(lane - H_C, 0, H_C - 1)
    in_g = (lane >= H_C) & (lane < 2 * H_C)
    prm = jnp.zeros((8, 128), F32)
    prm = prm.at[0].set(jnp.where(in_g, -jnp.exp(c_A_log[0])[hsel], 0.0))
    prm = prm.at[1].set(jnp.where(in_g, c_dt_bias[0][hsel], 0.0))
    gn = c_norm[0].reshape(1, DV_C)
    qkv_p3 = qkv_p.reshape(bp, tp, QKV_C)
    o_p, gs_p = _gdn(qkv_p3, z_p.reshape(bp, tp, d_c), ba_p.reshape(bp, tp, 128),
                     jnp.zeros((bp, 8, QKV_C), F32), jnp.zeros((bp, H_C, DK_C, DV_C), F32),
                     w_c_conv[0], prm, gn, GDN_CHUNK)
    qkv_s3 = qkv_s.reshape(bs, ts, QKV_C)
    o_s, gs_s = _gdn(_pad_rows(qkv_s3, 8), _pad_rows(z_s.reshape(bs, ts, d_c), 8),
                     _pad_rows(ba_s.reshape(bs, ts, 128), 8),
                     _pad_rows(state_gdn_conv[0], 8, front=8 - (CONV_W - 1)), state_gdn_S[0],
                     w_c_conv[0], prm, gn, ts)
    wo_c = w_out_c[0].astype(BF16)
    xp = _out_proj([o_p.reshape(np_, d_c)], [wo_c], xp)
    xs = _out_proj([o_s[:, :ts].reshape(ns, d_c).astype(BF16)], [wo_c], xs)
    yp, ys = peer_layer([xp, xs], 1, True)

    kvp6 = kvp3.reshape(bp, tp, 3, 2, G_A, DH_A)
    kvs6 = kvs3.reshape(bs, ts, 3, 2, G_A, DH_A)
    wlen = min(WINDOW, tp)
    cw = CONV_W - 1
    return (yp.reshape(bp, tp, d), ys.reshape(bs, ts, d),
            kvp6[None, :, :, 0], kvs6[None, :, :, 0], kvp6[None, :, :, 1], kvs6[None, :, :, 1],
            kvp6[None, :, tp - wlen:, 2], win_new.reshape(bs, -1, 2, G_A, DH_A)[None],
            hl_p[None], hl_s[None], xrp3[None, :, tp - cw:], xrs3[None, :, ts - cw:],
            gs_p[None], gs_s[None], qkv_p3[None, :, tp - cw:], qkv_s3[None, :, ts - cw:])
```
